```python
import math
import jax, jax.numpy as jnp
from jax import lax
import numpy as np

D_MODEL = 1024
BATCH = 8
SEQ = 2048
DEPTH = 2

GRID_W = 64
CTX_LEN = 256
D_HYENA = D_MODEL
D_LRU = D_MODEL
D_SC = D_MODEL
HYENA_ORDER = 2
HYENA_CONV_W = 3
HYENA_EMB_DIM = 33
HYENA_FILTER_W = 64
HYENA_FAST_DECAY = 0.3
HYENA_SLOW_DECAY = 1.5
HYENA_TARGET = 1e-2
HYENA_MAX_DECAY = math.log(HYENA_TARGET) / HYENA_FAST_DECAY
HYENA_MIN_DECAY = math.log(HYENA_TARGET) / HYENA_SLOW_DECAY
LRU_HEADS = 16
LRU_BLOCK = D_LRU // LRU_HEADS
LRU_CONV_W = 4
LRU_C = 8.0
SC_CONV_W = 3
N_EXPERTS = 16
EC_CAPACITY = 2
D_EXPERT = 2816
NORM_EPS = 1e-6
OFF_LRU_GATE = 3 * D_HYENA
OFF_LRU_REC = OFF_LRU_GATE + D_LRU
OFF_SC = OFF_LRU_REC + D_LRU
OFF_GATE = OFF_SC + 3 * D_SC
D_IN_PROJ = OFF_GATE + 3 * D_MODEL

kernel_name = 'hybrid_hyena_rglru_shortconv_ecmoe_dit'


def rmsnorm(x, g):
    xf = x.astype(jnp.float32)
    y = xf * lax.rsqrt(jnp.mean(xf * xf, axis=-1, keepdims=True) + NORM_EPS)
    return (y * g.astype(jnp.float32)).astype(x.dtype)


def modulate(x, shift, scale):
    return x * (1.0 + scale) + shift


def dwconv(x, w, b, left):
    K = w.shape[0]
    L = x.shape[1]
    xp = jnp.pad(x, ((0, 0), (left, K - 1 - left), (0, 0)))
    y = sum(xp[:, k:k + L] * w[k] for k in range(K))
    return y if b is None else y + b


def sincos_1d(pos, dim):
    half = dim // 2
    omega = 1.0 / (10000.0 ** (jnp.arange(half, dtype=jnp.float32) / half))
    ang = pos[:, None] * omega[None, :]
    return jnp.concatenate([jnp.sin(ang), jnp.cos(ang)], axis=-1)


def grid_pos_embed(rows):
    half = D_MODEL // 2
    emb_r = sincos_1d(jnp.arange(rows, dtype=jnp.float32), half)
    emb_c = sincos_1d(jnp.arange(GRID_W, dtype=jnp.float32), half)
    emb = jnp.concatenate([jnp.broadcast_to(emb_r[:, None, :], (rows, GRID_W, half)),
                           jnp.broadcast_to(emb_c[None, :, :], (rows, GRID_W, half))], axis=-1)
    return emb.reshape(rows * GRID_W, D_MODEL)


def hyena_frequency_response(L, w1, b1, w2, b2, w3, freq):
    f32 = jnp.float32
    bands = (HYENA_EMB_DIM - 1) // 2
    t01 = jnp.linspace(0.0, 1.0, L, dtype=f32)[:, None]
    w = (2.0 * math.pi / L) * jnp.arange(L, dtype=f32)[:, None]
    f = jnp.linspace(1e-4, bands - 1, bands, dtype=f32)[None, :]
    feats = jnp.concatenate([t01, jnp.cos(f * w), -jnp.sin(f * w)], axis=-1)
    h = jnp.sin(freq * (feats @ w1 + b1))
    h = jnp.sin(freq * (h @ w2 + b2))
    h = (h @ w3).astype(f32).reshape(L, HYENA_ORDER, 2, D_HYENA)
    deltas = jnp.linspace(HYENA_MIN_DECAY, HYENA_MAX_DECAY, D_HYENA, dtype=f32)
    h = h * jnp.exp(-t01 * jnp.abs(deltas))[:, None, None, :]
    fwd = h[:, :, 0]
    bwd = h[:0:-1, :, 1]
    buf = jnp.concatenate([fwd, jnp.zeros_like(fwd[:1]), bwd], axis=0)
    buf = buf / jnp.sum(jnp.abs(buf), axis=0, keepdims=True)
    return jnp.fft.rfft(buf, axis=0)


def fft_conv(u, kf, bias):
    L = u.shape[1]
    uf32 = u.astype(jnp.float32)
    uf = jnp.fft.rfft(uf32, n=2 * L, axis=1)
    y = jnp.fft.irfft(uf * kf[None], n=2 * L, axis=1)[:, :L]
    return (y + uf32 * bias.astype(jnp.float32)).astype(u.dtype)


def hyena_mixer(proj, conv_w, conv_b, kf, bias):
    u = dwconv(proj, conv_w, conv_b, HYENA_CONV_W // 2)
    x1, x2, v = jnp.split(u, 3, axis=-1)
    z = x1 * fft_conv(v, kf[:, 0], bias[0])
    return x2 * fft_conv(z, kf[:, 1], bias[1])


def _linear_combine(p, q):
    a1, b1 = p
    a2, b2 = q
    return a1 * a2, a2 * b1 + b2


def rglru(xc, wa, ba, wx, bx, lam, h0, reverse):
    B, L, _ = xc.shape
    xf = xc.astype(jnp.float32)
    xb = xf.reshape(B, L, LRU_HEADS, LRU_BLOCK)
    r = jax.nn.sigmoid(jnp.einsum('blhi,hij->blhj', xb, wa).reshape(B, L, D_LRU) + ba)
    i = jax.nn.sigmoid(jnp.einsum('blhi,hij->blhj', xb, wx).reshape(B, L, D_LRU) + bx)
    log_a = -LRU_C * r * jax.nn.softplus(-lam.astype(jnp.float32))
    a = jnp.exp(log_a)
    b = jnp.sqrt(-jnp.expm1(2.0 * log_a)) * (i * xf)
    first = L - 1 if reverse else 0
    b = b.at[:, first].add(a[:, first] * h0.astype(jnp.float32))
    _, h = lax.associative_scan(_linear_combine, (a, b), reverse=reverse, axis=1)
    return h.astype(xc.dtype)


def lru_scans(rec_in, p, h0_f, h0_b):
    xc = dwconv(rec_in, p['lru_conv_w'], p['lru_conv_b'], LRU_CONV_W // 2)
    hf = rglru(xc, p['lru_wa'][0], p['lru_ba'][0], p['lru_wx'][0], p['lru_bx'][0], p['lru_lambda'][0], h0_f, False)
    hb = rglru(xc, p['lru_wa'][1], p['lru_ba'][1], p['lru_wx'][1], p['lru_bx'][1], p['lru_lambda'][1], h0_b, True)
    return hf, hb


def shortconv_mixer(proj, conv_w):
    bg, cg, xv = jnp.split(proj, 3, axis=-1)
    return bg * dwconv(cg * xv, conv_w, None, SC_CONV_W // 2)


def mixer_sublayer(h, p, h0_f, h0_b):
    L = h.shape[1]
    proj = h @ p['w_in']
    hy_in, lru_gate, lru_rec, sc_in, gate_logits = jnp.split(proj, [OFF_LRU_GATE, OFF_LRU_REC, OFF_SC, OFF_GATE], axis=-1)
    kf = hyena_frequency_response(L, p['hy_filt_w1'], p['hy_filt_b1'], p['hy_filt_w2'], p['hy_filt_b2'], p['hy_filt_w3'], p['hy_filt_freq'])
    y_a = hyena_mixer(hy_in, p['hy_conv_w'], p['hy_conv_b'], kf, p['hy_bias'])
    hf, hb = lru_scans(lru_rec, p, h0_f, h0_b)
    y_b = jax.nn.gelu(lru_gate) * (hf + hb)
    y_c = shortconv_mixer(sc_in, p['sc_conv_w'])
    g_a, g_b, g_c = jnp.split(jax.nn.sigmoid(gate_logits), 3, axis=-1)
    merged = g_a * (y_a @ p['w_hy_out']) + g_b * (y_b @ p['w_lru_out']) + g_c * (y_c @ p['w_sc_out'])
    return merged @ p['w_o'], hf[:, -1], hb[:, 0]


def ec_moe(h, w_router, wg, wu, wd):
    n, d = h.shape[1], h.shape[2]
    cap = EC_CAPACITY * n // N_EXPERTS

    def per_set(hs):
        aff = jax.nn.softmax((hs @ w_router).astype(jnp.float32), axis=-1)
        gate, idx = lax.top_k(aff.T, cap)
        xs = hs[idx]
        hid = jax.nn.silu(jnp.einsum('ecd,edf->ecf', xs, wg)) * jnp.einsum('ecd,edf->ecf', xs, wu)
        ys = jnp.einsum('ecf,efd->ecd', hid, wd) * gate[..., None].astype(hs.dtype)
        return jnp.zeros_like(hs).at[idx.reshape(-1)].add(ys.reshape(-1, d))

    return jax.vmap(per_set)(h)


def setup_inputs(seed: int = 0) -> dict:
    key = jax.random.key(seed)
    ks = iter(jax.random.split(key, 48))
    f32 = jnp.float32

    def nrm(shape, scale):
        return scale * jax.random.normal(next(ks), shape, f32)

    D = D_MODEL
    a0 = jax.random.uniform(next(ks), (DEPTH, 2, D_LRU), f32, 0.9, 0.999) ** (1.0 / LRU_C)
    lru_lambda = jnp.log(a0) - jnp.log1p(-a0)
    return {
        'x': nrm((BATCH, SEQ, D), 1.0),
        'c': nrm((BATCH, D), 1.0),
        'ctx': nrm((BATCH, CTX_LEN, D), 1.0),
        'c_ctx': nrm((D,), 1.0),
        'w_ada': nrm((DEPTH, D, 6 * D), 0.02),
        'b_ada': nrm((DEPTH, 6 * D), 0.02),
        'norm1_g': 1.0 + nrm((DEPTH, D), 0.05),
        'norm2_g': 1.0 + nrm((DEPTH, D), 0.05),
        'w_in': nrm((DEPTH, D, D_IN_PROJ), D ** -0.5),
        'hy_conv_w': nrm((DEPTH, HYENA_CONV_W, 3 * D_HYENA), HYENA_CONV_W ** -0.5),
        'hy_conv_b': nrm((DEPTH, 3 * D_HYENA), 0.02),
        'hy_filt_w1': nrm((DEPTH, HYENA_EMB_DIM, HYENA_FILTER_W), HYENA_EMB_DIM ** -0.5),
        'hy_filt_b1': nrm((DEPTH, HYENA_FILTER_W), 0.1),
        'hy_filt_w2': nrm((DEPTH, HYENA_FILTER_W, HYENA_FILTER_W), HYENA_FILTER_W ** -0.5),
        'hy_filt_b2': nrm((DEPTH, HYENA_FILTER_W), 0.1),
        'hy_filt_w3': nrm((DEPTH, HYENA_FILTER_W, HYENA_ORDER * 2 * D_HYENA), HYENA_FILTER_W ** -0.5),
        'hy_filt_freq': 1.0 + nrm((DEPTH, HYENA_FILTER_W), 0.05),
        'hy_bias': nrm((DEPTH, HYENA_ORDER, D_HYENA), 1.0),
        'lru_conv_w': nrm((DEPTH, LRU_CONV_W, D_LRU), LRU_CONV_W ** -0.5),
        'lru_conv_b': nrm((DEPTH, D_LRU), 0.02),
        'lru_wa': nrm((DEPTH, 2, LRU_HEADS, LRU_BLOCK, LRU_BLOCK), LRU_BLOCK ** -0.5),
        'lru_ba': nrm((DEPTH, 2, D_LRU), 0.02),
        'lru_wx': nrm((DEPTH, 2, LRU_HEADS, LRU_BLOCK, LRU_BLOCK), LRU_BLOCK ** -0.5),
        'lru_bx': nrm((DEPTH, 2, D_LRU), 0.02),
        'lru_lambda': lru_lambda,
        'sc_conv_w': nrm((DEPTH, SC_CONV_W, D_SC), SC_CONV_W ** -0.5),
        'w_hy_out': nrm((DEPTH, D_HYENA, D), D_HYENA ** -0.5),
        'w_lru_out': nrm((DEPTH, D_LRU, D), D_LRU ** -0.5),
        'w_sc_out': nrm((DEPTH, D_SC, D), D_SC ** -0.5),
        'w_o': nrm((DEPTH, D, D), D ** -0.5),
        'w_router': nrm((DEPTH, D, N_EXPERTS), D ** -0.5),
        'w_exp_gate': nrm((DEPTH, N_EXPERTS, D, D_EXPERT), D ** -0.5),
        'w_exp_up': nrm((DEPTH, N_EXPERTS, D, D_EXPERT), D ** -0.5),
        'w_exp_down': nrm((DEPTH, N_EXPERTS, D_EXPERT, D), D_EXPERT ** -0.5),
        'final_norm_g': 1.0 + nrm((D,), 0.05),
    }


def reference(x, c, ctx, c_ctx, w_ada, b_ada, norm1_g, norm2_g, w_in, hy_conv_w, hy_conv_b,
              hy_filt_w1, hy_filt_b1, hy_filt_w2, hy_filt_b2, hy_filt_w3, hy_filt_freq, hy_bias,
              lru_conv_w, lru_conv_b, lru_wa, lru_ba, lru_wx, lru_bx, lru_lambda, sc_conv_w,
              w_hy_out, w_lru_out, w_sc_out, w_o, w_router, w_exp_gate, w_exp_up, w_exp_down,
              final_norm_g):
    B, n_lat, _ = x.shape
    ROWS = n_lat // GRID_W
    x = x + grid_pos_embed(ROWS).astype(x.dtype)[None]
    xc = ctx
    zeros_state = jnp.zeros((B, D_LRU), jnp.float32)
    for l in range(DEPTH):
        last = l == DEPTH - 1
        p = {
            'w_in': w_in[l], 'hy_conv_w': hy_conv_w[l], 'hy_conv_b': hy_conv_b[l],
            'hy_filt_w1': hy_filt_w1[l], 'hy_filt_b1': hy_filt_b1[l], 'hy_filt_w2': hy_filt_w2[l],
            'hy_filt_b2': hy_filt_b2[l], 'hy_filt_w3': hy_filt_w3[l], 'hy_filt_freq': hy_filt_freq[l],
            'hy_bias': hy_bias[l], 'lru_conv_w': lru_conv_w[l], 'lru_conv_b': lru_conv_b[l],
            'lru_wa': lru_wa[l], 'lru_ba': lru_ba[l], 'lru_wx': lru_wx[l], 'lru_bx': lru_bx[l],
            'lru_lambda': lru_lambda[l], 'sc_conv_w': sc_conv_w[l], 'w_hy_out': w_hy_out[l],
            'w_lru_out': w_lru_out[l], 'w_sc_out': w_sc_out[l], 'w_o': w_o[l],
        }
        mod_l = jnp.split((jax.nn.silu(c) @ w_ada[l] + b_ada[l])[:, None, :], 6, axis=-1)
        mod_c = jnp.split((jax.nn.silu(c_ctx) @ w_ada[l] + b_ada[l])[None, None, :], 6, axis=-1)
        h_lat = modulate(rmsnorm(x, norm1_g[l]), mod_l[0], mod_l[1])
        h_ctx = modulate(rmsnorm(xc, norm1_g[l]), mod_c[0], mod_c[1])
        if last:
            hf_c, hb_c = lru_scans(h_ctx @ p['w_in'][:, OFF_LRU_REC:OFF_SC], p, zeros_state, zeros_state)
            state_f, state_b = hf_c[:, -1], hb_c[:, 0]
        else:
            out_c, state_f, state_b = mixer_sublayer(h_ctx, p, zeros_state, zeros_state)
            xc = xc + mod_c[2] * out_c
            hc2 = modulate(rmsnorm(xc, norm2_g[l]), mod_c[3], mod_c[4])
            xc = xc + mod_c[5] * ec_moe(hc2, w_router[l], w_exp_gate[l], w_exp_up[l], w_exp_down[l])
        out_l, _, _ = mixer_sublayer(h_lat, p, state_f, state_b)
        x = x + mod_l[2] * out_l
        h2 = modulate(rmsnorm(x, norm2_g[l]), mod_l[3], mod_l[4])
        x = x + mod_l[5] * ec_moe(h2, w_router[l], w_exp_gate[l], w_exp_up[l], w_exp_down[l])
    return rmsnorm(x, final_norm_g)
```

```python
import functools
import math

import numpy as np
import jax
import jax.numpy as jnp
from jax import lax
from jax.experimental import pallas as pl
from jax.experimental.pallas import tpu as pltpu

F32 = jnp.float32
BF16 = jnp.bfloat16
HI = lax.Precision.HIGHEST

NORM_EPS = 1e-6
GRID_W = 64
N_EXPERTS = 16
EC_CAPACITY = 2
LRU_C = 8.0
HYENA_ORDER = 2
HYENA_EMB_DIM = 33
HYENA_FAST_DECAY = 0.3
HYENA_SLOW_DECAY = 1.5
HYENA_TARGET = 1e-2
MLP_PAD = 128

V7X_VMEM_BYTES = 64 * 1024 * 1024
VMEM_LIMIT = 56 * 1024 * 1024
assert VMEM_LIMIT < V7X_VMEM_BYTES


def _cparams(n_grid):
    return pltpu.CompilerParams(dimension_semantics=("arbitrary",) * n_grid,
                                vmem_limit_bytes=VMEM_LIMIT)


def _tile(n, pref):
    if n <= pref:
        return n
    t = pref
    while n % t:
        t //= 2
    return t


def _resident(block_shape, index_map):
    return pl.BlockSpec(block_shape, index_map, pipeline_mode=pl.Buffered(1))


@functools.lru_cache(maxsize=None)
def _pos_embed(rows, d):
    def sincos(pos, dim):
        half = dim // 2
        omega = 1.0 / (10000.0 ** (np.arange(half, dtype=np.float64) / half))
        ang = pos[:, None] * omega[None, :]
        return np.concatenate([np.sin(ang), np.cos(ang)], axis=-1)
    half = d // 2
    er = sincos(np.arange(rows, dtype=np.float64), half)
    ec = sincos(np.arange(GRID_W, dtype=np.float64), half)
    emb = np.concatenate([np.broadcast_to(er[:, None, :], (rows, GRID_W, half)),
                          np.broadcast_to(ec[None, :, :], (rows, GRID_W, half))], axis=-1)
    return emb.reshape(rows * GRID_W, d).astype(np.float32)


@functools.lru_cache(maxsize=None)
def _dft_table(L):
    k = np.arange(L, dtype=np.int64)
    m = (k[:, None] * k[None, :]) % (2 * L)
    ang = np.pi * m.astype(np.float64) / L
    return np.concatenate([np.cos(ang), np.sin(ang)], axis=0).astype(np.float32)


@functools.lru_cache(maxsize=None)
def _filter_feats(L):
    bands = (HYENA_EMB_DIM - 1) // 2
    t01 = np.linspace(0.0, 1.0, L, dtype=np.float32).astype(np.float64)[:, None]
    w = ((2.0 * math.pi / L) * np.arange(L, dtype=np.float32))[:, None].astype(np.float64)
    f = np.linspace(1e-4, bands - 1, bands, dtype=np.float32).astype(np.float64)[None, :]
    feats = np.concatenate([t01, np.cos(f * w), -np.sin(f * w)], axis=-1)
    out = np.zeros((L, MLP_PAD), np.float32)
    out[:, :HYENA_EMB_DIM] = feats
    return out


@functools.lru_cache(maxsize=None)
def _decay_window(L, d):
    max_decay = math.log(HYENA_TARGET) / HYENA_FAST_DECAY
    min_decay = math.log(HYENA_TARGET) / HYENA_SLOW_DECAY
    t01 = np.linspace(0.0, 1.0, L, dtype=np.float32).astype(np.float64)[:, None]
    deltas = np.linspace(min_decay, max_decay, d, dtype=np.float32).astype(np.float64)
    return np.exp(-t01 * np.abs(deltas)[None, :]).astype(np.float32)


@functools.lru_cache(maxsize=None)
def _prefix_table(n):
    return np.triu(np.ones((n, n), np.float32))


def _shift_rows(x, s):
    if s == 0:
        return x
    n = x.shape[0]
    rolled = pltpu.roll(x, s % n, axis=0)
    row = lax.broadcasted_iota(jnp.int32, x.shape, 0)
    keep = row >= s if s > 0 else row < n + s
    return jnp.where(keep, rolled, 0.0)


def _dwconv(x, w_ref, left):
    acc = None
    for k in range(w_ref.shape[0]):
        term = _shift_rows(x, left - k) * w_ref[k:k + 1, :]
        acc = term if acc is None else acc + term
    return acc


def _sigmoid(x):
    return 1.0 / (1.0 + jnp.exp(-x))


def _rms_modulate(x, g, shift, scale):
    y = x * lax.rsqrt(jnp.mean(x * x, axis=-1, keepdims=True) + NORM_EPS)
    return (y * g) * (1.0 + scale) + shift


def _ada_body(c_ref, w_ref, b_ref, o_ref):
    c = c_ref[...]
    s = c * _sigmoid(c)
    o_ref[...] = jnp.dot(s, w_ref[...], precision=HI, preferred_element_type=F32) + b_ref[...]


def ada_mods(cc, w_ada, b_ada):
    depth, d, n = w_ada.shape
    r = cc.shape[0]
    tn = _tile(n, 1536)
    return pl.pallas_call(
        _ada_body,
        grid=(depth, n // tn),
        in_specs=[pl.BlockSpec((r, d), lambda l, j: (0, 0)),
                  pl.BlockSpec((None, d, tn), lambda l, j: (l, 0, j)),
                  pl.BlockSpec((None, 1, tn), lambda l, j: (l, 0, j))],
        out_specs=pl.BlockSpec((None, r, tn), lambda l, j: (l, 0, j)),
        out_shape=jax.ShapeDtypeStruct((depth, r, n), F32),
        compiler_params=_cparams(2),
        name="ada",
    )(cc, w_ada, b_ada.reshape(depth, 1, n))


def _add_body(x_ref, p_ref, o_ref):
    o_ref[...] = x_ref[...] + p_ref[...]


def add_pos(x, pe):
    b, l, d = x.shape
    tl = _tile(l, 1024)
    return pl.pallas_call(
        _add_body,
        grid=(l // tl, b),
        in_specs=[pl.BlockSpec((None, tl, d), lambda i, bb: (bb, i, 0)),
                  pl.BlockSpec((tl, d), lambda i, bb: (i, 0))],
        out_specs=pl.BlockSpec((None, tl, d), lambda i, bb: (bb, i, 0)),
        out_shape=jax.ShapeDtypeStruct(x.shape, F32),
        compiler_params=_cparams(2),
        name="add_pos",
    )(x, pe)


def _norm_proj_body(x_ref, g_ref, sh_ref, sc_ref, w_ref, o_ref, h_ref):
    @pl.when(pl.program_id(2) == 0)
    def _():
        h_ref[...] = _rms_modulate(x_ref[...], g_ref[...], sh_ref[...], sc_ref[...]).astype(BF16)

    o_ref[...] = jnp.dot(h_ref[...], w_ref[...], preferred_element_type=F32).astype(o_ref.dtype)


def norm_proj(x, g, shift, scale, w):
    b, l, d = x.shape
    n = w.shape[1]
    tl = _tile(l, 1024)
    tn = _tile(n, 1024)
    return pl.pallas_call(
        _norm_proj_body,
        grid=(b, l // tl, n // tn),
        in_specs=[pl.BlockSpec((None, tl, d), lambda bb, i, j: (bb, i, 0)),
                  pl.BlockSpec((1, d), lambda bb, i, j: (0, 0)),
                  pl.BlockSpec((None, 1, d), lambda bb, i, j: (bb, 0, 0)),
                  pl.BlockSpec((None, 1, d), lambda bb, i, j: (bb, 0, 0)),
                  pl.BlockSpec((d, tn), lambda bb, i, j: (0, j))],
        out_specs=pl.BlockSpec((None, tl, tn), lambda bb, i, j: (bb, i, j)),
        out_shape=jax.ShapeDtypeStruct((b, l, n), BF16),
        scratch_shapes=[pltpu.VMEM((tl, d), BF16)],
        compiler_params=_cparams(3),
        name="norm_proj",
    )(x, g.reshape(1, d), shift, scale, w)


def _hy_filter_body(feats_ref, w1_ref, b1_ref, w2_ref, b2_ref, fr_ref, w3f_ref, w3b_ref,
                    decay_ref, fs_ref, t1_ref, t2_ref, ny_ref):
    L = feats_ref.shape[0]
    fr = fr_ref[...]
    h = jnp.sin(fr * (jnp.dot(feats_ref[...], w1_ref[...], precision=HI,
                              preferred_element_type=F32) + b1_ref[...]))
    h = jnp.sin(fr * (jnp.dot(h, w2_ref[...], precision=HI,
                              preferred_element_type=F32) + b2_ref[...]))
    decay = decay_ref[...]
    fwd = jnp.dot(h, w3f_ref[...], precision=HI, preferred_element_type=F32) * decay
    bwd = jnp.dot(h, w3b_ref[...], precision=HI, preferred_element_type=F32) * decay
    row = lax.broadcasted_iota(jnp.int32, fwd.shape, 0)
    bwd = jnp.where(row >= 1, bwd, 0.0)
    inv = 1.0 / (jnp.sum(jnp.abs(fwd), axis=0, keepdims=True)
                 + jnp.sum(jnp.abs(bwd), axis=0, keepdims=True))
    p = fwd + bwd
    q = bwd - fwd
    sign = (1 - 2 * (row & 1)).astype(F32)
    k_nyq = jnp.sum(p * sign, axis=0, keepdims=True)
    k_re = jnp.dot(fs_ref[0:L, :], p.astype(BF16), preferred_element_type=F32)
    k_im = jnp.dot(fs_ref[L:2 * L, :], q.astype(BF16), preferred_element_type=F32)
    n_fft = 2.0 * L
    w_k = jnp.where(row >= 1, 2.0 / n_fft, 1.0 / n_fft)
    t1_ref[...] = k_re * (w_k * inv)
    t2_ref[...] = k_im * ((2.0 / n_fft) * inv)
    ny_ref[...] = k_nyq * (inv / n_fft)


def hy_filter_tables(L, w1, b1, w2, b2, w3, freq, fs):
    d = w3.shape[1] // (HYENA_ORDER * 2)
    fw = w1.shape[1]
    pad = lambda a, r, c: jnp.zeros((r, c), F32).at[:a.shape[0], :a.shape[1]].set(a)
    w1p = pad(w1, MLP_PAD, MLP_PAD)
    w2p = pad(w2, MLP_PAD, MLP_PAD)
    w3p = pad(w3, MLP_PAD, w3.shape[1])
    b1p = pad(b1.reshape(1, fw), 1, MLP_PAD)
    b2p = pad(b2.reshape(1, fw), 1, MLP_PAD)
    frp = pad(freq.reshape(1, fw), 1, MLP_PAD)
    feats = jnp.asarray(_filter_feats(L))
    decay = jnp.asarray(_decay_window(L, d))
    tn = _tile(d, 256)
    nj = d // tn
    full = lambda o, j: (0, 0)
    t1, t2, ny = pl.pallas_call(
        _hy_filter_body,
        grid=(HYENA_ORDER, nj),
        in_specs=[pl.BlockSpec((L, MLP_PAD), full),
                  pl.BlockSpec((MLP_PAD, MLP_PAD), full),
                  pl.BlockSpec((1, MLP_PAD), full),
                  pl.BlockSpec((MLP_PAD, MLP_PAD), full),
                  pl.BlockSpec((1, MLP_PAD), full),
                  pl.BlockSpec((1, MLP_PAD), full),
                  pl.BlockSpec((MLP_PAD, tn), lambda o, j: (0, o * 2 * nj + j)),
                  pl.BlockSpec((MLP_PAD, tn), lambda o, j: (0, o * 2 * nj + nj + j)),
                  pl.BlockSpec((L, tn), lambda o, j: (0, j)),
                  _resident((2 * L, L), full)],
        out_specs=[pl.BlockSpec((L, tn), lambda o, j: (0, o * nj + j)),
                   pl.BlockSpec((L, tn), lambda o, j: (0, o * nj + j)),
                   pl.BlockSpec((1, tn), lambda o, j: (0, o * nj + j))],
        out_shape=[jax.ShapeDtypeStruct((L, HYENA_ORDER * d), F32),
                   jax.ShapeDtypeStruct((L, HYENA_ORDER * d), F32),
                   jax.ShapeDtypeStruct((1, HYENA_ORDER * d), F32)],
        compiler_params=_cparams(2),
        name="hy_filter",
    )(feats, w1p, b1p, w2p, b2p, frp, w3p, w3p, decay, fs)
    return t1, t2, ny


HALO = 8


def _chunks(n_rows, chunk, fn):
    def body(c, carry):
        fn(pl.multiple_of(c * chunk, chunk))
        return carry
    lax.fori_loop(0, n_rows // chunk, body, 0)


def _conv_rows(src_ref, w_ref, b_ref, left, pad_ref, emit):
    L = src_ref.shape[0]
    rc = min(L, 256)
    win = rc + 2 * HALO
    zeros = jnp.zeros((HALO, pad_ref.shape[1]), F32)
    pad_ref[0:HALO, :] = zeros
    pad_ref[L + HALO:L + 2 * HALO, :] = zeros

    def fill(r0):
        pad_ref[pl.ds(r0 + HALO, rc), :] = src_ref[pl.ds(r0, rc), :].astype(F32)
    _chunks(L, rc, fill)

    def conv(r0):
        w = pad_ref[pl.ds(r0, win), :]
        acc = None
        for k in range(w_ref.shape[0]):
            s = left - k
            term = (w if s == 0 else pltpu.roll(w, s % win, axis=0)) * w_ref[k:k + 1, :]
            acc = term if acc is None else acc + term
        out = acc[HALO:HALO + rc]
        emit(r0, out if b_ref is None else out + b_ref[...])
    _chunks(L, rc, conv)


def _long_conv_rows(u_ref, ub_ref, t1_ref, t2_ref, ny_ref, skip, fs_ref, spec_ref, za_ref, zb_ref, emit):
    L = u_ref.shape[0]
    rc = min(L, 512)
    row = lax.broadcasted_iota(jnp.int32, (rc, 1), 0)
    sign = (1 - 2 * (row & 1)).astype(F32)

    def forward(r0):
        spec_ref[pl.ds(r0, rc), :] = jnp.dot(fs_ref[pl.ds(r0, rc), :], ub_ref[...], preferred_element_type=F32)
    _chunks(2 * L, rc, forward)

    def nyq_sum(c, acc):
        r0 = pl.multiple_of(c * rc, rc)
        return acc + jnp.sum(u_ref[pl.ds(r0, rc), :] * sign, axis=0, keepdims=True)
    u_nyq = lax.fori_loop(0, L // rc, nyq_sum, jnp.zeros((1, u_ref.shape[1]), F32))
    nyq_term = u_nyq * ny_ref[...]

    def mix(r0):
        ua = spec_ref[pl.ds(r0, rc), :]
        ub = spec_ref[pl.ds(L + r0, rc), :]
        t1 = t1_ref[pl.ds(r0, rc), :]
        t2 = t2_ref[pl.ds(r0, rc), :]
        za_ref[pl.ds(r0, rc), :] = (ua * t1 + ub * t2).astype(BF16)
        zb_ref[pl.ds(r0, rc), :] = (ub * t1 - ua * t2).astype(BF16)
    _chunks(L, rc, mix)

    def inverse(r0):
        y = jnp.dot(fs_ref[pl.ds(r0, rc), :], za_ref[...], preferred_element_type=F32)
        y = y + jnp.dot(fs_ref[pl.ds(L + r0, rc), :], zb_ref[...], preferred_element_type=F32)
        emit(r0, y + sign * nyq_term + u_ref[pl.ds(r0, rc), :] * skip)
    _chunks(L, rc, inverse)


def _hyena_body(p1_ref, p2_ref, pv_ref, cw1_ref, cw2_ref, cwv_ref, cb1_ref, cb2_ref, cbv_ref,
                bias_ref, t1a_ref, t2a_ref, nya_ref, t1b_ref, t2b_ref, nyb_ref, fs_ref, o_ref,
                pad_ref, u_ref, ub_ref, x1_ref, x2_ref, spec_ref, za_ref, zb_ref):
    def put(ref):
        def emit(r0, rows):
            ref[pl.ds(r0, rows.shape[0]), :] = rows
        return emit

    def put_u(r0, rows):
        u_ref[pl.ds(r0, rows.shape[0]), :] = rows
        ub_ref[pl.ds(r0, rows.shape[0]), :] = rows.astype(BF16)

    _conv_rows(p1_ref, cw1_ref, cb1_ref, 1, pad_ref, put(x1_ref))
    _conv_rows(p2_ref, cw2_ref, cb2_ref, 1, pad_ref, put(x2_ref))
    _conv_rows(pv_ref, cwv_ref, cbv_ref, 1, pad_ref, put_u)

    def first(r0, c1):
        put_u(r0, x1_ref[pl.ds(r0, c1.shape[0]), :] * c1)
    _long_conv_rows(u_ref, ub_ref, t1a_ref, t2a_ref, nya_ref, bias_ref[0:1, :], fs_ref,
                    spec_ref, za_ref, zb_ref, first)

    def second(r0, c2):
        o_ref[pl.ds(r0, c2.shape[0]), :] = (x2_ref[pl.ds(r0, c2.shape[0]), :] * c2).astype(o_ref.dtype)
    _long_conv_rows(u_ref, ub_ref, t1b_ref, t2b_ref, nyb_ref, bias_ref[1:2, :], fs_ref,
                    spec_ref, za_ref, zb_ref, second)


def hyena_mixer(proj, conv_w, conv_b, bias, t1, t2, ny, fs):
    b, l, _ = proj.shape
    d = bias.shape[1]
    tn = _tile(d, 256)
    nj = d // tn
    k = conv_w.shape[0]
    cb = conv_b.reshape(1, 3 * d)
    sec = lambda s: pl.BlockSpec((None, l, tn), lambda j, bb: (bb, 0, s * nj + j))
    cws = lambda s: pl.BlockSpec((k, tn), lambda j, bb: (0, s * nj + j))
    cbs = lambda s: pl.BlockSpec((1, tn), lambda j, bb: (0, s * nj + j))
    tab = lambda o, r: _resident((r, tn), lambda j, bb: (0, o * nj + j))
    return pl.pallas_call(
        _hyena_body,
        grid=(nj, b),
        in_specs=[sec(0), sec(1), sec(2), cws(0), cws(1), cws(2), cbs(0), cbs(1), cbs(2),
                  pl.BlockSpec((HYENA_ORDER, tn), lambda j, bb: (0, j)),
                  tab(0, l), tab(0, l), tab(0, 1), tab(1, l), tab(1, l), tab(1, 1),
                  _resident((2 * l, l), lambda j, bb: (0, 0))],
        out_specs=pl.BlockSpec((None, l, tn), lambda j, bb: (bb, 0, j)),
        out_shape=jax.ShapeDtypeStruct((b, l, d), BF16),
        scratch_shapes=[pltpu.VMEM((l + 2 * HALO, tn), F32),
                        pltpu.VMEM((l, tn), F32), pltpu.VMEM((l, tn), BF16),
                        pltpu.VMEM((l, tn), F32), pltpu.VMEM((l, tn), F32),
                        pltpu.VMEM((2 * l, tn), F32),
                        pltpu.VMEM((l, tn), BF16), pltpu.VMEM((l, tn), BF16)],
        compiler_params=_cparams(2),
        name="hyena",
    )(proj, proj, proj, conv_w, conv_w, conv_w, cb, cb, cb, bias, t1, t2, ny, t1, t2, ny, fs)


def _lru_gates_body(rec_ref, cw_ref, cb_ref, wa_ref, ba_ref, wx_ref, bx_ref, lam_ref,
                    af_ref, bf_ref, ab_ref, bb_ref):
    xc = _dwconv(rec_ref[...].astype(F32), cw_ref, 2) + cb_ref[...]
    xcb = xc.astype(BF16)
    for dr, (a_out, b_out) in enumerate(((af_ref, bf_ref), (ab_ref, bb_ref))):
        r = _sigmoid(jnp.dot(xcb, wa_ref[dr], preferred_element_type=F32) + ba_ref[dr:dr + 1, :])
        i = _sigmoid(jnp.dot(xcb, wx_ref[dr], preferred_element_type=F32) + bx_ref[dr:dr + 1, :])
        softplus = jnp.log1p(jnp.exp(-lam_ref[dr:dr + 1, :]))
        a = jnp.exp(-LRU_C * r * softplus)
        a_out[...] = a
        b_out[...] = jnp.sqrt(1.0 - a * a) * (i * xc)


def _block_diag_tiles(w, tn):
    two, h, bs, _ = w.shape
    hpt = tn // bs
    wt = w.reshape(two, h // hpt, hpt, bs, bs)
    eye = jnp.eye(hpt, dtype=w.dtype)
    dense = jnp.einsum('tnhij,hg->tnhigj', wt, eye)
    return dense.reshape(two, h // hpt, tn, tn)


def lru_gates(proj, rec_sec, conv_w, conv_b, wa, ba, wx, bx, lam):
    b, l, n = proj.shape
    d = conv_w.shape[1]
    tn = _tile(d, 256)
    nj = d // tn
    k = conv_w.shape[0]
    wa_t = _block_diag_tiles(wa, tn).astype(BF16)
    wx_t = _block_diag_tiles(wx, tn).astype(BF16)
    vec = lambda r: pl.BlockSpec((r, tn), lambda bb, j: (0, j))
    wsp = pl.BlockSpec((2, None, tn, tn), lambda bb, j: (0, j, 0, 0))
    out = pl.BlockSpec((None, l, tn), lambda bb, j: (bb, 0, j))
    shp = jax.ShapeDtypeStruct((b, l, d), F32)
    return pl.pallas_call(
        _lru_gates_body,
        grid=(b, nj),
        in_specs=[pl.BlockSpec((None, l, tn), lambda bb, j: (bb, 0, rec_sec * nj + j)),
                  vec(k), vec(1), wsp, vec(2), wsp, vec(2), vec(2)],
        out_specs=[out, out, out, out],
        out_shape=[shp, shp, shp, shp],
        compiler_params=_cparams(2),
        name="lru_gates",
    )(proj, conv_w, conv_b.reshape(1, d), wa_t, ba, wx_t, bx, lam)


def _lru_scan_body(a_ref, b_ref, h0_ref, o_ref, carry_ref, *, reverse):
    tl = a_ref.shape[0]
    groups = tl // 8

    @pl.when(pl.program_id(1) == 0)
    def _():
        carry_ref[0:1, :] = h0_ref[...]

    def step(g, h):
        base = pl.multiple_of((groups - 1 - g if reverse else g) * 8, 8)
        for s in (range(7, -1, -1) if reverse else range(8)):
            h = a_ref[pl.ds(base + s, 1), :] * h + b_ref[pl.ds(base + s, 1), :]
            o_ref[pl.ds(base + s, 1), :] = h
        return h

    carry_ref[0:1, :] = lax.fori_loop(0, groups, step, carry_ref[0:1, :])


def lru_scan(a, b, h0, reverse):
    bsz, l, d = a.shape
    tl = _tile(l, 256)
    nt = l // tl
    tmap = (lambda bb, i: (bb, nt - 1 - i, 0)) if reverse else (lambda bb, i: (bb, i, 0))
    return pl.pallas_call(
        functools.partial(_lru_scan_body, reverse=reverse),
        grid=(bsz, nt),
        in_specs=[pl.BlockSpec((None, tl, d), tmap),
                  pl.BlockSpec((None, tl, d), tmap),
                  pl.BlockSpec((None, 1, d), lambda bb, i: (bb, 0, 0))],
        out_specs=pl.BlockSpec((None, tl, d), tmap),
        out_shape=jax.ShapeDtypeStruct((bsz, l, d), F32),
        scratch_shapes=[pltpu.VMEM((8, d), F32)],
        compiler_params=_cparams(2),
        name="lru_scan",
    )(a, b, h0.reshape(bsz, 1, d))


def _shortconv_body(bg_ref, cg_ref, xv_ref, w_ref, o_ref):
    prod = cg_ref[...].astype(F32) * xv_ref[...].astype(F32)
    o_ref[...] = (bg_ref[...].astype(F32) * _dwconv(prod, w_ref, 1)).astype(o_ref.dtype)


def shortconv_mixer(proj, first_sec, conv_w):
    b, l, _ = proj.shape
    k, d = conv_w.shape
    tn = _tile(d, 256)
    nj = d // tn
    sec = lambda s: pl.BlockSpec((None, l, tn), lambda bb, j: (bb, 0, (first_sec + s) * nj + j))
    return pl.pallas_call(
        _shortconv_body,
        grid=(b, nj),
        in_specs=[sec(0), sec(1), sec(2), pl.BlockSpec((k, tn), lambda bb, j: (0, j))],
        out_specs=pl.BlockSpec((None, l, tn), lambda bb, j: (bb, 0, j)),
        out_shape=jax.ShapeDtypeStruct((b, l, d), BF16),
        compiler_params=_cparams(2),
        name="shortconv",
    )(proj, proj, proj, conv_w)


def _gelu_tanh(x):
    return 0.5 * x * (1.0 + jnp.tanh(math.sqrt(2.0 / math.pi) * (x + 0.044715 * (x * x * x))))


def _merge_body(ya_ref, yc_ref, lg_ref, hf_ref, hb_ref, ga_ref, gb_ref, gc_ref, x_ref,
                mg_ref, g2_ref, sh_ref, sc_ref, wa_ref, wb_ref, wc_ref, wo_ref, wr_ref,
                xo_ref, h2_ref, aff_ref):
    yb = (_gelu_tanh(lg_ref[...].astype(F32)) * (hf_ref[...] + hb_ref[...])).astype(BF16)
    merged = _sigmoid(ga_ref[...].astype(F32)) * jnp.dot(ya_ref[...], wa_ref[...], preferred_element_type=F32)
    merged += _sigmoid(gb_ref[...].astype(F32)) * jnp.dot(yb, wb_ref[...], preferred_element_type=F32)
    merged += _sigmoid(gc_ref[...].astype(F32)) * jnp.dot(yc_ref[...], wc_ref[...], preferred_element_type=F32)
    out = jnp.dot(merged.astype(BF16), wo_ref[...], preferred_element_type=F32)
    x = x_ref[...] + mg_ref[...] * out
    xo_ref[...] = x
    h2 = _rms_modulate(x, g2_ref[...], sh_ref[...], sc_ref[...])
    h2_ref[...] = h2.astype(BF16)
    logits = jnp.dot(h2, wr_ref[...], precision=HI, preferred_element_type=F32)
    e = jnp.exp(logits - jnp.max(logits, axis=-1, keepdims=True))
    aff_ref[...] = e / jnp.sum(e, axis=-1, keepdims=True)


def merge_mixers(proj, gate_sec, lg_sec, ya, yc, hf, hb, x, mod_gate, g2, shift2, scale2,
                 w_a, w_b, w_c, w_o, w_r):
    b, l, d = x.shape
    e = w_r.shape[1]
    tl = _tile(l, 256)
    row = lambda: pl.BlockSpec((None, tl, d), lambda bb, i: (bb, i, 0))
    sec = lambda s: pl.BlockSpec((None, tl, d), lambda bb, i: (bb, i, s))
    modv = lambda: pl.BlockSpec((None, 1, d), lambda bb, i: (bb, 0, 0))
    wsp = lambda: _resident((d, d), lambda bb, i: (0, 0))
    return pl.pallas_call(
        _merge_body,
        grid=(b, l // tl),
        in_specs=[row(), row(), sec(lg_sec), row(), row(),
                  sec(gate_sec), sec(gate_sec + 1), sec(gate_sec + 2), row(),
                  modv(), pl.BlockSpec((1, d), lambda bb, i: (0, 0)), modv(), modv(),
                  wsp(), wsp(), wsp(), wsp(), _resident((d, e), lambda bb, i: (0, 0))],
        out_specs=[row(), row(), pl.BlockSpec((None, tl, e), lambda bb, i: (bb, i, 0))],
        out_shape=[jax.ShapeDtypeStruct((b, l, d), F32),
                   jax.ShapeDtypeStruct((b, l, d), BF16),
                   jax.ShapeDtypeStruct((b, l, e), F32)],
        compiler_params=_cparams(2),
        name="merge",
    )(ya, yc, proj, hf, hb, proj, proj, proj, x, mod_gate, g2.reshape(1, d), shift2, scale2,
      w_a, w_b, w_c, w_o, w_r)


def _select_body(aff_ref, tri_ref, rank_ref, *, cap):
    bits = lax.bitcast_convert_type(aff_ref[...], jnp.int32)

    def refine(i, thr):
        cand = thr | jnp.left_shift(jnp.int32(1), 30 - i)
        cnt = jnp.sum((bits >= cand).astype(F32), axis=1, keepdims=True)
        return jnp.where(cnt >= cap, cand, thr)

    thr = lax.fori_loop(0, 31, refine, jnp.zeros((bits.shape[0], 1), jnp.int32))
    gt = bits > thr
    eq = bits == thr
    n_gt = jnp.sum(gt.astype(F32), axis=1, keepdims=True)
    eq_rank = jnp.dot(eq.astype(BF16), tri_ref[...], preferred_element_type=F32)
    sel = gt | (eq & (eq_rank <= cap - n_gt))
    rank = jnp.dot(sel.astype(BF16), tri_ref[...], preferred_element_type=F32) - 1.0
    rank_ref[...] = jnp.where(sel, rank, -1.0).astype(jnp.int32)


def select_tokens(aff_t, cap):
    r, n = aff_t.shape
    tri = jnp.asarray(_prefix_table(n), BF16)
    return pl.pallas_call(
        functools.partial(_select_body, cap=cap),
        grid=(1,),
        in_specs=[pl.BlockSpec((r, n), lambda i: (0, 0)), pl.BlockSpec((n, n), lambda i: (0, 0))],
        out_specs=pl.BlockSpec((r, n), lambda i: (0, 0)),
        out_shape=jax.ShapeDtypeStruct((r, n), jnp.int32),
        compiler_params=_cparams(1),
        name="select",
    )(aff_t, tri)


def _gather_body(rank_ref, hs_ref, o_ref):
    e = pl.program_id(1)
    cap = o_ref.shape[0]
    n = hs_ref.shape[0]
    rank_row = rank_ref[pl.ds(e, 1), :]
    slot = lax.broadcasted_iota(jnp.int32, (cap, n), 0)
    onehot = jnp.where(slot == rank_row, 1.0, 0.0).astype(BF16)
    o_ref[...] = jnp.dot(onehot, hs_ref[...], preferred_element_type=F32).astype(o_ref.dtype)


def gather_tokens(rank_t, hs, cap):
    s, e, n = rank_t.shape
    d = hs.shape[2]
    return pl.pallas_call(
        _gather_body,
        grid=(s, e),
        in_specs=[pl.BlockSpec((None, e, n), lambda ss, ee: (ss, 0, 0)),
                  pl.BlockSpec((None, n, d), lambda ss, ee: (ss, 0, 0))],
        out_specs=pl.BlockSpec((None, cap, d), lambda ss, ee: (ee, ss, 0)),
        out_shape=jax.ShapeDtypeStruct((e, s * cap, d), BF16),
        compiler_params=_cparams(2),
        name="gather",
    )(rank_t, hs)


def _ffn_body(x_ref, wg_ref, wu_ref, wd_ref, o_ref, acc_ref):
    f = pl.program_id(1)
    x = x_ref[...]
    g = jnp.dot(x, wg_ref[...].astype(BF16), preferred_element_type=F32)
    u = jnp.dot(x, wu_ref[...].astype(BF16), preferred_element_type=F32)
    hid = (g * _sigmoid(g) * u).astype(BF16)
    part = jnp.dot(hid, wd_ref[...].astype(BF16), preferred_element_type=F32)

    @pl.when(f == 0)
    def _():
        acc_ref[...] = part

    @pl.when(f > 0)
    def _():
        acc_ref[...] += part

    @pl.when(f == pl.num_programs(1) - 1)
    def _():
        o_ref[...] = acc_ref[...].astype(o_ref.dtype)


def expert_ffn(xs, wg, wu, wd):
    e, m, d = xs.shape
    fdim = wg.shape[2]
    tf = _tile(fdim, 256)
    return pl.pallas_call(
        _ffn_body,
        grid=(e, fdim // tf),
        in_specs=[pl.BlockSpec((None, m, d), lambda ee, f: (ee, 0, 0)),
                  pl.BlockSpec((None, d, tf), lambda ee, f: (ee, 0, f)),
                  pl.BlockSpec((None, d, tf), lambda ee, f: (ee, 0, f)),
                  pl.BlockSpec((None, tf, d), lambda ee, f: (ee, f, 0))],
        out_specs=pl.BlockSpec((None, m, d), lambda ee, f: (ee, 0, 0)),
        out_shape=jax.ShapeDtypeStruct((e, m, d), BF16),
        scratch_shapes=[pltpu.VMEM((m, d), F32)],
        compiler_params=_cparams(2),
        name="ffn",
    )(xs, wg, wu, wd)


def _scatter_body(rank_ref, aff_ref, ys_ref, x_ref, mg_ref, gf_ref, o_ref, *, final_norm):
    tq = rank_ref.shape[0]
    n_exp, cap, _ = ys_ref.shape
    slot = lax.broadcasted_iota(jnp.int32, (tq, cap), 1)
    rank = rank_ref[...]
    aff = aff_ref[...]
    acc = None
    for e in range(n_exp):
        q = jnp.where(slot == rank[:, e:e + 1], aff[:, e:e + 1], 0.0).astype(BF16)
        part = jnp.dot(q, ys_ref[e], preferred_element_type=F32)
        acc = part if acc is None else acc + part
    x = x_ref[...] + mg_ref[...] * acc
    if final_norm:
        x = x * lax.rsqrt(jnp.mean(x * x, axis=-1, keepdims=True) + NORM_EPS) * gf_ref[...]
    o_ref[...] = x


def scatter_residual(rank, aff, ys, row_off, cap, x, mod_gate, final_g):
    s, n, e = rank.shape
    d = x.shape[2]
    tq = _tile(n, 512)
    boff = row_off // cap
    final_norm = final_g is not None
    gf = (final_g if final_norm else jnp.ones((d,), F32)).reshape(1, d)
    return pl.pallas_call(
        functools.partial(_scatter_body, final_norm=final_norm),
        grid=(s, n // tq),
        in_specs=[pl.BlockSpec((None, tq, e), lambda ss, i: (ss, i, 0)),
                  pl.BlockSpec((None, tq, e), lambda ss, i: (ss, i, 0)),
                  pl.BlockSpec((e, cap, d), lambda ss, i: (0, boff + ss, 0)),
                  pl.BlockSpec((None, tq, d), lambda ss, i: (ss, i, 0)),
                  pl.BlockSpec((None, 1, d), lambda ss, i: (ss, 0, 0)),
                  pl.BlockSpec((1, d), lambda ss, i: (0, 0))],
        out_specs=pl.BlockSpec((None, tq, d), lambda ss, i: (ss, i, 0)),
        out_shape=jax.ShapeDtypeStruct(x.shape, F32),
        compiler_params=_cparams(2),
        name="scatter",
    )(rank, aff, ys, x, mod_gate, gf)


def _mixer_and_route(x, h0_f, h0_b, mods, lw, fs):
    b, l, d = x.shape
    proj = norm_proj(x, lw['norm1_g'], mods[0], mods[1], lw['w_in'])
    t1, t2, ny = hy_filter_tables(l, lw['hy_filt_w1'], lw['hy_filt_b1'], lw['hy_filt_w2'], lw['hy_filt_b2'],
                                  lw['hy_filt_w3'], lw['hy_filt_freq'], fs)
    ya = hyena_mixer(proj, lw['hy_conv_w'], lw['hy_conv_b'], lw['hy_bias'], t1, t2, ny, fs)
    a_f, b_f, a_b, b_b = lru_gates(proj, 4, lw['lru_conv_w'], lw['lru_conv_b'], lw['lru_wa'], lw['lru_ba'],
                                   lw['lru_wx'], lw['lru_bx'], lw['lru_lambda'])
    hf = lru_scan(a_f, b_f, h0_f, False)
    hb = lru_scan(a_b, b_b, h0_b, True)
    yc = shortconv_mixer(proj, 5, lw['sc_conv_w'])
    x_new, h2, aff = merge_mixers(proj, 8, 3, ya, yc, hf, hb, x, mods[2], lw['norm2_g'], mods[3], mods[4],
                                  lw['w_hy_out'], lw['w_lru_out'], lw['w_sc_out'], lw['w_o'], lw['w_router'])
    return x_new, h2, aff, hf[:, -1], hb[:, 0]


def _route(h2, aff):
    s, n, e = aff.shape
    cap = EC_CAPACITY * n // e
    aff_t = jnp.transpose(aff, (0, 2, 1))
    rank_t = select_tokens(aff_t.reshape(s * e, n), cap).reshape(s, e, n)
    xs = gather_tokens(rank_t, h2, cap)
    return xs, jnp.transpose(rank_t, (0, 2, 1)), cap


def kernel(x, c, ctx, c_ctx, w_ada, b_ada, norm1_g, norm2_g, w_in, hy_conv_w, hy_conv_b, hy_filt_w1, hy_filt_b1, hy_filt_w2, hy_filt_b2, hy_filt_w3, hy_filt_freq, hy_bias, lru_conv_w, lru_conv_b, lru_wa, lru_ba, lru_wx, lru_bx, lru_lambda, sc_conv_w, w_hy_out, w_lru_out, w_sc_out, w_o, w_router, w_exp_gate, w_exp_up, w_exp_down, final_norm_g):
    bsz, n_lat, d = x.shape
    n_ctx = ctx.shape[1]
    depth = w_ada.shape[0]
    rows = n_lat // GRID_W

    x = add_pos(x, jnp.asarray(_pos_embed(rows, d)))
    xc = ctx
    fs_lat = jnp.asarray(_dft_table(n_lat), BF16)
    fs_ctx = jnp.asarray(_dft_table(n_ctx), BF16)

    cc = jnp.zeros((2 * bsz, d), F32).at[:bsz].set(c).at[bsz].set(c_ctx)
    mods = ada_mods(cc, w_ada, b_ada)
    zeros_state = jnp.zeros((bsz, d), F32)

    for l in range(depth):
        last = l == depth - 1
        lw = {
            'norm1_g': norm1_g[l], 'norm2_g': norm2_g[l], 'w_in': w_in[l].astype(BF16),
            'hy_conv_w': hy_conv_w[l], 'hy_conv_b': hy_conv_b[l],
            'hy_filt_w1': hy_filt_w1[l], 'hy_filt_b1': hy_filt_b1[l], 'hy_filt_w2': hy_filt_w2[l],
            'hy_filt_b2': hy_filt_b2[l], 'hy_filt_w3': hy_filt_w3[l], 'hy_filt_freq': hy_filt_freq[l],
            'hy_bias': hy_bias[l], 'lru_conv_w': lru_conv_w[l], 'lru_conv_b': lru_conv_b[l],
            'lru_wa': lru_wa[l], 'lru_ba': lru_ba[l], 'lru_wx': lru_wx[l], 'lru_bx': lru_bx[l],
            'lru_lambda': lru_lambda[l], 'sc_conv_w': sc_conv_w[l],
            'w_hy_out': w_hy_out[l].astype(BF16), 'w_lru_out': w_lru_out[l].astype(BF16),
            'w_sc_out': w_sc_out[l].astype(BF16), 'w_o': w_o[l].astype(BF16), 'w_router': w_router[l],
        }
        mod_l = [mods[l, :bsz, k * d:(k + 1) * d].reshape(bsz, 1, d) for k in range(6)]
        mod_c = [jnp.broadcast_to(mods[l, bsz, k * d:(k + 1) * d].reshape(1, 1, d), (bsz, 1, d)) for k in range(6)]

        if last:
            proj_c = norm_proj(xc, lw['norm1_g'], mod_c[0], mod_c[1], lw['w_in'][:, 4 * d:5 * d])
            a_f, b_f, a_b, b_b = lru_gates(proj_c, 0, lw['lru_conv_w'], lw['lru_conv_b'], lw['lru_wa'], lw['lru_ba'],
                                           lw['lru_wx'], lw['lru_bx'], lw['lru_lambda'])
            state_f = lru_scan(a_f, b_f, zeros_state, False)[:, -1]
            state_b = lru_scan(a_b, b_b, zeros_state, True)[:, 0]
            ctx_route = None
        else:
            xc_mid, h2_c, aff_c, state_f, state_b = _mixer_and_route(xc, zeros_state, zeros_state, mod_c, lw, fs_ctx)
            xs_c, rank_c, cap_c = _route(h2_c, aff_c)
            ctx_route = (xs_c, rank_c, aff_c, cap_c, xc_mid)

        x_mid, h2_l, aff_l, _, _ = _mixer_and_route(x, state_f, state_b, mod_l, lw, fs_lat)
        xs_l, rank_l, cap_l = _route(h2_l, aff_l)

        xs_all = xs_l if ctx_route is None else jnp.concatenate([xs_l, ctx_route[0]], axis=1)
        ys_all = expert_ffn(xs_all, w_exp_gate[l], w_exp_up[l], w_exp_down[l])

        if ctx_route is not None:
            _, rank_c, aff_c, cap_c, xc_mid = ctx_route
            xc = scatter_residual(rank_c, aff_c, ys_all, xs_l.shape[1], cap_c, xc_mid, mod_c[5], None)
        x = scatter_residual(rank_l, aff_l, ys_all, 0, cap_l, x_mid, mod_l[5], final_norm_g if last else None)
    return x
```

```python
import functools
import math

import numpy as np
import jax
import jax.numpy as jnp
from jax import lax
from jax.experimental import pallas as pl
from jax.experimental.pallas import tpu as pltpu

F32 = jnp.float32
BF16 = jnp.bfloat16
HI = lax.Precision.HIGHEST

NORM_EPS = 1e-6
GRID_W = 64
N_EXPERTS = 16
EC_CAPACITY = 2
LRU_C = 8.0
HYENA_ORDER = 2
HYENA_EMB_DIM = 33
HYENA_FAST_DECAY = 0.3
HYENA_SLOW_DECAY = 1.5
HYENA_TARGET = 1e-2
MLP_PAD = 128

V7X_VMEM_BYTES = 64 * 1024 * 1024
VMEM_LIMIT = 56 * 1024 * 1024
assert VMEM_LIMIT < V7X_VMEM_BYTES


def _cparams(n_grid):
    return pltpu.CompilerParams(dimension_semantics=("arbitrary",) * n_grid,
                                vmem_limit_bytes=VMEM_LIMIT)


def _tile(n, pref):
    if n <= pref:
        return n
    t = pref
    while n % t:
        t //= 2
    return t


def _resident(block_shape, index_map):
    return pl.BlockSpec(block_shape, index_map, pipeline_mode=pl.Buffered(1))


@functools.lru_cache(maxsize=None)
def _pos_embed(rows, d):
    def sincos(pos, dim):
        half = dim // 2
        omega = 1.0 / (10000.0 ** (np.arange(half, dtype=np.float64) / half))
        ang = pos[:, None] * omega[None, :]
        return np.concatenate([np.sin(ang), np.cos(ang)], axis=-1)
    half = d // 2
    er = sincos(np.arange(rows, dtype=np.float64), half)
    ec = sincos(np.arange(GRID_W, dtype=np.float64), half)
    emb = np.concatenate([np.broadcast_to(er[:, None, :], (rows, GRID_W, half)),
                          np.broadcast_to(ec[None, :, :], (rows, GRID_W, half))], axis=-1)
    return emb.reshape(rows * GRID_W, d).astype(np.float32)


@functools.lru_cache(maxsize=None)
def _dft_table(L):
    k = np.arange(L, dtype=np.int64)
    m = (k[:, None] * k[None, :]) % (2 * L)
    ang = np.pi * m.astype(np.float64) / L
    return np.concatenate([np.cos(ang), np.sin(ang)], axis=0).astype(np.float32)


@functools.lru_cache(maxsize=None)
def _filter_feats(L):
    bands = (HYENA_EMB_DIM - 1) // 2
    t01 = np.linspace(0.0, 1.0, L, dtype=np.float32).astype(np.float64)[:, None]
    w = ((2.0 * math.pi / L) * np.arange(L, dtype=np.float32))[:, None].astype(np.float64)
    f = np.linspace(1e-4, bands - 1, bands, dtype=np.float32).astype(np.float64)[None, :]
    feats = np.concatenate([t01, np.cos(f * w), -np.sin(f * w)], axis=-1)
    out = np.zeros((L, MLP_PAD), np.float32)
    out[:, :HYENA_EMB_DIM] = feats
    return out


@functools.lru_cache(maxsize=None)
def _decay_window(L, d):
    max_decay = math.log(HYENA_TARGET) / HYENA_FAST_DECAY
    min_decay = math.log(HYENA_TARGET) / HYENA_SLOW_DECAY
    t01 = np.linspace(0.0, 1.0, L, dtype=np.float32).astype(np.float64)[:, None]
    deltas = np.linspace(min_decay, max_decay, d, dtype=np.float32).astype(np.float64)
    return np.exp(-t01 * np.abs(deltas)[None, :]).astype(np.float32)


@functools.lru_cache(maxsize=None)
def _prefix_table(n):
    return np.triu(np.ones((n, n), np.float32))


def _shift_rows(x, s):
    if s == 0:
        return x
    n = x.shape[0]
    rolled = pltpu.roll(x, s % n, axis=0)
    row = lax.broadcasted_iota(jnp.int32, x.shape, 0)
    keep = row >= s if s > 0 else row < n + s
    return jnp.where(keep, rolled, 0.0)


def _dwconv(x, w_ref, left):
    acc = None
    for k in range(w_ref.shape[0]):
        term = _shift_rows(x, left - k) * w_ref[k:k + 1, :]
        acc = term if acc is None else acc + term
    return acc


def _sigmoid(x):
    return 0.5 * jnp.tanh(0.5 * x) + 0.5


def _rms_modulate(x, g, shift, scale):
    y = x * lax.rsqrt(jnp.mean(x * x, axis=-1, keepdims=True) + NORM_EPS)
    return (y * g) * (1.0 + scale) + shift


def _ada_body(c_ref, w_ref, b_ref, o_ref):
    c = c_ref[...]
    s = c * _sigmoid(c)
    o_ref[...] = jnp.dot(s, w_ref[...], precision=HI, preferred_element_type=F32) + b_ref[...]


def ada_mods(cc, w_ada, b_ada):
    depth, d, n = w_ada.shape
    r = cc.shape[0]
    tn = _tile(n, 1536)
    return pl.pallas_call(
        _ada_body,
        grid=(depth, n // tn),
        in_specs=[pl.BlockSpec((r, d), lambda l, j: (0, 0)),
                  pl.BlockSpec((None, d, tn), lambda l, j: (l, 0, j)),
                  pl.BlockSpec((None, 1, tn), lambda l, j: (l, 0, j))],
        out_specs=pl.BlockSpec((None, r, tn), lambda l, j: (l, 0, j)),
        out_shape=jax.ShapeDtypeStruct((depth, r, n), F32),
        compiler_params=_cparams(2),
        name="ada",
    )(cc, w_ada, b_ada.reshape(depth, 1, n))


def _add_body(x_ref, p_ref, o_ref):
    o_ref[...] = x_ref[...] + p_ref[...]


def add_pos(x, pe):
    b, l, d = x.shape
    tl = _tile(l, 1024)
    return pl.pallas_call(
        _add_body,
        grid=(l // tl, b),
        in_specs=[pl.BlockSpec((None, tl, d), lambda i, bb: (bb, i, 0)),
                  pl.BlockSpec((tl, d), lambda i, bb: (i, 0))],
        out_specs=pl.BlockSpec((None, tl, d), lambda i, bb: (bb, i, 0)),
        out_shape=jax.ShapeDtypeStruct(x.shape, F32),
        compiler_params=_cparams(2),
        name="add_pos",
    )(x, pe)


def _norm_proj_body(x_ref, g_ref, sh_ref, sc_ref, w_ref, o_ref, h_ref):
    @pl.when(pl.program_id(2) == 0)
    def _():
        h_ref[...] = _rms_modulate(x_ref[...], g_ref[...], sh_ref[...], sc_ref[...]).astype(BF16)

    o_ref[...] = jnp.dot(h_ref[...], w_ref[...], preferred_element_type=F32).astype(o_ref.dtype)


def norm_proj(x, g, shift, scale, w):
    b, l, d = x.shape
    n = w.shape[1]
    tl = _tile(l, 1024)
    tn = _tile(n, 1024)
    return pl.pallas_call(
        _norm_proj_body,
        grid=(b, l // tl, n // tn),
        in_specs=[pl.BlockSpec((None, tl, d), lambda bb, i, j: (bb, i, 0)),
                  pl.BlockSpec((1, d), lambda bb, i, j: (0, 0)),
                  pl.BlockSpec((None, 1, d), lambda bb, i, j: (bb, 0, 0)),
                  pl.BlockSpec((None, 1, d), lambda bb, i, j: (bb, 0, 0)),
                  pl.BlockSpec((d, tn), lambda bb, i, j: (0, j))],
        out_specs=pl.BlockSpec((None, tl, tn), lambda bb, i, j: (bb, i, j)),
        out_shape=jax.ShapeDtypeStruct((b, l, n), BF16),
        scratch_shapes=[pltpu.VMEM((tl, d), BF16)],
        compiler_params=_cparams(3),
        name="norm_proj",
    )(x, g.reshape(1, d), shift, scale, w)


def _hy_hidden_body(feats_ref, w1_ref, b1_ref, w2_ref, b2_ref, fr_ref, h_ref):
    fr = fr_ref[...]
    h = jnp.sin(fr * (jnp.dot(feats_ref[...], w1_ref[...], precision=HI,
                              preferred_element_type=F32) + b1_ref[...]))
    h_ref[...] = jnp.sin(fr * (jnp.dot(h, w2_ref[...], precision=HI,
                                       preferred_element_type=F32) + b2_ref[...]))


def _hy_filter_body(h_ref, w3f_ref, w3b_ref, decay_ref, fs_ref, t1_ref, t2_ref, ny_ref):
    L = h_ref.shape[0]
    h = h_ref[...]
    decay = decay_ref[...]
    fwd = jnp.dot(h, w3f_ref[...], precision=HI, preferred_element_type=F32) * decay
    bwd = jnp.dot(h, w3b_ref[...], precision=HI, preferred_element_type=F32) * decay
    row = lax.broadcasted_iota(jnp.int32, fwd.shape, 0)
    bwd = jnp.where(row >= 1, bwd, 0.0)
    inv = 1.0 / (jnp.sum(jnp.abs(fwd), axis=0, keepdims=True)
                 + jnp.sum(jnp.abs(bwd), axis=0, keepdims=True))
    p = fwd + bwd
    q = bwd - fwd
    sign = (1 - 2 * (row & 1)).astype(F32)
    k_nyq = jnp.sum(p * sign, axis=0, keepdims=True)
    k_re = jnp.dot(fs_ref[0:L, :], p.astype(BF16), preferred_element_type=F32)
    k_im = jnp.dot(fs_ref[L:2 * L, :], q.astype(BF16), preferred_element_type=F32)
    n_fft = 2.0 * L
    w_k = jnp.where(row >= 1, 2.0 / n_fft, 1.0 / n_fft)
    t1_ref[...] = k_re * (w_k * inv)
    t2_ref[...] = k_im * ((2.0 / n_fft) * inv)
    ny_ref[...] = k_nyq * (inv / n_fft)


def hy_filter_tables(L, w1, b1, w2, b2, w3, freq, fs):
    d = w3.shape[1] // (HYENA_ORDER * 2)
    fw = w1.shape[1]
    pad = lambda a, r, c: jnp.zeros((r, c), F32).at[:a.shape[0], :a.shape[1]].set(a)
    w1p = pad(w1, MLP_PAD, MLP_PAD)
    w2p = pad(w2, MLP_PAD, MLP_PAD)
    w3p = pad(w3, MLP_PAD, w3.shape[1])
    b1p = pad(b1.reshape(1, fw), 1, MLP_PAD)
    b2p = pad(b2.reshape(1, fw), 1, MLP_PAD)
    frp = pad(freq.reshape(1, fw), 1, MLP_PAD)
    feats = jnp.asarray(_filter_feats(L))
    decay = jnp.asarray(_decay_window(L, d))
    tn = _tile(d, 256)
    nj = d // tn
    full = lambda o, j: (0, 0)
    one = lambda i: (0, 0)
    hidden = pl.pallas_call(
        _hy_hidden_body,
        grid=(1,),
        in_specs=[pl.BlockSpec((L, MLP_PAD), one),
                  pl.BlockSpec((MLP_PAD, MLP_PAD), one),
                  pl.BlockSpec((1, MLP_PAD), one),
                  pl.BlockSpec((MLP_PAD, MLP_PAD), one),
                  pl.BlockSpec((1, MLP_PAD), one),
                  pl.BlockSpec((1, MLP_PAD), one)],
        out_specs=pl.BlockSpec((L, MLP_PAD), one),
        out_shape=jax.ShapeDtypeStruct((L, MLP_PAD), F32),
        compiler_params=_cparams(1),
        name="hy_hidden",
    )(feats, w1p, b1p, w2p, b2p, frp)
    t1, t2, ny = pl.pallas_call(
        _hy_filter_body,
        grid=(HYENA_ORDER, nj),
        in_specs=[pl.BlockSpec((L, MLP_PAD), full),
                  pl.BlockSpec((MLP_PAD, tn), lambda o, j: (0, o * 2 * nj + j)),
                  pl.BlockSpec((MLP_PAD, tn), lambda o, j: (0, o * 2 * nj + nj + j)),
                  pl.BlockSpec((L, tn), lambda o, j: (0, j)),
                  _resident((2 * L, L), full)],
        out_specs=[pl.BlockSpec((L, tn), lambda o, j: (0, o * nj + j)),
                   pl.BlockSpec((L, tn), lambda o, j: (0, o * nj + j)),
                   pl.BlockSpec((1, tn), lambda o, j: (0, o * nj + j))],
        out_shape=[jax.ShapeDtypeStruct((L, HYENA_ORDER * d), F32),
                   jax.ShapeDtypeStruct((L, HYENA_ORDER * d), F32),
                   jax.ShapeDtypeStruct((1, HYENA_ORDER * d), F32)],
        compiler_params=_cparams(2),
        name="hy_filter",
    )(hidden, w3p, w3p, decay, fs)
    return t1, t2, ny


HALO = 8


def _chunks(n_rows, chunk, fn):
    def body(c, carry):
        fn(pl.multiple_of(c * chunk, chunk))
        return carry
    lax.fori_loop(0, n_rows // chunk, body, 0)


def _conv_rows(src_ref, w_ref, b_ref, left, pad_ref, emit):
    L = src_ref.shape[0]
    rc = min(L, 256)
    win = rc + 2 * HALO
    zeros = jnp.zeros((HALO, pad_ref.shape[1]), F32)
    pad_ref[0:HALO, :] = zeros
    pad_ref[L + HALO:L + 2 * HALO, :] = zeros

    def fill(r0):
        pad_ref[pl.ds(r0 + HALO, rc), :] = src_ref[pl.ds(r0, rc), :].astype(F32)
    _chunks(L, rc, fill)

    def conv(r0):
        w = pad_ref[pl.ds(r0, win), :]
        acc = None
        for k in range(w_ref.shape[0]):
            s = left - k
            term = (w if s == 0 else pltpu.roll(w, s % win, axis=0)) * w_ref[k:k + 1, :]
            acc = term if acc is None else acc + term
        out = acc[HALO:HALO + rc]
        emit(r0, out if b_ref is None else out + b_ref[...])
    _chunks(L, rc, conv)


def _long_conv_rows(u_ref, ub_ref, t1_ref, t2_ref, ny_ref, skip, fs_ref, spec_ref, za_ref, zb_ref, emit):
    L, c = u_ref.shape
    rf = min(2 * L, 2048)
    ri = min(L, 1024)
    rm = min(L, 128)

    def forward(r0):
        spec_ref[pl.ds(r0, rf), :] = jnp.dot(fs_ref[pl.ds(r0, rf), :], ub_ref[...], preferred_element_type=F32)
    _chunks(2 * L, rf, forward)

    def sign_col(rows):
        row = lax.broadcasted_iota(jnp.int32, (rows, 1), 0)
        return (1 - 2 * (row & 1)).astype(F32)

    sign_m = sign_col(rm)

    def mix(i, nyq):
        r0 = pl.multiple_of(i * rm, rm)
        ua = spec_ref[pl.ds(r0, rm), :]
        ub = spec_ref[pl.ds(L + r0, rm), :]
        t1 = t1_ref[pl.ds(r0, rm), :]
        t2 = t2_ref[pl.ds(r0, rm), :]
        za_ref[pl.ds(r0, rm), :] = (ua * t1 + ub * t2).astype(BF16)
        zb_ref[pl.ds(r0, rm), :] = (ub * t1 - ua * t2).astype(BF16)
        return nyq + jnp.sum(u_ref[pl.ds(r0, rm), :] * sign_m, axis=0, keepdims=True)
    u_nyq = lax.fori_loop(0, L // rm, mix, jnp.zeros((1, c), F32))
    nyq_term = u_nyq * ny_ref[...]
    sign_i = sign_col(ri)

    def inverse(r0):
        y = jnp.dot(fs_ref[pl.ds(r0, ri), :], za_ref[...], preferred_element_type=F32)
        y = y + jnp.dot(fs_ref[pl.ds(L + r0, ri), :], zb_ref[...], preferred_element_type=F32)
        emit(r0, y + sign_i * nyq_term + u_ref[pl.ds(r0, ri), :] * skip)
    _chunks(L, ri, inverse)


def _hyena_body(p1_ref, p2_ref, pv_ref, cw1_ref, cw2_ref, cwv_ref, cb1_ref, cb2_ref, cbv_ref,
                bias_ref, t1a_ref, t2a_ref, nya_ref, t1b_ref, t2b_ref, nyb_ref, fs_ref, o_ref,
                pad_ref, u_ref, ub_ref, x1_ref, x2_ref, spec_ref, za_ref, zb_ref):
    def put(ref):
        def emit(r0, rows):
            ref[pl.ds(r0, rows.shape[0]), :] = rows
        return emit

    def put_u(r0, rows):
        u_ref[pl.ds(r0, rows.shape[0]), :] = rows
        ub_ref[pl.ds(r0, rows.shape[0]), :] = rows.astype(BF16)

    _conv_rows(p1_ref, cw1_ref, cb1_ref, 1, pad_ref, put(x1_ref))
    _conv_rows(p2_ref, cw2_ref, cb2_ref, 1, pad_ref, put(x2_ref))
    _conv_rows(pv_ref, cwv_ref, cbv_ref, 1, pad_ref, put_u)

    def first(r0, c1):
        put_u(r0, x1_ref[pl.ds(r0, c1.shape[0]), :] * c1)
    _long_conv_rows(u_ref, ub_ref, t1a_ref, t2a_ref, nya_ref, bias_ref[0:1, :], fs_ref,
                    spec_ref, za_ref, zb_ref, first)

    def second(r0, c2):
        o_ref[pl.ds(r0, c2.shape[0]), :] = (x2_ref[pl.ds(r0, c2.shape[0]), :] * c2).astype(o_ref.dtype)
    _long_conv_rows(u_ref, ub_ref, t1b_ref, t2b_ref, nyb_ref, bias_ref[1:2, :], fs_ref,
                    spec_ref, za_ref, zb_ref, second)


def hyena_mixer(proj, conv_w, conv_b, bias, t1, t2, ny, fs):
    b, l, _ = proj.shape
    d = bias.shape[1]
    tn = _tile(d, 256)
    nj = d // tn
    k = conv_w.shape[0]
    cb = conv_b.reshape(1, 3 * d)
    sec = lambda s: pl.BlockSpec((None, l, tn), lambda j, bb: (bb, 0, s * nj + j))
    cws = lambda s: pl.BlockSpec((k, tn), lambda j, bb: (0, s * nj + j))
    cbs = lambda s: pl.BlockSpec((1, tn), lambda j, bb: (0, s * nj + j))
    tab = lambda o, r: _resident((r, tn), lambda j, bb: (0, o * nj + j))
    return pl.pallas_call(
        _hyena_body,
        grid=(nj, b),
        in_specs=[sec(0), sec(1), sec(2), cws(0), cws(1), cws(2), cbs(0), cbs(1), cbs(2),
                  pl.BlockSpec((HYENA_ORDER, tn), lambda j, bb: (0, j)),
                  tab(0, l), tab(0, l), tab(0, 1), tab(1, l), tab(1, l), tab(1, 1),
                  _resident((2 * l, l), lambda j, bb: (0, 0))],
        out_specs=pl.BlockSpec((None, l, tn), lambda j, bb: (bb, 0, j)),
        out_shape=jax.ShapeDtypeStruct((b, l, d), BF16),
        scratch_shapes=[pltpu.VMEM((l + 2 * HALO, tn), F32),
                        pltpu.VMEM((l, tn), F32), pltpu.VMEM((l, tn), BF16),
                        pltpu.VMEM((l, tn), F32), pltpu.VMEM((l, tn), F32),
                        pltpu.VMEM((2 * l, tn), F32),
                        pltpu.VMEM((l, tn), BF16), pltpu.VMEM((l, tn), BF16)],
        compiler_params=_cparams(2),
        name="hyena",
    )(proj, proj, proj, conv_w, conv_w, conv_w, cb, cb, cb, bias, t1, t2, ny, t1, t2, ny, fs)


def _lru_gates_body(rec_ref, cw_ref, cb_ref, wa_ref, ba_ref, wx_ref, bx_ref, lam_ref,
                    af_ref, bf_ref, ab_ref, bb_ref):
    xc = _dwconv(rec_ref[...].astype(F32), cw_ref, 2) + cb_ref[...]
    xcb = xc.astype(BF16)
    xh = 0.5 * xc
    for dr, (a_out, b_out) in enumerate(((af_ref, bf_ref), (ab_ref, bb_ref))):
        tr = jnp.tanh(jnp.dot(xcb, wa_ref[dr], preferred_element_type=F32) + 0.5 * ba_ref[dr:dr + 1, :])
        ti = jnp.tanh(jnp.dot(xcb, wx_ref[dr], preferred_element_type=F32) + 0.5 * bx_ref[dr:dr + 1, :])
        ch = (-0.5 * LRU_C * math.log2(math.e)) * jnp.log1p(jnp.exp(-lam_ref[dr:dr + 1, :]))
        a = jnp.exp2(tr * ch + ch)
        a_out[...] = a
        b_out[...] = jnp.sqrt(1.0 - a * a) * (ti * xh + xh)


def _block_diag_tiles(w, tn):
    two, h, bs, _ = w.shape
    hpt = tn // bs
    wt = w.reshape(two, h // hpt, hpt, bs, bs)
    eye = jnp.eye(hpt, dtype=w.dtype)
    dense = jnp.einsum('tnhij,hg->tnhigj', wt, eye)
    return dense.reshape(two, h // hpt, tn, tn)


def lru_gates(proj, rec_sec, conv_w, conv_b, wa, ba, wx, bx, lam):
    b, l, n = proj.shape
    d = conv_w.shape[1]
    tn = _tile(d, 256)
    nj = d // tn
    k = conv_w.shape[0]
    wa_t = (0.5 * _block_diag_tiles(wa, tn)).astype(BF16)
    wx_t = (0.5 * _block_diag_tiles(wx, tn)).astype(BF16)
    vec = lambda r: pl.BlockSpec((r, tn), lambda bb, j: (0, j))
    wsp = pl.BlockSpec((2, None, tn, tn), lambda bb, j: (0, j, 0, 0))
    out = pl.BlockSpec((None, l, tn), lambda bb, j: (bb, 0, j))
    shp = jax.ShapeDtypeStruct((b, l, d), F32)
    return pl.pallas_call(
        _lru_gates_body,
        grid=(b, nj),
        in_specs=[pl.BlockSpec((None, l, tn), lambda bb, j: (bb, 0, rec_sec * nj + j)),
                  vec(k), vec(1), wsp, vec(2), wsp, vec(2), vec(2)],
        out_specs=[out, out, out, out],
        out_shape=[shp, shp, shp, shp],
        compiler_params=_cparams(2),
        name="lru_gates",
    )(proj, conv_w, conv_b.reshape(1, d), wa_t, ba, wx_t, bx, lam)


def _lru_scan_body(a_ref, b_ref, h0_ref, o_ref, carry_ref, *, reverse):
    tl = a_ref.shape[0]
    groups = tl // 8

    @pl.when(pl.program_id(1) == 0)
    def _():
        carry_ref[0:1, :] = h0_ref[...]

    def step(g, h):
        base = pl.multiple_of((groups - 1 - g if reverse else g) * 8, 8)
        for s in (range(7, -1, -1) if reverse else range(8)):
            h = a_ref[pl.ds(base + s, 1), :] * h + b_ref[pl.ds(base + s, 1), :]
            o_ref[pl.ds(base + s, 1), :] = h
        return h

    carry_ref[0:1, :] = lax.fori_loop(0, groups, step, carry_ref[0:1, :])


def lru_scan(a, b, h0, reverse):
    bsz, l, d = a.shape
    tl = _tile(l, 256)
    nt = l // tl
    tmap = (lambda bb, i: (bb, nt - 1 - i, 0)) if reverse else (lambda bb, i: (bb, i, 0))
    return pl.pallas_call(
        functools.partial(_lru_scan_body, reverse=reverse),
        grid=(bsz, nt),
        in_specs=[pl.BlockSpec((None, tl, d), tmap),
                  pl.BlockSpec((None, tl, d), tmap),
                  pl.BlockSpec((None, 1, d), lambda bb, i: (bb, 0, 0))],
        out_specs=pl.BlockSpec((None, tl, d), tmap),
        out_shape=jax.ShapeDtypeStruct((bsz, l, d), F32),
        scratch_shapes=[pltpu.VMEM((8, d), F32)],
        compiler_params=_cparams(2),
        name="lru_scan",
    )(a, b, h0.reshape(bsz, 1, d))


def _shortconv_body(bg_ref, cg_ref, xv_ref, w_ref, o_ref):
    prod = cg_ref[...].astype(F32) * xv_ref[...].astype(F32)
    o_ref[...] = (bg_ref[...].astype(F32) * _dwconv(prod, w_ref, 1)).astype(o_ref.dtype)


def shortconv_mixer(proj, first_sec, conv_w):
    b, l, _ = proj.shape
    k, d = conv_w.shape
    tn = _tile(d, 256)
    nj = d // tn
    sec = lambda s: pl.BlockSpec((None, l, tn), lambda bb, j: (bb, 0, (first_sec + s) * nj + j))
    return pl.pallas_call(
        _shortconv_body,
        grid=(b, nj),
        in_specs=[sec(0), sec(1), sec(2), pl.BlockSpec((k, tn), lambda bb, j: (0, j))],
        out_specs=pl.BlockSpec((None, l, tn), lambda bb, j: (bb, 0, j)),
        out_shape=jax.ShapeDtypeStruct((b, l, d), BF16),
        compiler_params=_cparams(2),
        name="shortconv",
    )(proj, proj, proj, conv_w)


def _gelu_tanh(x):
    return 0.5 * x * (1.0 + jnp.tanh(math.sqrt(2.0 / math.pi) * (x + 0.044715 * (x * x * x))))


def _merge_body(ya_ref, yc_ref, lg_ref, hf_ref, hb_ref, ga_ref, gb_ref, gc_ref, x_ref,
                mg_ref, g2_ref, sh_ref, sc_ref, wa_ref, wb_ref, wc_ref, wo_ref, wr_ref,
                xo_ref, h2_ref, aff_ref):
    yb = (_gelu_tanh(lg_ref[...].astype(F32)) * (hf_ref[...] + hb_ref[...])).astype(BF16)
    merged = _sigmoid(ga_ref[...].astype(F32)) * jnp.dot(ya_ref[...], wa_ref[...], preferred_element_type=F32)
    merged += _sigmoid(gb_ref[...].astype(F32)) * jnp.dot(yb, wb_ref[...], preferred_element_type=F32)
    merged += _sigmoid(gc_ref[...].astype(F32)) * jnp.dot(yc_ref[...], wc_ref[...], preferred_element_type=F32)
    out = jnp.dot(merged.astype(BF16), wo_ref[...], preferred_element_type=F32)
    x = x_ref[...] + mg_ref[...] * out
    xo_ref[...] = x
    h2 = _rms_modulate(x, g2_ref[...], sh_ref[...], sc_ref[...])
    h2_hi = h2.astype(BF16)
    h2_ref[...] = h2_hi
    h2_lo = (h2 - h2_hi.astype(F32)).astype(BF16)
    wr = wr_ref[...]
    wr_hi = wr.astype(BF16)
    wr_lo = (wr - wr_hi.astype(F32)).astype(BF16)
    logits = (jnp.dot(h2_hi, wr_hi, preferred_element_type=F32)
              + jnp.dot(h2_hi, wr_lo, preferred_element_type=F32)
              + jnp.dot(h2_lo, wr_hi, preferred_element_type=F32))
    e = jnp.exp(logits - jnp.max(logits, axis=-1, keepdims=True))
    aff_ref[...] = e / jnp.sum(e, axis=-1, keepdims=True)


def merge_mixers(proj, gate_sec, lg_sec, ya, yc, hf, hb, x, mod_gate, g2, shift2, scale2,
                 w_a, w_b, w_c, w_o, w_r):
    b, l, d = x.shape
    e = w_r.shape[1]
    tl = _tile(l, 512)
    row = lambda: pl.BlockSpec((None, tl, d), lambda bb, i: (bb, i, 0))
    sec = lambda s: pl.BlockSpec((None, tl, d), lambda bb, i: (bb, i, s))
    modv = lambda: pl.BlockSpec((None, 1, d), lambda bb, i: (bb, 0, 0))
    wsp = lambda: _resident((d, d), lambda bb, i: (0, 0))
    return pl.pallas_call(
        _merge_body,
        grid=(b, l // tl),
        in_specs=[row(), row(), sec(lg_sec), row(), row(),
                  sec(gate_sec), sec(gate_sec + 1), sec(gate_sec + 2), row(),
                  modv(), pl.BlockSpec((1, d), lambda bb, i: (0, 0)), modv(), modv(),
                  wsp(), wsp(), wsp(), wsp(), _resident((d, e), lambda bb, i: (0, 0))],
        out_specs=[row(), row(), pl.BlockSpec((None, tl, e), lambda bb, i: (bb, i, 0))],
        out_shape=[jax.ShapeDtypeStruct((b, l, d), F32),
                   jax.ShapeDtypeStruct((b, l, d), BF16),
                   jax.ShapeDtypeStruct((b, l, e), F32)],
        compiler_params=_cparams(2),
        name="merge",
    )(ya, yc, proj, hf, hb, proj, proj, proj, x, mod_gate, g2.reshape(1, d), shift2, scale2,
      w_a, w_b, w_c, w_o, w_r)


def _select_body(aff_ref, tri_ref, rank_ref, *, cap):
    bits = lax.bitcast_convert_type(aff_ref[...], jnp.int32)

    def refine(i, thr):
        cand = thr | jnp.left_shift(jnp.int32(1), 30 - i)
        cnt = jnp.sum((bits >= cand).astype(F32), axis=1, keepdims=True)
        return jnp.where(cnt >= cap, cand, thr)

    thr = lax.fori_loop(0, 31, refine, jnp.zeros((bits.shape[0], 1), jnp.int32))
    gt = bits > thr
    eq = bits == thr
    n_gt = jnp.sum(gt.astype(F32), axis=1, keepdims=True)
    eq_rank = jnp.dot(eq.astype(BF16), tri_ref[...], preferred_element_type=F32)
    sel = gt | (eq & (eq_rank <= cap - n_gt))
    rank = jnp.dot(sel.astype(BF16), tri_ref[...], preferred_element_type=F32) - 1.0
    rank_ref[...] = jnp.where(sel, rank, -1.0).astype(jnp.int32)


def select_tokens(aff_t, cap):
    r, n = aff_t.shape
    tri = jnp.asarray(_prefix_table(n), BF16)
    return pl.pallas_call(
        functools.partial(_select_body, cap=cap),
        grid=(1,),
        in_specs=[pl.BlockSpec((r, n), lambda i: (0, 0)), pl.BlockSpec((n, n), lambda i: (0, 0))],
        out_specs=pl.BlockSpec((r, n), lambda i: (0, 0)),
        out_shape=jax.ShapeDtypeStruct((r, n), jnp.int32),
        compiler_params=_cparams(1),
        name="select",
    )(aff_t, tri)


def _gather_body(rank_ref, hs_ref, o_ref):
    e = pl.program_id(1)
    cap = o_ref.shape[0]
    n = hs_ref.shape[0]
    rank_row = rank_ref[pl.ds(e, 1), :]
    slot = lax.broadcasted_iota(jnp.int32, (cap, n), 0)
    onehot = jnp.where(slot == rank_row, 1.0, 0.0).astype(BF16)
    o_ref[...] = jnp.dot(onehot, hs_ref[...], preferred_element_type=F32).astype(o_ref.dtype)


def _gather_into_body(rank_ref, hs_ref, prev_ref, o_ref):
    del prev_ref
    _gather_body(rank_ref, hs_ref, o_ref)


def gather_tokens(rank_t, hs, cap, total_rows, row_off, into=None):
    s, e, n = rank_t.shape
    d = hs.shape[2]
    boff = row_off // cap
    in_specs = [pl.BlockSpec((None, e, n), lambda ss, ee: (ss, 0, 0)),
                pl.BlockSpec((None, n, d), lambda ss, ee: (ss, 0, 0))]
    args = [rank_t, hs]
    if into is not None:
        in_specs.append(pl.BlockSpec(memory_space=pl.ANY))
        args.append(into)
    return pl.pallas_call(
        _gather_body if into is None else _gather_into_body,
        grid=(s, e),
        in_specs=in_specs,
        out_specs=pl.BlockSpec((None, cap, d), lambda ss, ee: (ee, boff + ss, 0)),
        out_shape=jax.ShapeDtypeStruct((e, total_rows, d), BF16),
        input_output_aliases={} if into is None else {2: 0},
        compiler_params=_cparams(2),
        name="gather",
    )(*args)


def _ffn_body(x_ref, wg_ref, wu_ref, wd_ref, o_ref, acc_ref):
    f = pl.program_id(1)

    @pl.when(f == 0)
    def _():
        acc_ref[...] = jnp.zeros_like(acc_ref)

    x = x_ref[...]
    g = jnp.dot(x, wg_ref[...].astype(BF16), preferred_element_type=F32)
    u = jnp.dot(x, wu_ref[...].astype(BF16), preferred_element_type=F32)
    hid = (g * _sigmoid(g) * u).astype(BF16)
    acc_ref[...] += jnp.dot(hid, wd_ref[...].astype(BF16), preferred_element_type=F32)

    @pl.when(f == pl.num_programs(1) - 1)
    def _():
        o_ref[...] = acc_ref[...].astype(o_ref.dtype)


def expert_ffn(xs, wg, wu, wd, layer):
    e, m, d = xs.shape
    fdim = wg.shape[3]
    tf = _tile(fdim, 256)
    return pl.pallas_call(
        _ffn_body,
        grid=(e, fdim // tf),
        in_specs=[pl.BlockSpec((None, m, d), lambda ee, f: (ee, 0, 0)),
                  pl.BlockSpec((None, None, d, tf), lambda ee, f: (layer, ee, 0, f)),
                  pl.BlockSpec((None, None, d, tf), lambda ee, f: (layer, ee, 0, f)),
                  pl.BlockSpec((None, None, tf, d), lambda ee, f: (layer, ee, f, 0))],
        out_specs=pl.BlockSpec((None, m, d), lambda ee, f: (ee, 0, 0)),
        out_shape=jax.ShapeDtypeStruct((e, m, d), BF16),
        scratch_shapes=[pltpu.VMEM((m, d), F32)],
        compiler_params=_cparams(2),
        name="ffn",
    )(xs, wg, wu, wd)


def _scatter_body(rank_ref, aff_ref, ys_ref, x_ref, mg_ref, gf_ref, o_ref, *, final_norm):
    tq = rank_ref.shape[0]
    n_exp, cap, _ = ys_ref.shape
    slot = lax.broadcasted_iota(jnp.int32, (tq, cap), 1)
    rank = rank_ref[...]
    aff = aff_ref[...]
    acc = None
    for e in range(n_exp):
        q = jnp.where(slot == rank[:, e:e + 1], aff[:, e:e + 1], 0.0).astype(BF16)
        part = jnp.dot(q, ys_ref[e], preferred_element_type=F32)
        acc = part if acc is None else acc + part
    x = x_ref[...] + mg_ref[...] * acc
    if final_norm:
        x = x * lax.rsqrt(jnp.mean(x * x, axis=-1, keepdims=True) + NORM_EPS) * gf_ref[...]
    o_ref[...] = x


def scatter_residual(rank, aff, ys, row_off, cap, x, mod_gate, final_g):
    s, n, e = rank.shape
    d = x.shape[2]
    tq = _tile(n, 512)
    boff = row_off // cap
    final_norm = final_g is not None
    gf = (final_g if final_norm else jnp.ones((d,), F32)).reshape(1, d)
    return pl.pallas_call(
        functools.partial(_scatter_body, final_norm=final_norm),
        grid=(s, n // tq),
        in_specs=[pl.BlockSpec((None, tq, e), lambda ss, i: (ss, i, 0)),
                  pl.BlockSpec((None, tq, e), lambda ss, i: (ss, i, 0)),
                  pl.BlockSpec((e, cap, d), lambda ss, i: (0, boff + ss, 0)),
                  pl.BlockSpec((None, tq, d), lambda ss, i: (ss, i, 0)),
                  pl.BlockSpec((None, 1, d), lambda ss, i: (ss, 0, 0)),
                  pl.BlockSpec((1, d), lambda ss, i: (0, 0))],
        out_specs=pl.BlockSpec((None, tq, d), lambda ss, i: (ss, i, 0)),
        out_shape=jax.ShapeDtypeStruct(x.shape, F32),
        compiler_params=_cparams(2),
        name="scatter",
    )(rank, aff, ys, x, mod_gate, gf)


def _mixer_and_route(x, h0_f, h0_b, mods, lw, fs, shared_mod=False):
    b, l, d = x.shape
    if shared_mod:
        flat = lambda a: a.reshape(1, b * l, a.shape[-1])
        unflat = lambda a: a.reshape(b, l, a.shape[-1])
        mods = [m[:1] for m in mods]
    else:
        flat = unflat = lambda a: a
    proj = unflat(norm_proj(flat(x), lw['norm1_g'], mods[0], mods[1], lw['w_in']))
    t1, t2, ny = hy_filter_tables(l, lw['hy_filt_w1'], lw['hy_filt_b1'], lw['hy_filt_w2'], lw['hy_filt_b2'],
                                  lw['hy_filt_w3'], lw['hy_filt_freq'], fs)
    ya = hyena_mixer(proj, lw['hy_conv_w'], lw['hy_conv_b'], lw['hy_bias'], t1, t2, ny, fs)
    a_f, b_f, a_b, b_b = lru_gates(proj, 4, lw['lru_conv_w'], lw['lru_conv_b'], lw['lru_wa'], lw['lru_ba'],
                                   lw['lru_wx'], lw['lru_bx'], lw['lru_lambda'])
    hf = lru_scan(a_f, b_f, h0_f, False)
    hb = lru_scan(a_b, b_b, h0_b, True)
    yc = shortconv_mixer(proj, 5, lw['sc_conv_w'])
    x_new, h2, aff = merge_mixers(flat(proj), 8, 3, flat(ya), flat(yc), flat(hf), flat(hb), flat(x), mods[2],
                                  lw['norm2_g'], mods[3], mods[4],
                                  lw['w_hy_out'], lw['w_lru_out'], lw['w_sc_out'], lw['w_o'], lw['w_router'])
    return unflat(x_new), unflat(h2), unflat(aff), hf[:, -1], hb[:, 0]


def _capacity(n, e):
    return EC_CAPACITY * n // e


def _route(h2, aff, total_rows, row_off, into):
    s, n, e = aff.shape
    cap = _capacity(n, e)
    aff_t = jnp.transpose(aff, (0, 2, 1))
    rank_t = select_tokens(aff_t.reshape(s * e, n), cap).reshape(s, e, n)
    xs = gather_tokens(rank_t, h2, cap, total_rows, row_off, into)
    return xs, jnp.transpose(rank_t, (0, 2, 1))


def kernel(x, c, ctx, c_ctx, w_ada, b_ada, norm1_g, norm2_g, w_in, hy_conv_w, hy_conv_b, hy_filt_w1, hy_filt_b1, hy_filt_w2, hy_filt_b2, hy_filt_w3, hy_filt_freq, hy_bias, lru_conv_w, lru_conv_b, lru_wa, lru_ba, lru_wx, lru_bx, lru_lambda, sc_conv_w, w_hy_out, w_lru_out, w_sc_out, w_o, w_router, w_exp_gate, w_exp_up, w_exp_down, final_norm_g):
    bsz, n_lat, d = x.shape
    n_ctx = ctx.shape[1]
    depth = w_ada.shape[0]
    rows = n_lat // GRID_W

    x = add_pos(x, jnp.asarray(_pos_embed(rows, d)))
    xc = ctx
    fs_lat = jnp.asarray(_dft_table(n_lat), BF16)
    fs_ctx = jnp.asarray(_dft_table(n_ctx), BF16)

    cc = jnp.zeros((2 * bsz, d), F32).at[:bsz].set(c).at[bsz].set(c_ctx)
    mods = ada_mods(cc, w_ada, b_ada)
    zeros_state = jnp.zeros((bsz, d), F32)

    for l in range(depth):
        last = l == depth - 1
        lw = {
            'norm1_g': norm1_g[l], 'norm2_g': norm2_g[l], 'w_in': w_in[l].astype(BF16),
            'hy_conv_w': hy_conv_w[l], 'hy_conv_b': hy_conv_b[l],
            'hy_filt_w1': hy_filt_w1[l], 'hy_filt_b1': hy_filt_b1[l], 'hy_filt_w2': hy_filt_w2[l],
            'hy_filt_b2': hy_filt_b2[l], 'hy_filt_w3': hy_filt_w3[l], 'hy_filt_freq': hy_filt_freq[l],
            'hy_bias': hy_bias[l], 'lru_conv_w': lru_conv_w[l], 'lru_conv_b': lru_conv_b[l],
            'lru_wa': lru_wa[l], 'lru_ba': lru_ba[l], 'lru_wx': lru_wx[l], 'lru_bx': lru_bx[l],
            'lru_lambda': lru_lambda[l], 'sc_conv_w': sc_conv_w[l],
            'w_hy_out': w_hy_out[l].astype(BF16), 'w_lru_out': w_lru_out[l].astype(BF16),
            'w_sc_out': w_sc_out[l].astype(BF16), 'w_o': w_o[l].astype(BF16), 'w_router': w_router[l],
        }
        mod_l = [mods[l, :bsz, k * d:(k + 1) * d].reshape(bsz, 1, d) for k in range(6)]
        mod_c = [jnp.broadcast_to(mods[l, bsz, k * d:(k + 1) * d].reshape(1, 1, d), (bsz, 1, d)) for k in range(6)]

        n_exp = w_router.shape[2]
        cap_l = _capacity(n_lat, n_exp)
        cap_c = _capacity(n_ctx, n_exp)
        rows_l = bsz * cap_l
        if last:
            proj_c = norm_proj(xc.reshape(1, bsz * n_ctx, d), lw['norm1_g'], mod_c[0][:1], mod_c[1][:1],
                               lw['w_in'][:, 4 * d:5 * d]).reshape(bsz, n_ctx, d)
            a_f, b_f, a_b, b_b = lru_gates(proj_c, 0, lw['lru_conv_w'], lw['lru_conv_b'], lw['lru_wa'], lw['lru_ba'],
                                           lw['lru_wx'], lw['lru_bx'], lw['lru_lambda'])
            state_f = lru_scan(a_f, b_f, zeros_state, False)[:, -1]
            state_b = lru_scan(a_b, b_b, zeros_state, True)[:, 0]
            total_rows = rows_l
            xs = None
        else:
            xc_mid, h2_c, aff_c, state_f, state_b = _mixer_and_route(xc, zeros_state, zeros_state, mod_c, lw, fs_ctx,
                                                                     shared_mod=True)
            total_rows = rows_l + bsz * cap_c
            xs, rank_c = _route(h2_c, aff_c, total_rows, rows_l, None)

        x_mid, h2_l, aff_l, _, _ = _mixer_and_route(x, state_f, state_b, mod_l, lw, fs_lat)
        xs, rank_l = _route(h2_l, aff_l, total_rows, 0, xs)
        ys = expert_ffn(xs, w_exp_gate, w_exp_up, w_exp_down, l)

        if not last:
            xc = scatter_residual(rank_c, aff_c, ys, rows_l, cap_c, xc_mid, mod_c[5], None)
        x = scatter_residual(rank_l, aff_l, ys, 0, cap_l, x_mid, mod_l[5], final_norm_g if last else None)
    return x
```

```python
import functools
import math

import numpy as np
import jax
import jax.numpy as jnp
from jax import lax
from jax.experimental import pallas as pl
from jax.experimental.pallas import tpu as pltpu

F32 = jnp.float32
BF16 = jnp.bfloat16
HI = lax.Precision.HIGHEST

NORM_EPS = 1e-6
GRID_W = 64
N_EXPERTS = 16
EC_CAPACITY = 2
LRU_C = 8.0
HYENA_ORDER = 2
HYENA_EMB_DIM = 33
HYENA_FAST_DECAY = 0.3
HYENA_SLOW_DECAY = 1.5
HYENA_TARGET = 1e-2
MLP_PAD = 128

V7X_VMEM_BYTES = 64 * 1024 * 1024
VMEM_LIMIT = 56 * 1024 * 1024
assert VMEM_LIMIT < V7X_VMEM_BYTES


def _cparams(n_grid):
    return pltpu.CompilerParams(dimension_semantics=("arbitrary",) * n_grid,
                                vmem_limit_bytes=VMEM_LIMIT)


def _tile(n, pref):
    if n <= pref:
        return n
    t = pref
    while n % t:
        t //= 2
    return t


def _resident(block_shape, index_map):
    return pl.BlockSpec(block_shape, index_map, pipeline_mode=pl.Buffered(1))


@functools.lru_cache(maxsize=None)
def _pos_embed(rows, d):
    def sincos(pos, dim):
        half = dim // 2
        omega = 1.0 / (10000.0 ** (np.arange(half, dtype=np.float64) / half))
        ang = pos[:, None] * omega[None, :]
        return np.concatenate([np.sin(ang), np.cos(ang)], axis=-1)
    half = d // 2
    er = sincos(np.arange(rows, dtype=np.float64), half)
    ec = sincos(np.arange(GRID_W, dtype=np.float64), half)
    emb = np.concatenate([np.broadcast_to(er[:, None, :], (rows, GRID_W, half)),
                          np.broadcast_to(ec[None, :, :], (rows, GRID_W, half))], axis=-1)
    return emb.reshape(rows * GRID_W, d).astype(np.float32)


@functools.lru_cache(maxsize=None)
def _dft_table(L):
    h = L // 2
    k = np.arange(h, dtype=np.int64)
    ang_e = np.pi * ((k[:, None] * (2 * k[None, :])) % (2 * L)).astype(np.float64) / L
    ang_o = np.pi * ((k[:, None] * (2 * k[None, :] + 1)) % (2 * L)).astype(np.float64) / L
    ce, se, co, so = np.cos(ang_e), np.sin(ang_e), np.cos(ang_o), np.sin(ang_o)
    return np.concatenate([ce, se, co, so, co.T, so.T], axis=0).astype(np.float32)


@functools.lru_cache(maxsize=None)
def _filter_feats(L):
    bands = (HYENA_EMB_DIM - 1) // 2
    t01 = np.linspace(0.0, 1.0, L, dtype=np.float32).astype(np.float64)[:, None]
    w = ((2.0 * math.pi / L) * np.arange(L, dtype=np.float32))[:, None].astype(np.float64)
    f = np.linspace(1e-4, bands - 1, bands, dtype=np.float32).astype(np.float64)[None, :]
    feats = np.concatenate([t01, np.cos(f * w), -np.sin(f * w)], axis=-1)
    out = np.zeros((L, MLP_PAD), np.float32)
    out[:, :HYENA_EMB_DIM] = feats
    return out


@functools.lru_cache(maxsize=None)
def _decay_window(L, d):
    max_decay = math.log(HYENA_TARGET) / HYENA_FAST_DECAY
    min_decay = math.log(HYENA_TARGET) / HYENA_SLOW_DECAY
    t01 = np.linspace(0.0, 1.0, L, dtype=np.float32).astype(np.float64)[:, None]
    deltas = np.linspace(min_decay, max_decay, d, dtype=np.float32).astype(np.float64)
    return np.exp(-t01 * np.abs(deltas)[None, :]).astype(np.float32)


@functools.lru_cache(maxsize=None)
def _prefix_table(n):
    return np.triu(np.ones((n, n), np.float32))


def _shift_rows(x, s):
    if s == 0:
        return x
    n = x.shape[0]
    rolled = pltpu.roll(x, s % n, axis=0)
    row = lax.broadcasted_iota(jnp.int32, x.shape, 0)
    keep = row >= s if s > 0 else row < n + s
    return jnp.where(keep, rolled, 0.0)


def _dwconv(x, w_ref, left):
    acc = None
    for k in range(w_ref.shape[0]):
        term = _shift_rows(x, left - k) * w_ref[k:k + 1, :]
        acc = term if acc is None else acc + term
    return acc


def _sigmoid(x):
    return 0.5 * jnp.tanh(0.5 * x) + 0.5


def _rms_modulate(x, g, shift, scale):
    y = x * lax.rsqrt(jnp.mean(x * x, axis=-1, keepdims=True) + NORM_EPS)
    return (y * g) * (1.0 + scale) + shift


def _ada_body(c_ref, w_ref, b_ref, o_ref):
    c = c_ref[...]
    s = c * _sigmoid(c)
    o_ref[...] = jnp.dot(s, w_ref[...], precision=HI, preferred_element_type=F32) + b_ref[...]


def ada_mods(cc, w_ada, b_ada):
    depth, d, n = w_ada.shape
    r = cc.shape[0]
    tn = _tile(n, 1536)
    return pl.pallas_call(
        _ada_body,
        grid=(depth, n // tn),
        in_specs=[pl.BlockSpec((r, d), lambda l, j: (0, 0)),
                  pl.BlockSpec((None, d, tn), lambda l, j: (l, 0, j)),
                  pl.BlockSpec((None, 1, tn), lambda l, j: (l, 0, j))],
        out_specs=pl.BlockSpec((None, r, tn), lambda l, j: (l, 0, j)),
        out_shape=jax.ShapeDtypeStruct((depth, r, n), F32),
        compiler_params=_cparams(2),
        name="ada",
    )(cc, w_ada, b_ada.reshape(depth, 1, n))


def _add_body(x_ref, p_ref, o_ref):
    o_ref[...] = x_ref[...] + p_ref[...]


def add_pos(x, pe):
    b, l, d = x.shape
    tl = _tile(l, 1024)
    return pl.pallas_call(
        _add_body,
        grid=(l // tl, b),
        in_specs=[pl.BlockSpec((None, tl, d), lambda i, bb: (bb, i, 0)),
                  pl.BlockSpec((tl, d), lambda i, bb: (i, 0))],
        out_specs=pl.BlockSpec((None, tl, d), lambda i, bb: (bb, i, 0)),
        out_shape=jax.ShapeDtypeStruct(x.shape, F32),
        compiler_params=_cparams(2),
        name="add_pos",
    )(x, pe)


def _norm_proj_body(x_ref, g_ref, sh_ref, sc_ref, w_ref, o_ref, h_ref):
    @pl.when(pl.program_id(2) == 0)
    def _():
        h_ref[...] = _rms_modulate(x_ref[...], g_ref[...], sh_ref[...], sc_ref[...]).astype(BF16)

    o_ref[...] = jnp.dot(h_ref[...], w_ref[...], preferred_element_type=F32).astype(o_ref.dtype)


def norm_proj(x, g, shift, scale, w):
    b, l, d = x.shape
    n = w.shape[1]
    tl = _tile(l, 1024)
    tn = _tile(n, 1024)
    return pl.pallas_call(
        _norm_proj_body,
        grid=(b, l // tl, n // tn),
        in_specs=[pl.BlockSpec((None, tl, d), lambda bb, i, j: (bb, i, 0)),
                  pl.BlockSpec((1, d), lambda bb, i, j: (0, 0)),
                  pl.BlockSpec((None, 1, d), lambda bb, i, j: (bb, 0, 0)),
                  pl.BlockSpec((None, 1, d), lambda bb, i, j: (bb, 0, 0)),
                  pl.BlockSpec((d, tn), lambda bb, i, j: (0, j))],
        out_specs=pl.BlockSpec((None, tl, tn), lambda bb, i, j: (bb, i, j)),
        out_shape=jax.ShapeDtypeStruct((b, l, n), BF16),
        scratch_shapes=[pltpu.VMEM((tl, d), BF16)],
        compiler_params=_cparams(3),
        name="norm_proj",
    )(x, g.reshape(1, d), shift, scale, w)


def _hy_hidden_body(feats_ref, w1_ref, b1_ref, w2_ref, b2_ref, fr_ref, h_ref):
    fr = fr_ref[...]
    h = jnp.sin(fr * (jnp.dot(feats_ref[...], w1_ref[...], precision=HI,
                              preferred_element_type=F32) + b1_ref[...]))
    h_ref[...] = jnp.sin(fr * (jnp.dot(h, w2_ref[...], precision=HI,
                                       preferred_element_type=F32) + b2_ref[...]))


LANES = 128


def _sign_col(rows):
    row = lax.broadcasted_iota(jnp.int32, (rows, 1), 0)
    return (1 - 2 * (row & 1)).astype(F32)


def _put_slabs(nat_ref, r0, rows):
    for s in range(nat_ref.shape[0]):
        nat_ref[s, pl.ds(r0, rows.shape[0]), :] = rows[:, s * LANES:(s + 1) * LANES]


def _get_slabs(nat_ref, r0, n):
    return jnp.concatenate([nat_ref[s, pl.ds(r0, n), :] for s in range(nat_ref.shape[0])], axis=1)


def _split_rows(nat_ref, e_ref, o_ref):
    nslab, L, _ = nat_ref.shape
    h = L // 2
    rd = min(h, 256)
    sgn = _sign_col(rd)

    def body(i, carry):
        m0 = pl.multiple_of(i * rd, rd)
        e = jnp.concatenate([nat_ref[s, pl.ds(2 * m0, rd, stride=2), :] for s in range(nslab)], axis=1)
        o = jnp.concatenate([nat_ref[s, pl.ds(2 * m0 + 1, rd, stride=2), :] for s in range(nslab)], axis=1)
        e_ref[pl.ds(m0, rd), :] = e.astype(BF16)
        o_ref[pl.ds(m0, rd), :] = o.astype(BF16)
        return (carry[0] + jnp.sum(e * sgn, axis=0, keepdims=True),
                carry[1] + jnp.sum(o * sgn, axis=0, keepdims=True))

    zero = jnp.zeros((1, nslab * LANES), F32)
    return lax.fori_loop(0, h // rd, body, (zero, zero))


def _hy_filter_body(h_ref, w3f_ref, w3b_ref, decay_ref, tb_ref, t1_ref, t2_ref, mid_ref,
                    p_ref, q_ref, pe_ref, po_ref, qe_ref, qo_ref):
    L = h_ref.shape[0]
    hl = L // 2
    hid = h_ref[...]
    decay = decay_ref[...]
    fwd = jnp.dot(hid, w3f_ref[...], precision=HI, preferred_element_type=F32) * decay
    bwd = jnp.dot(hid, w3b_ref[...], precision=HI, preferred_element_type=F32) * decay
    row = lax.broadcasted_iota(jnp.int32, fwd.shape, 0)
    bwd = jnp.where(row >= 1, bwd, 0.0)
    inv = 1.0 / (jnp.sum(jnp.abs(fwd), axis=0, keepdims=True)
                 + jnp.sum(jnp.abs(bwd), axis=0, keepdims=True))
    _put_slabs(p_ref, 0, fwd + bwd)
    _put_slabs(q_ref, 0, bwd - fwd)
    k_re_mid, _ = _split_rows(p_ref, pe_ref, po_ref)
    _, k_im_mid = _split_rows(q_ref, qe_ref, qo_ref)
    a_e = jnp.dot(tb_ref[0:hl, :], pe_ref[...], preferred_element_type=F32)
    a_o = jnp.dot(tb_ref[2 * hl:3 * hl, :], po_ref[...], preferred_element_type=F32)
    b_e = jnp.dot(tb_ref[hl:2 * hl, :], qe_ref[...], preferred_element_type=F32)
    b_o = jnp.dot(tb_ref[3 * hl:4 * hl, :], qo_ref[...], preferred_element_type=F32)
    n_fft = 2.0 * L
    row_h = lax.broadcasted_iota(jnp.int32, (hl, fwd.shape[1]), 0)
    w_k = jnp.where(row_h >= 1, 2.0 / n_fft, 1.0 / n_fft) * inv
    w_s = (2.0 / n_fft) * inv
    t1_ref[0:hl, :] = (a_e + a_o) * w_k
    t1_ref[hl:L, :] = (a_e - a_o) * w_k
    t2_ref[0:hl, :] = (b_e + b_o) * w_s
    t2_ref[hl:L, :] = (b_o - b_e) * w_s
    mid_ref[0:1, :] = k_re_mid * w_s
    mid_ref[1:2, :] = k_im_mid * w_s


def hy_filter_tables(L, w1, b1, w2, b2, w3, freq, fs):
    d = w3.shape[1] // (HYENA_ORDER * 2)
    fw = w1.shape[1]
    pad = lambda a, r, c: jnp.zeros((r, c), F32).at[:a.shape[0], :a.shape[1]].set(a)
    w1p = pad(w1, MLP_PAD, MLP_PAD)
    w2p = pad(w2, MLP_PAD, MLP_PAD)
    w3p = pad(w3, MLP_PAD, w3.shape[1])
    b1p = pad(b1.reshape(1, fw), 1, MLP_PAD)
    b2p = pad(b2.reshape(1, fw), 1, MLP_PAD)
    frp = pad(freq.reshape(1, fw), 1, MLP_PAD)
    feats = jnp.asarray(_filter_feats(L))
    decay = jnp.asarray(_decay_window(L, d))
    tn = _tile(d, 256)
    nj = d // tn
    full = lambda o, j: (0, 0)
    one = lambda i: (0, 0)
    hidden = pl.pallas_call(
        _hy_hidden_body,
        grid=(1,),
        in_specs=[pl.BlockSpec((L, MLP_PAD), one),
                  pl.BlockSpec((MLP_PAD, MLP_PAD), one),
                  pl.BlockSpec((1, MLP_PAD), one),
                  pl.BlockSpec((MLP_PAD, MLP_PAD), one),
                  pl.BlockSpec((1, MLP_PAD), one),
                  pl.BlockSpec((1, MLP_PAD), one)],
        out_specs=pl.BlockSpec((L, MLP_PAD), one),
        out_shape=jax.ShapeDtypeStruct((L, MLP_PAD), F32),
        compiler_params=_cparams(1),
        name="hy_hidden",
    )(feats, w1p, b1p, w2p, b2p, frp)
    hl = L // 2
    t1, t2, mid = pl.pallas_call(
        _hy_filter_body,
        grid=(HYENA_ORDER, nj),
        in_specs=[pl.BlockSpec((L, MLP_PAD), full),
                  pl.BlockSpec((MLP_PAD, tn), lambda o, j: (0, o * 2 * nj + j)),
                  pl.BlockSpec((MLP_PAD, tn), lambda o, j: (0, o * 2 * nj + nj + j)),
                  pl.BlockSpec((L, tn), lambda o, j: (0, j)),
                  _resident(fs.shape, full)],
        out_specs=[pl.BlockSpec((L, tn), lambda o, j: (0, o * nj + j)),
                   pl.BlockSpec((L, tn), lambda o, j: (0, o * nj + j)),
                   pl.BlockSpec((2, tn), lambda o, j: (0, o * nj + j))],
        out_shape=[jax.ShapeDtypeStruct((L, HYENA_ORDER * d), F32),
                   jax.ShapeDtypeStruct((L, HYENA_ORDER * d), F32),
                   jax.ShapeDtypeStruct((2, HYENA_ORDER * d), F32)],
        scratch_shapes=[pltpu.VMEM((tn // LANES, L, LANES), F32), pltpu.VMEM((tn // LANES, L, LANES), F32),
                        pltpu.VMEM((hl, tn), BF16), pltpu.VMEM((hl, tn), BF16),
                        pltpu.VMEM((hl, tn), BF16), pltpu.VMEM((hl, tn), BF16)],
        compiler_params=_cparams(2),
        name="hy_filter",
    )(hidden, w3p, w3p, decay, fs)
    return t1, t2, mid


HALO = 8


def _chunks(n_rows, chunk, fn):
    def body(c, carry):
        fn(pl.multiple_of(c * chunk, chunk))
        return carry
    lax.fori_loop(0, n_rows // chunk, body, 0)


def _conv_rows(src_ref, w_ref, b_ref, left, pad_ref, emit):
    L = src_ref.shape[0]
    rc = min(L, 256)
    win = rc + 2 * HALO
    zeros = jnp.zeros((HALO, pad_ref.shape[1]), F32)
    pad_ref[0:HALO, :] = zeros
    pad_ref[L + HALO:L + 2 * HALO, :] = zeros

    def fill(r0):
        pad_ref[pl.ds(r0 + HALO, rc), :] = src_ref[pl.ds(r0, rc), :].astype(F32)
    _chunks(L, rc, fill)

    def conv(r0):
        w = pad_ref[pl.ds(r0, win), :]
        acc = None
        for k in range(w_ref.shape[0]):
            s = left - k
            term = (w if s == 0 else pltpu.roll(w, s % win, axis=0)) * w_ref[k:k + 1, :]
            acc = term if acc is None else acc + term
        out = acc[HALO:HALO + rc]
        emit(r0, out if b_ref is None else out + b_ref[...])
    _chunks(L, rc, conv)


def _long_conv_rows(u_ref, t1_ref, t2_ref, mid_ref, skip, tb_ref, ue_ref, uo_ref, spec_e_ref, spec_o_ref,
                    xec_ref, xes_ref, xoc_ref, xos_ref, c_ref, emit):
    nslab, L, _ = u_ref.shape
    h = L // 2
    rm = min(h, 64)
    ri = min(h, 1024)
    rc = min(L, 256)

    ua_mid, ub_mid = _split_rows(u_ref, ue_ref, uo_ref)
    spec_e_ref[...] = jnp.dot(tb_ref[0:2 * h, :], ue_ref[...], preferred_element_type=F32)
    spec_o_ref[...] = jnp.dot(tb_ref[2 * h:4 * h, :], uo_ref[...], preferred_element_type=F32)

    def mix(r0):
        ae = spec_e_ref[pl.ds(r0, rm), :]
        be = spec_e_ref[pl.ds(h + r0, rm), :]
        ao = spec_o_ref[pl.ds(r0, rm), :]
        bo = spec_o_ref[pl.ds(h + r0, rm), :]
        p0, p1, q0, q1 = ae + ao, ae - ao, be + bo, bo - be
        t1_lo = t1_ref[pl.ds(r0, rm), :]
        t2_lo = t2_ref[pl.ds(r0, rm), :]
        t1_hi = t1_ref[pl.ds(h + r0, rm), :]
        t2_hi = t2_ref[pl.ds(h + r0, rm), :]
        za0 = p0 * t1_lo + q0 * t2_lo
        zb0 = q0 * t1_lo - p0 * t2_lo
        za1 = p1 * t1_hi + q1 * t2_hi
        zb1 = q1 * t1_hi - p1 * t2_hi
        xec_ref[pl.ds(r0, rm), :] = (za0 + za1).astype(BF16)
        xes_ref[pl.ds(r0, rm), :] = (zb0 - zb1).astype(BF16)
        xoc_ref[pl.ds(r0, rm), :] = (za0 - za1).astype(BF16)
        xos_ref[pl.ds(r0, rm), :] = (zb0 + zb1).astype(BF16)
    _chunks(h, rm, mix)

    t1_mid = mid_ref[0:1, :]
    t2_mid = mid_ref[1:2, :]
    za_mid = ua_mid * t1_mid + ub_mid * t2_mid
    zb_mid = ub_mid * t1_mid - ua_mid * t2_mid
    sgn = _sign_col(ri)

    def inverse(r0):
        ye = jnp.dot(tb_ref[pl.ds(r0, ri), :], xec_ref[...], preferred_element_type=F32)
        ye = ye + jnp.dot(tb_ref[pl.ds(h + r0, ri), :], xes_ref[...], preferred_element_type=F32)
        yo = jnp.dot(tb_ref[pl.ds(4 * h + r0, ri), :], xoc_ref[...], preferred_element_type=F32)
        yo = yo + jnp.dot(tb_ref[pl.ds(5 * h + r0, ri), :], xos_ref[...], preferred_element_type=F32)
        ye = ye + sgn * za_mid
        yo = yo + sgn * zb_mid
        for s in range(nslab):
            c_ref[s, pl.ds(2 * r0, ri, stride=2), :] = ye[:, s * LANES:(s + 1) * LANES]
            c_ref[s, pl.ds(2 * r0 + 1, ri, stride=2), :] = yo[:, s * LANES:(s + 1) * LANES]
    _chunks(h, ri, inverse)

    def finish(r0):
        emit(r0, _get_slabs(c_ref, r0, rc) + _get_slabs(u_ref, r0, rc) * skip)
    _chunks(L, rc, finish)


def _hyena_body(p1_ref, p2_ref, pv_ref, cw1_ref, cw2_ref, cwv_ref, cb1_ref, cb2_ref, cbv_ref,
                bias_ref, t1a_ref, t2a_ref, mida_ref, t1b_ref, t2b_ref, midb_ref, tb_ref, o_ref,
                pad_ref, u_ref, x1_ref, x2_ref, ue_ref, uo_ref, spec_e_ref, spec_o_ref,
                xec_ref, xes_ref, xoc_ref, xos_ref, c_ref):
    def put(ref):
        def emit(r0, rows):
            ref[pl.ds(r0, rows.shape[0]), :] = rows
        return emit

    def put_u(r0, rows):
        _put_slabs(u_ref, r0, rows)

    _conv_rows(p1_ref, cw1_ref, cb1_ref, 1, pad_ref, put(x1_ref))
    _conv_rows(p2_ref, cw2_ref, cb2_ref, 1, pad_ref, put(x2_ref))
    _conv_rows(pv_ref, cwv_ref, cbv_ref, 1, pad_ref, put_u)
    work = (tb_ref, ue_ref, uo_ref, spec_e_ref, spec_o_ref, xec_ref, xes_ref, xoc_ref, xos_ref, c_ref)

    def first(r0, c1):
        put_u(r0, x1_ref[pl.ds(r0, c1.shape[0]), :] * c1)
    _long_conv_rows(u_ref, t1a_ref, t2a_ref, mida_ref, bias_ref[0:1, :], *work, first)

    def second(r0, c2):
        o_ref[pl.ds(r0, c2.shape[0]), :] = (x2_ref[pl.ds(r0, c2.shape[0]), :] * c2).astype(o_ref.dtype)
    _long_conv_rows(u_ref, t1b_ref, t2b_ref, midb_ref, bias_ref[1:2, :], *work, second)


def hyena_mixer(proj, conv_w, conv_b, bias, t1, t2, mid, fs):
    b, l, _ = proj.shape
    d = bias.shape[1]
    tn = _tile(d, 256)
    nj = d // tn
    k = conv_w.shape[0]
    hl = l // 2
    cb = conv_b.reshape(1, 3 * d)
    sec = lambda s: pl.BlockSpec((None, l, tn), lambda j, bb: (bb, 0, s * nj + j))
    cws = lambda s: pl.BlockSpec((k, tn), lambda j, bb: (0, s * nj + j))
    cbs = lambda s: pl.BlockSpec((1, tn), lambda j, bb: (0, s * nj + j))
    tab = lambda o, r: _resident((r, tn), lambda j, bb: (0, o * nj + j))
    half_bf16 = pltpu.VMEM((hl, tn), BF16)
    natural = pltpu.VMEM((tn // LANES, l, LANES), F32)
    return pl.pallas_call(
        _hyena_body,
        grid=(nj, b),
        in_specs=[sec(0), sec(1), sec(2), cws(0), cws(1), cws(2), cbs(0), cbs(1), cbs(2),
                  pl.BlockSpec((HYENA_ORDER, tn), lambda j, bb: (0, j)),
                  tab(0, l), tab(0, l), tab(0, 2), tab(1, l), tab(1, l), tab(1, 2),
                  _resident(fs.shape, lambda j, bb: (0, 0))],
        out_specs=pl.BlockSpec((None, l, tn), lambda j, bb: (bb, 0, j)),
        out_shape=jax.ShapeDtypeStruct((b, l, d), BF16),
        scratch_shapes=[pltpu.VMEM((l + 2 * HALO, tn), F32),
                        natural,
                        pltpu.VMEM((l, tn), F32), pltpu.VMEM((l, tn), F32),
                        half_bf16, half_bf16,
                        pltpu.VMEM((l, tn), F32), pltpu.VMEM((l, tn), F32),
                        half_bf16, half_bf16, half_bf16, half_bf16,
                        natural],
        compiler_params=_cparams(2),
        name="hyena",
    )(proj, proj, proj, conv_w, conv_w, conv_w, cb, cb, cb, bias, t1, t2, mid, t1, t2, mid, fs)


def _lru_gates_body(rec_ref, cw_ref, cb_ref, wa_ref, ba_ref, wx_ref, bx_ref, lam_ref,
                    af_ref, bf_ref, ab_ref, bb_ref):
    xc = _dwconv(rec_ref[...].astype(F32), cw_ref, 2) + cb_ref[...]
    xcb = xc.astype(BF16)
    xh = 0.5 * xc
    for dr, (a_out, b_out) in enumerate(((af_ref, bf_ref), (ab_ref, bb_ref))):
        tr = jnp.tanh(jnp.dot(xcb, wa_ref[dr], preferred_element_type=F32) + 0.5 * ba_ref[dr:dr + 1, :])
        ti = jnp.tanh(jnp.dot(xcb, wx_ref[dr], preferred_element_type=F32) + 0.5 * bx_ref[dr:dr + 1, :])
        ch = (-0.5 * LRU_C * math.log2(math.e)) * jnp.log1p(jnp.exp(-lam_ref[dr:dr + 1, :]))
        a = jnp.exp2(tr * ch + ch)
        a_out[...] = a
        b_out[...] = jnp.sqrt(1.0 - a * a) * (ti * xh + xh)


def _block_diag_tiles(w, tn):
    two, h, bs, _ = w.shape
    hpt = tn // bs
    wt = w.reshape(two, h // hpt, hpt, bs, bs)
    eye = jnp.eye(hpt, dtype=w.dtype)
    dense = jnp.einsum('tnhij,hg->tnhigj', wt, eye)
    return dense.reshape(two, h // hpt, tn, tn)


def lru_gates(proj, rec_sec, conv_w, conv_b, wa, ba, wx, bx, lam):
    b, l, n = proj.shape
    d = conv_w.shape[1]
    tn = _tile(d, 256)
    nj = d // tn
    k = conv_w.shape[0]
    wa_t = (0.5 * _block_diag_tiles(wa, tn)).astype(BF16)
    wx_t = (0.5 * _block_diag_tiles(wx, tn)).astype(BF16)
    vec = lambda r: pl.BlockSpec((r, tn), lambda bb, j: (0, j))
    wsp = pl.BlockSpec((2, None, tn, tn), lambda bb, j: (0, j, 0, 0))
    out = pl.BlockSpec((None, l, tn), lambda bb, j: (bb, 0, j))
    shp = jax.ShapeDtypeStruct((b, l, d), F32)
    return pl.pallas_call(
        _lru_gates_body,
        grid=(b, nj),
        in_specs=[pl.BlockSpec((None, l, tn), lambda bb, j: (bb, 0, rec_sec * nj + j)),
                  vec(k), vec(1), wsp, vec(2), wsp, vec(2), vec(2)],
        out_specs=[out, out, out, out],
        out_shape=[shp, shp, shp, shp],
        compiler_params=_cparams(2),
        name="lru_gates",
    )(proj, conv_w, conv_b.reshape(1, d), wa_t, ba, wx_t, bx, lam)


def _lru_scan_body(a_ref, b_ref, h0_ref, o_ref, carry_ref, *, reverse):
    tl = a_ref.shape[0]
    groups = tl // 8

    @pl.when(pl.program_id(1) == 0)
    def _():
        carry_ref[0:1, :] = h0_ref[...]

    def step(g, h):
        base = pl.multiple_of((groups - 1 - g if reverse else g) * 8, 8)
        for s in (range(7, -1, -1) if reverse else range(8)):
            h = a_ref[pl.ds(base + s, 1), :] * h + b_ref[pl.ds(base + s, 1), :]
            o_ref[pl.ds(base + s, 1), :] = h
        return h

    carry_ref[0:1, :] = lax.fori_loop(0, groups, step, carry_ref[0:1, :])


def lru_scan(a, b, h0, reverse):
    bsz, l, d = a.shape
    tl = _tile(l, 256)
    nt = l // tl
    tmap = (lambda bb, i: (bb, nt - 1 - i, 0)) if reverse else (lambda bb, i: (bb, i, 0))
    return pl.pallas_call(
        functools.partial(_lru_scan_body, reverse=reverse),
        grid=(bsz, nt),
        in_specs=[pl.BlockSpec((None, tl, d), tmap),
                  pl.BlockSpec((None, tl, d), tmap),
                  pl.BlockSpec((None, 1, d), lambda bb, i: (bb, 0, 0))],
        out_specs=pl.BlockSpec((None, tl, d), tmap),
        out_shape=jax.ShapeDtypeStruct((bsz, l, d), F32),
        scratch_shapes=[pltpu.VMEM((8, d), F32)],
        compiler_params=_cparams(2),
        name="lru_scan",
    )(a, b, h0.reshape(bsz, 1, d))


def _shortconv_body(bg_ref, cg_ref, xv_ref, w_ref, o_ref):
    prod = cg_ref[...].astype(F32) * xv_ref[...].astype(F32)
    o_ref[...] = (bg_ref[...].astype(F32) * _dwconv(prod, w_ref, 1)).astype(o_ref.dtype)


def shortconv_mixer(proj, first_sec, conv_w):
    b, l, _ = proj.shape
    k, d = conv_w.shape
    tn = _tile(d, 256)
    nj = d // tn
    sec = lambda s: pl.BlockSpec((None, l, tn), lambda bb, j: (bb, 0, (first_sec + s) * nj + j))
    return pl.pallas_call(
        _shortconv_body,
        grid=(b, nj),
        in_specs=[sec(0), sec(1), sec(2), pl.BlockSpec((k, tn), lambda bb, j: (0, j))],
        out_specs=pl.BlockSpec((None, l, tn), lambda bb, j: (bb, 0, j)),
        out_shape=jax.ShapeDtypeStruct((b, l, d), BF16),
        compiler_params=_cparams(2),
        name="shortconv",
    )(proj, proj, proj, conv_w)


def _gelu_tanh(x):
    return 0.5 * x * (1.0 + jnp.tanh(math.sqrt(2.0 / math.pi) * (x + 0.044715 * (x * x * x))))


def _merge_body(ya_ref, yc_ref, lg_ref, hf_ref, hb_ref, ga_ref, gb_ref, gc_ref, x_ref,
                mg_ref, g2_ref, sh_ref, sc_ref, wa_ref, wb_ref, wc_ref, wo_ref, wr_ref,
                xo_ref, h2_ref, aff_ref):
    yb = (_gelu_tanh(lg_ref[...].astype(F32)) * (hf_ref[...] + hb_ref[...])).astype(BF16)
    merged = _sigmoid(ga_ref[...].astype(F32)) * jnp.dot(ya_ref[...], wa_ref[...], preferred_element_type=F32)
    merged += _sigmoid(gb_ref[...].astype(F32)) * jnp.dot(yb, wb_ref[...], preferred_element_type=F32)
    merged += _sigmoid(gc_ref[...].astype(F32)) * jnp.dot(yc_ref[...], wc_ref[...], preferred_element_type=F32)
    out = jnp.dot(merged.astype(BF16), wo_ref[...], preferred_element_type=F32)
    x = x_ref[...] + mg_ref[...] * out
    xo_ref[...] = x
    h2 = _rms_modulate(x, g2_ref[...], sh_ref[...], sc_ref[...])
    h2_hi = h2.astype(BF16)
    h2_ref[...] = h2_hi
    h2_lo = (h2 - h2_hi.astype(F32)).astype(BF16)
    wr = wr_ref[...]
    wr_hi = wr.astype(BF16)
    wr_lo = (wr - wr_hi.astype(F32)).astype(BF16)
    logits = (jnp.dot(h2_hi, wr_hi, preferred_element_type=F32)
              + jnp.dot(h2_hi, wr_lo, preferred_element_type=F32)
              + jnp.dot(h2_lo, wr_hi, preferred_element_type=F32))
    e = jnp.exp(logits - jnp.max(logits, axis=-1, keepdims=True))
    aff_ref[...] = e / jnp.sum(e, axis=-1, keepdims=True)


def merge_mixers(proj, gate_sec, lg_sec, ya, yc, hf, hb, x, mod_gate, g2, shift2, scale2,
                 w_a, w_b, w_c, w_o, w_r):
    b, l, d = x.shape
    e = w_r.shape[1]
    tl = _tile(l, 512)
    row = lambda: pl.BlockSpec((None, tl, d), lambda bb, i: (bb, i, 0))
    sec = lambda s: pl.BlockSpec((None, tl, d), lambda bb, i: (bb, i, s))
    modv = lambda: pl.BlockSpec((None, 1, d), lambda bb, i: (bb, 0, 0))
    wsp = lambda: _resident((d, d), lambda bb, i: (0, 0))
    return pl.pallas_call(
        _merge_body,
        grid=(b, l // tl),
        in_specs=[row(), row(), sec(lg_sec), row(), row(),
                  sec(gate_sec), sec(gate_sec + 1), sec(gate_sec + 2), row(),
                  modv(), pl.BlockSpec((1, d), lambda bb, i: (0, 0)), modv(), modv(),
                  wsp(), wsp(), wsp(), wsp(), _resident((d, e), lambda bb, i: (0, 0))],
        out_specs=[row(), row(), pl.BlockSpec((None, tl, e), lambda bb, i: (bb, i, 0))],
        out_shape=[jax.ShapeDtypeStruct((b, l, d), F32),
                   jax.ShapeDtypeStruct((b, l, d), BF16),
                   jax.ShapeDtypeStruct((b, l, e), F32)],
        compiler_params=_cparams(2),
        name="merge",
    )(ya, yc, proj, hf, hb, proj, proj, proj, x, mod_gate, g2.reshape(1, d), shift2, scale2,
      w_a, w_b, w_c, w_o, w_r)


def _select_body(aff_ref, tri_ref, rank_ref, *, cap):
    bits = lax.bitcast_convert_type(aff_ref[...], jnp.int32)

    def refine(i, thr):
        cand = thr | jnp.left_shift(jnp.int32(1), 30 - i)
        cnt = jnp.sum((bits >= cand).astype(F32), axis=1, keepdims=True)
        return jnp.where(cnt >= cap, cand, thr)

    thr = lax.fori_loop(0, 31, refine, jnp.zeros((bits.shape[0], 1), jnp.int32))
    gt = bits > thr
    eq = bits == thr
    n_gt = jnp.sum(gt.astype(F32), axis=1, keepdims=True)
    eq_rank = jnp.dot(eq.astype(BF16), tri_ref[...], preferred_element_type=F32)
    sel = gt | (eq & (eq_rank <= cap - n_gt))
    rank = jnp.dot(sel.astype(BF16), tri_ref[...], preferred_element_type=F32) - 1.0
    rank_ref[...] = jnp.where(sel, rank, -1.0).astype(jnp.int32)


def select_tokens(aff_t, cap):
    r, n = aff_t.shape
    tri = jnp.asarray(_prefix_table(n), BF16)
    return pl.pallas_call(
        functools.partial(_select_body, cap=cap),
        grid=(1,),
        in_specs=[pl.BlockSpec((r, n), lambda i: (0, 0)), pl.BlockSpec((n, n), lambda i: (0, 0))],
        out_specs=pl.BlockSpec((r, n), lambda i: (0, 0)),
        out_shape=jax.ShapeDtypeStruct((r, n), jnp.int32),
        compiler_params=_cparams(1),
        name="select",
    )(aff_t, tri)


GATHER_GROUP = 8


def _gather_body(rank_ref, hs_ref, o_ref):
    grp, cap, d = o_ref.shape
    n = hs_ref.shape[0]
    g0 = pl.multiple_of(pl.program_id(1) * grp, grp)
    ranks = rank_ref[pl.ds(g0, grp), :]
    slot = lax.broadcasted_iota(jnp.int32, (cap, n), 0)
    onehot = jnp.concatenate([jnp.where(slot == ranks[e:e + 1, :], 1.0, 0.0).astype(BF16) for e in range(grp)],
                             axis=0)
    rows = jnp.dot(onehot, hs_ref[...], preferred_element_type=F32)
    o_ref[...] = rows.reshape(grp, cap, d).astype(o_ref.dtype)


def _gather_into_body(rank_ref, hs_ref, prev_ref, o_ref):
    del prev_ref
    _gather_body(rank_ref, hs_ref, o_ref)


def gather_tokens(rank_t, hs, cap, total_rows, row_off, into=None):
    s, e, n = rank_t.shape
    d = hs.shape[2]
    boff = row_off // cap
    in_specs = [pl.BlockSpec((None, e, n), lambda ss, ee: (ss, 0, 0)),
                pl.BlockSpec((None, n, d), lambda ss, ee: (ss, 0, 0))]
    args = [rank_t, hs]
    if into is not None:
        in_specs.append(pl.BlockSpec(memory_space=pl.ANY))
        args.append(into)
    return pl.pallas_call(
        _gather_body if into is None else _gather_into_body,
        grid=(s, e // GATHER_GROUP),
        in_specs=in_specs,
        out_specs=pl.BlockSpec((GATHER_GROUP, cap, d), lambda ss, ee: (ee, boff + ss, 0)),
        out_shape=jax.ShapeDtypeStruct((e, total_rows, d), BF16),
        input_output_aliases={} if into is None else {2: 0},
        compiler_params=_cparams(2),
        name="gather",
    )(*args)


def _ffn_body(x_ref, wg_ref, wu_ref, wd_ref, o_ref, acc_ref):
    f = pl.program_id(1)

    @pl.when(f == 0)
    def _():
        acc_ref[...] = jnp.zeros_like(acc_ref)

    x = x_ref[...]
    g = jnp.dot(x, wg_ref[...].astype(BF16), preferred_element_type=F32)
    u = jnp.dot(x, wu_ref[...].astype(BF16), preferred_element_type=F32)
    hid = (g * _sigmoid(g) * u).astype(BF16)
    acc_ref[...] += jnp.dot(hid, wd_ref[...].astype(BF16), preferred_element_type=F32)

    @pl.when(f == pl.num_programs(1) - 1)
    def _():
        o_ref[...] = acc_ref[...].astype(o_ref.dtype)


def expert_ffn(xs, wg, wu, wd, layer):
    e, m, d = xs.shape
    fdim = wg.shape[3]
    tf = _tile(fdim, 256)
    return pl.pallas_call(
        _ffn_body,
        grid=(e, fdim // tf),
        in_specs=[pl.BlockSpec((None, m, d), lambda ee, f: (ee, 0, 0)),
                  pl.BlockSpec((None, None, d, tf), lambda ee, f: (layer, ee, 0, f)),
                  pl.BlockSpec((None, None, d, tf), lambda ee, f: (layer, ee, 0, f)),
                  pl.BlockSpec((None, None, tf, d), lambda ee, f: (layer, ee, f, 0))],
        out_specs=pl.BlockSpec((None, m, d), lambda ee, f: (ee, 0, 0)),
        out_shape=jax.ShapeDtypeStruct((e, m, d), BF16),
        scratch_shapes=[pltpu.VMEM((m, d), F32)],
        compiler_params=_cparams(2),
        name="ffn",
    )(xs, wg, wu, wd)


def _scatter_body(rank_ref, aff_ref, ys_ref, x_ref, mg_ref, gf_ref, o_ref, *, final_norm):
    tq = rank_ref.shape[0]
    n_exp, cap, _ = ys_ref.shape
    slot = lax.broadcasted_iota(jnp.int32, (tq, cap), 1)
    rank = rank_ref[...]
    aff = aff_ref[...]
    acc = None
    for e in range(n_exp):
        q = jnp.where(slot == rank[:, e:e + 1], aff[:, e:e + 1], 0.0).astype(BF16)
        part = jnp.dot(q, ys_ref[e], preferred_element_type=F32)
        acc = part if acc is None else acc + part
    x = x_ref[...] + mg_ref[...] * acc
    if final_norm:
        x = x * lax.rsqrt(jnp.mean(x * x, axis=-1, keepdims=True) + NORM_EPS) * gf_ref[...]
    o_ref[...] = x


def scatter_residual(rank, aff, ys, row_off, cap, x, mod_gate, final_g):
    s, n, e = rank.shape
    d = x.shape[2]
    tq = _tile(n, 1024)
    boff = row_off // cap
    final_norm = final_g is not None
    gf = (final_g if final_norm else jnp.ones((d,), F32)).reshape(1, d)
    return pl.pallas_call(
        functools.partial(_scatter_body, final_norm=final_norm),
        grid=(s, n // tq),
        in_specs=[pl.BlockSpec((None, tq, e), lambda ss, i: (ss, i, 0)),
                  pl.BlockSpec((None, tq, e), lambda ss, i: (ss, i, 0)),
                  pl.BlockSpec((e, cap, d), lambda ss, i: (0, boff + ss, 0)),
                  pl.BlockSpec((None, tq, d), lambda ss, i: (ss, i, 0)),
                  pl.BlockSpec((None, 1, d), lambda ss, i: (ss, 0, 0)),
                  pl.BlockSpec((1, d), lambda ss, i: (0, 0))],
        out_specs=pl.BlockSpec((None, tq, d), lambda ss, i: (ss, i, 0)),
        out_shape=jax.ShapeDtypeStruct(x.shape, F32),
        compiler_params=_cparams(2),
        name="scatter",
    )(rank, aff, ys, x, mod_gate, gf)


def _mixer_and_route(x, h0_f, h0_b, mods, lw, fs, shared_mod=False):
    b, l, d = x.shape
    if shared_mod:
        flat = lambda a: a.reshape(1, b * l, a.shape[-1])
        unflat = lambda a: a.reshape(b, l, a.shape[-1])
        mods = [m[:1] for m in mods]
    else:
        flat = unflat = lambda a: a
    proj = unflat(norm_proj(flat(x), lw['norm1_g'], mods[0], mods[1], lw['w_in']))
    t1, t2, mid = hy_filter_tables(l, lw['hy_filt_w1'], lw['hy_filt_b1'], lw['hy_filt_w2'], lw['hy_filt_b2'],
                                  lw['hy_filt_w3'], lw['hy_filt_freq'], fs)
    ya = hyena_mixer(proj, lw['hy_conv_w'], lw['hy_conv_b'], lw['hy_bias'], t1, t2, mid, fs)
    a_f, b_f, a_b, b_b = lru_gates(proj, 4, lw['lru_conv_w'], lw['lru_conv_b'], lw['lru_wa'], lw['lru_ba'],
                                   lw['lru_wx'], lw['lru_bx'], lw['lru_lambda'])
    hf = lru_scan(a_f, b_f, h0_f, False)
    hb = lru_scan(a_b, b_b, h0_b, True)
    yc = shortconv_mixer(proj, 5, lw['sc_conv_w'])
    x_new, h2, aff = merge_mixers(flat(proj), 8, 3, flat(ya), flat(yc), flat(hf), flat(hb), flat(x), mods[2],
                                  lw['norm2_g'], mods[3], mods[4],
                                  lw['w_hy_out'], lw['w_lru_out'], lw['w_sc_out'], lw['w_o'], lw['w_router'])
    return unflat(x_new), unflat(h2), unflat(aff), hf[:, -1], hb[:, 0]


def _capacity(n, e):
    return EC_CAPACITY * n // e


def _route(h2, aff, total_rows, row_off, into):
    s, n, e = aff.shape
    cap = _capacity(n, e)
    aff_t = jnp.transpose(aff, (0, 2, 1))
    rank_t = select_tokens(aff_t.reshape(s * e, n), cap).reshape(s, e, n)
    xs = gather_tokens(rank_t, h2, cap, total_rows, row_off, into)
    return xs, jnp.transpose(rank_t, (0, 2, 1))


def kernel(x, c, ctx, c_ctx, w_ada, b_ada, norm1_g, norm2_g, w_in, hy_conv_w, hy_conv_b, hy_filt_w1, hy_filt_b1, hy_filt_w2, hy_filt_b2, hy_filt_w3, hy_filt_freq, hy_bias, lru_conv_w, lru_conv_b, lru_wa, lru_ba, lru_wx, lru_bx, lru_lambda, sc_conv_w, w_hy_out, w_lru_out, w_sc_out, w_o, w_router, w_exp_gate, w_exp_up, w_exp_down, final_norm_g):
    bsz, n_lat, d = x.shape
    n_ctx = ctx.shape[1]
    depth = w_ada.shape[0]
    rows = n_lat // GRID_W

    x = add_pos(x, jnp.asarray(_pos_embed(rows, d)))
    xc = ctx
    fs_lat = jnp.asarray(_dft_table(n_lat), BF16)
    fs_ctx = jnp.asarray(_dft_table(n_ctx), BF16)

    cc = jnp.zeros((2 * bsz, d), F32).at[:bsz].set(c).at[bsz].set(c_ctx)
    mods = ada_mods(cc, w_ada, b_ada)
    zeros_state = jnp.zeros((bsz, d), F32)

    for l in range(depth):
        last = l == depth - 1
        lw = {
            'norm1_g': norm1_g[l], 'norm2_g': norm2_g[l], 'w_in': w_in[l].astype(BF16),
            'hy_conv_w': hy_conv_w[l], 'hy_conv_b': hy_conv_b[l],
            'hy_filt_w1': hy_filt_w1[l], 'hy_filt_b1': hy_filt_b1[l], 'hy_filt_w2': hy_filt_w2[l],
            'hy_filt_b2': hy_filt_b2[l], 'hy_filt_w3': hy_filt_w3[l], 'hy_filt_freq': hy_filt_freq[l],
            'hy_bias': hy_bias[l], 'lru_conv_w': lru_conv_w[l], 'lru_conv_b': lru_conv_b[l],
            'lru_wa': lru_wa[l], 'lru_ba': lru_ba[l], 'lru_wx': lru_wx[l], 'lru_bx': lru_bx[l],
            'lru_lambda': lru_lambda[l], 'sc_conv_w': sc_conv_w[l],
            'w_hy_out': w_hy_out[l].astype(BF16), 'w_lru_out': w_lru_out[l].astype(BF16),
            'w_sc_out': w_sc_out[l].astype(BF16), 'w_o': w_o[l].astype(BF16), 'w_router': w_router[l],
        }
        mod_l = [mods[l, :bsz, k * d:(k + 1) * d].reshape(bsz, 1, d) for k in range(6)]
        mod_c = [jnp.broadcast_to(mods[l, bsz, k * d:(k + 1) * d].reshape(1, 1, d), (bsz, 1, d)) for k in range(6)]

        n_exp = w_router.shape[2]
        cap_l = _capacity(n_lat, n_exp)
        cap_c = _capacity(n_ctx, n_exp)
        rows_l = bsz * cap_l
        if last:
            proj_c = norm_proj(xc.reshape(1, bsz * n_ctx, d), lw['norm1_g'], mod_c[0][:1], mod_c[1][:1],
                               lw['w_in'][:, 4 * d:5 * d]).reshape(bsz, n_ctx, d)
            a_f, b_f, a_b, b_b = lru_gates(proj_c, 0, lw['lru_conv_w'], lw['lru_conv_b'], lw['lru_wa'], lw['lru_ba'],
                                           lw['lru_wx'], lw['lru_bx'], lw['lru_lambda'])
            state_f = lru_scan(a_f, b_f, zeros_state, False)[:, -1]
            state_b = lru_scan(a_b, b_b, zeros_state, True)[:, 0]
            total_rows = rows_l
            xs = None
        else:
            xc_mid, h2_c, aff_c, state_f, state_b = _mixer_and_route(xc, zeros_state, zeros_state, mod_c, lw, fs_ctx,
                                                                     shared_mod=True)
            total_rows = rows_l + bsz * cap_c
            xs, rank_c = _route(h2_c, aff_c, total_rows, rows_l, None)

        x_mid, h2_l, aff_l, _, _ = _mixer_and_route(x, state_f, state_b, mod_l, lw, fs_lat)
        xs, rank_l = _route(h2_l, aff_l, total_rows, 0, xs)
        ys = expert_ffn(xs, w_exp_gate, w_exp_up, w_exp_down, l)

        if not last:
            xc = scatter_residual(rank_c, aff_c, ys, rows_l, cap_c, xc_mid, mod_c[5], None)
        x = scatter_residual(rank_l, aff_l, ys, 0, cap_l, x_mid, mod_l[5], final_norm_g if last else None)
    return x
```

```python
import functools
import math

import numpy as np
import jax
import jax.numpy as jnp
from jax import lax
from jax.experimental import pallas as pl
from jax.experimental.pallas import tpu as pltpu

F32 = jnp.float32
BF16 = jnp.bfloat16
HI = lax.Precision.HIGHEST

NORM_EPS = 1e-6
GRID_W = 64
N_EXPERTS = 16
EC_CAPACITY = 2
LRU_C = 8.0
HYENA_ORDER = 2
HYENA_EMB_DIM = 33
HYENA_FAST_DECAY = 0.3
HYENA_SLOW_DECAY = 1.5
HYENA_TARGET = 1e-2
MLP_PAD = 128

V7X_VMEM_BYTES = 64 * 1024 * 1024
VMEM_LIMIT = 56 * 1024 * 1024
assert VMEM_LIMIT < V7X_VMEM_BYTES


def _cparams(n_grid):
    return pltpu.CompilerParams(dimension_semantics=("arbitrary",) * n_grid,
                                vmem_limit_bytes=VMEM_LIMIT)


def _tile(n, pref):
    if n <= pref:
        return n
    t = pref
    while n % t:
        t //= 2
    return t


def _resident(block_shape, index_map):
    return pl.BlockSpec(block_shape, index_map, pipeline_mode=pl.Buffered(1))


@functools.lru_cache(maxsize=None)
def _pos_embed(rows, d):
    def sincos(pos, dim):
        half = dim // 2
        omega = 1.0 / (10000.0 ** (np.arange(half, dtype=np.float64) / half))
        ang = pos[:, None] * omega[None, :]
        return np.concatenate([np.sin(ang), np.cos(ang)], axis=-1)
    half = d // 2
    er = sincos(np.arange(rows, dtype=np.float64), half)
    ec = sincos(np.arange(GRID_W, dtype=np.float64), half)
    emb = np.concatenate([np.broadcast_to(er[:, None, :], (rows, GRID_W, half)),
                          np.broadcast_to(ec[None, :, :], (rows, GRID_W, half))], axis=-1)
    return emb.reshape(rows * GRID_W, d).astype(np.float32)


@functools.lru_cache(maxsize=None)
def _dft_table(L):
    h = L // 2
    k = np.arange(h, dtype=np.int64)
    ang_e = np.pi * ((k[:, None] * (2 * k[None, :])) % (2 * L)).astype(np.float64) / L
    ang_o = np.pi * ((k[:, None] * (2 * k[None, :] + 1)) % (2 * L)).astype(np.float64) / L
    ce, se, co, so = np.cos(ang_e), np.sin(ang_e), np.cos(ang_o), np.sin(ang_o)
    return np.concatenate([ce, se, co, so, co.T, so.T], axis=0).astype(np.float32)


@functools.lru_cache(maxsize=None)
def _filter_feats(L):
    bands = (HYENA_EMB_DIM - 1) // 2
    t01 = np.linspace(0.0, 1.0, L, dtype=np.float32).astype(np.float64)[:, None]
    w = ((2.0 * math.pi / L) * np.arange(L, dtype=np.float32))[:, None].astype(np.float64)
    f = np.linspace(1e-4, bands - 1, bands, dtype=np.float32).astype(np.float64)[None, :]
    feats = np.concatenate([t01, np.cos(f * w), -np.sin(f * w)], axis=-1)
    out = np.zeros((L, MLP_PAD), np.float32)
    out[:, :HYENA_EMB_DIM] = feats
    return out


@functools.lru_cache(maxsize=None)
def _decay_window(L, d):
    max_decay = math.log(HYENA_TARGET) / HYENA_FAST_DECAY
    min_decay = math.log(HYENA_TARGET) / HYENA_SLOW_DECAY
    t01 = np.linspace(0.0, 1.0, L, dtype=np.float32).astype(np.float64)[:, None]
    deltas = np.linspace(min_decay, max_decay, d, dtype=np.float32).astype(np.float64)
    return np.exp(-t01 * np.abs(deltas)[None, :]).astype(np.float32)


@functools.lru_cache(maxsize=None)
def _prefix_table(n):
    return np.triu(np.ones((n, n), np.float32))


def _shift_rows(x, s):
    if s == 0:
        return x
    n = x.shape[0]
    rolled = pltpu.roll(x, s % n, axis=0)
    row = lax.broadcasted_iota(jnp.int32, x.shape, 0)
    keep = row >= s if s > 0 else row < n + s
    return jnp.where(keep, rolled, 0.0)


def _dwconv(x, w_ref, left):
    acc = None
    for k in range(w_ref.shape[0]):
        term = _shift_rows(x, left - k) * w_ref[k:k + 1, :]
        acc = term if acc is None else acc + term
    return acc


def _sigmoid(x):
    return 0.5 * jnp.tanh(0.5 * x) + 0.5


def _rms_modulate(x, g, shift, scale):
    y = x * lax.rsqrt(jnp.mean(x * x, axis=-1, keepdims=True) + NORM_EPS)
    return (y * g) * (1.0 + scale) + shift


def _ada_body(c_ref, w_ref, b_ref, o_ref):
    c = c_ref[...]
    s = c * _sigmoid(c)
    o_ref[...] = jnp.dot(s, w_ref[...], precision=HI, preferred_element_type=F32) + b_ref[...]


def ada_mods(cc, w_ada, b_ada):
    depth, d, n = w_ada.shape
    r = cc.shape[0]
    tn = _tile(n, 1536)
    return pl.pallas_call(
        _ada_body,
        grid=(depth, n // tn),
        in_specs=[pl.BlockSpec((r, d), lambda l, j: (0, 0)),
                  pl.BlockSpec((None, d, tn), lambda l, j: (l, 0, j)),
                  pl.BlockSpec((None, 1, tn), lambda l, j: (l, 0, j))],
        out_specs=pl.BlockSpec((None, r, tn), lambda l, j: (l, 0, j)),
        out_shape=jax.ShapeDtypeStruct((depth, r, n), F32),
        compiler_params=_cparams(2),
        name="ada",
    )(cc, w_ada, b_ada.reshape(depth, 1, n))


def _norm_proj_body(x_ref, g_ref, sh_ref, sc_ref, w_ref, o_ref, h_ref):
    @pl.when(pl.program_id(2) == 0)
    def _():
        h_ref[...] = _rms_modulate(x_ref[...], g_ref[...], sh_ref[...], sc_ref[...]).astype(BF16)

    o_ref[...] = jnp.dot(h_ref[...], w_ref[...], preferred_element_type=F32).astype(o_ref.dtype)


def _embed_norm_proj_body(x_ref, pe_ref, g_ref, sh_ref, sc_ref, w_ref, o_ref, xo_ref, h_ref):
    @pl.when(pl.program_id(2) == 0)
    def _():
        x = x_ref[...] + pe_ref[...]
        xo_ref[...] = x
        h_ref[...] = _rms_modulate(x, g_ref[...], sh_ref[...], sc_ref[...]).astype(BF16)

    o_ref[...] = jnp.dot(h_ref[...], w_ref[...], preferred_element_type=F32).astype(o_ref.dtype)


PROJ_COLS = 2816


def norm_proj(x, g, shift, scale, w, pos_embed=None):
    b, l, d = x.shape
    n = w.shape[1]
    tl = _tile(l, 1024)
    tn = _tile(n, PROJ_COLS)
    row = pl.BlockSpec((None, tl, d), lambda bb, i, j: (bb, i, 0))
    tail = [pl.BlockSpec((1, d), lambda bb, i, j: (0, 0)),
            pl.BlockSpec((None, 1, d), lambda bb, i, j: (bb, 0, 0)),
            pl.BlockSpec((None, 1, d), lambda bb, i, j: (bb, 0, 0)),
            pl.BlockSpec((d, tn), lambda bb, i, j: (0, j))]
    proj_spec = pl.BlockSpec((None, tl, tn), lambda bb, i, j: (bb, i, j))
    proj_shape = jax.ShapeDtypeStruct((b, l, n), BF16)
    common = dict(grid=(b, l // tl, n // tn), scratch_shapes=[pltpu.VMEM((tl, d), BF16)],
                  compiler_params=_cparams(3), name="norm_proj")
    if pos_embed is None:
        return pl.pallas_call(_norm_proj_body, in_specs=[row] + tail, out_specs=proj_spec, out_shape=proj_shape,
                              **common)(x, g.reshape(1, d), shift, scale, w)
    return pl.pallas_call(
        _embed_norm_proj_body,
        in_specs=[row, pl.BlockSpec((tl, d), lambda bb, i, j: (i, 0))] + tail,
        out_specs=[proj_spec, row],
        out_shape=[proj_shape, jax.ShapeDtypeStruct(x.shape, F32)],
        **common)(x, pos_embed, g.reshape(1, d), shift, scale, w)


def _hy_hidden_body(feats_ref, w1_ref, b1_ref, w2_ref, b2_ref, fr_ref, h_ref):
    fr = fr_ref[...]
    h = jnp.sin(fr * (jnp.dot(feats_ref[...], w1_ref[...], precision=HI,
                              preferred_element_type=F32) + b1_ref[...]))
    h_ref[...] = jnp.sin(fr * (jnp.dot(h, w2_ref[...], precision=HI,
                                       preferred_element_type=F32) + b2_ref[...]))


LANES = 128


def _sign_col(rows):
    row = lax.broadcasted_iota(jnp.int32, (rows, 1), 0)
    return (1 - 2 * (row & 1)).astype(F32)


def _put_slabs(nat_ref, r0, rows):
    for s in range(nat_ref.shape[0]):
        nat_ref[s, pl.ds(r0, rows.shape[0]), :] = rows[:, s * LANES:(s + 1) * LANES]


def _get_slabs(nat_ref, r0, n):
    return jnp.concatenate([nat_ref[s, pl.ds(r0, n), :] for s in range(nat_ref.shape[0])], axis=1)


def _split_rows(nat_ref, e_ref, o_ref):
    nslab, L, _ = nat_ref.shape
    h = L // 2
    rd = min(h, 256)
    sgn = _sign_col(rd)

    def body(i, carry):
        m0 = pl.multiple_of(i * rd, rd)
        e = jnp.concatenate([nat_ref[s, pl.ds(2 * m0, rd, stride=2), :] for s in range(nslab)], axis=1)
        o = jnp.concatenate([nat_ref[s, pl.ds(2 * m0 + 1, rd, stride=2), :] for s in range(nslab)], axis=1)
        e_ref[pl.ds(m0, rd), :] = e.astype(BF16)
        o_ref[pl.ds(m0, rd), :] = o.astype(BF16)
        return (carry[0] + jnp.sum(e * sgn, axis=0, keepdims=True),
                carry[1] + jnp.sum(o * sgn, axis=0, keepdims=True))

    zero = jnp.zeros((1, nslab * LANES), F32)
    return lax.fori_loop(0, h // rd, body, (zero, zero))


def _hy_filter_body(h_ref, w3f_ref, w3b_ref, decay_ref, tb_ref, t1_ref, t2_ref, mid_ref,
                    p_ref, q_ref, pe_ref, po_ref, qe_ref, qo_ref):
    L = h_ref.shape[0]
    hl = L // 2
    hid = h_ref[...]
    decay = decay_ref[...]
    fwd = jnp.dot(hid, w3f_ref[...], precision=HI, preferred_element_type=F32) * decay
    bwd = jnp.dot(hid, w3b_ref[...], precision=HI, preferred_element_type=F32) * decay
    row = lax.broadcasted_iota(jnp.int32, fwd.shape, 0)
    bwd = jnp.where(row >= 1, bwd, 0.0)
    inv = 1.0 / (jnp.sum(jnp.abs(fwd), axis=0, keepdims=True)
                 + jnp.sum(jnp.abs(bwd), axis=0, keepdims=True))
    _put_slabs(p_ref, 0, fwd + bwd)
    _put_slabs(q_ref, 0, bwd - fwd)
    k_re_mid, _ = _split_rows(p_ref, pe_ref, po_ref)
    _, k_im_mid = _split_rows(q_ref, qe_ref, qo_ref)
    a_e = jnp.dot(tb_ref[0:hl, :], pe_ref[...], preferred_element_type=F32)
    a_o = jnp.dot(tb_ref[2 * hl:3 * hl, :], po_ref[...], preferred_element_type=F32)
    b_e = jnp.dot(tb_ref[hl:2 * hl, :], qe_ref[...], preferred_element_type=F32)
    b_o = jnp.dot(tb_ref[3 * hl:4 * hl, :], qo_ref[...], preferred_element_type=F32)
    n_fft = 2.0 * L
    row_h = lax.broadcasted_iota(jnp.int32, (hl, fwd.shape[1]), 0)
    w_k = jnp.where(row_h >= 1, 2.0 / n_fft, 1.0 / n_fft) * inv
    w_s = (2.0 / n_fft) * inv
    t1_ref[0:hl, :] = (a_e + a_o) * w_k
    t1_ref[hl:L, :] = (a_e - a_o) * w_k
    t2_ref[0:hl, :] = (b_e + b_o) * w_s
    t2_ref[hl:L, :] = (b_o - b_e) * w_s
    mid_ref[0:1, :] = k_re_mid * w_s
    mid_ref[1:2, :] = k_im_mid * w_s


def hy_filter_tables(L, w1, b1, w2, b2, w3, freq, fs):
    d = w3.shape[1] // (HYENA_ORDER * 2)
    fw = w1.shape[1]
    pad = lambda a, r, c: jnp.zeros((r, c), F32).at[:a.shape[0], :a.shape[1]].set(a)
    w1p = pad(w1, MLP_PAD, MLP_PAD)
    w2p = pad(w2, MLP_PAD, MLP_PAD)
    w3p = pad(w3, MLP_PAD, w3.shape[1])
    b1p = pad(b1.reshape(1, fw), 1, MLP_PAD)
    b2p = pad(b2.reshape(1, fw), 1, MLP_PAD)
    frp = pad(freq.reshape(1, fw), 1, MLP_PAD)
    feats = jnp.asarray(_filter_feats(L))
    decay = jnp.asarray(_decay_window(L, d))
    tn = _tile(d, 256)
    nj = d // tn
    full = lambda o, j: (0, 0)
    one = lambda i: (0, 0)
    hidden = pl.pallas_call(
        _hy_hidden_body,
        grid=(1,),
        in_specs=[pl.BlockSpec((L, MLP_PAD), one),
                  pl.BlockSpec((MLP_PAD, MLP_PAD), one),
                  pl.BlockSpec((1, MLP_PAD), one),
                  pl.BlockSpec((MLP_PAD, MLP_PAD), one),
                  pl.BlockSpec((1, MLP_PAD), one),
                  pl.BlockSpec((1, MLP_PAD), one)],
        out_specs=pl.BlockSpec((L, MLP_PAD), one),
        out_shape=jax.ShapeDtypeStruct((L, MLP_PAD), F32),
        compiler_params=_cparams(1),
        name="hy_hidden",
    )(feats, w1p, b1p, w2p, b2p, frp)
    hl = L // 2
    t1, t2, mid = pl.pallas_call(
        _hy_filter_body,
        grid=(HYENA_ORDER, nj),
        in_specs=[pl.BlockSpec((L, MLP_PAD), full),
                  pl.BlockSpec((MLP_PAD, tn), lambda o, j: (0, o * 2 * nj + j)),
                  pl.BlockSpec((MLP_PAD, tn), lambda o, j: (0, o * 2 * nj + nj + j)),
                  pl.BlockSpec((L, tn), lambda o, j: (0, j)),
                  _resident(fs.shape, full)],
        out_specs=[pl.BlockSpec((L, tn), lambda o, j: (0, o * nj + j)),
                   pl.BlockSpec((L, tn), lambda o, j: (0, o * nj + j)),
                   pl.BlockSpec((2, tn), lambda o, j: (0, o * nj + j))],
        out_shape=[jax.ShapeDtypeStruct((L, HYENA_ORDER * d), F32),
                   jax.ShapeDtypeStruct((L, HYENA_ORDER * d), F32),
                   jax.ShapeDtypeStruct((2, HYENA_ORDER * d), F32)],
        scratch_shapes=[pltpu.VMEM((tn // LANES, L, LANES), F32), pltpu.VMEM((tn // LANES, L, LANES), F32),
                        pltpu.VMEM((hl, tn), BF16), pltpu.VMEM((hl, tn), BF16),
                        pltpu.VMEM((hl, tn), BF16), pltpu.VMEM((hl, tn), BF16)],
        compiler_params=_cparams(2),
        name="hy_filter",
    )(hidden, w3p, w3p, decay, fs)
    return t1, t2, mid


HALO = 8


def _chunks(n_rows, chunk, fn):
    def body(c, carry):
        fn(pl.multiple_of(c * chunk, chunk))
        return carry
    lax.fori_loop(0, n_rows // chunk, body, 0)


def _conv_rows(src_ref, w_ref, b_ref, left, pad_ref, emit):
    L = src_ref.shape[0]
    rc = min(L, 256)
    win = rc + 2 * HALO
    zeros = jnp.zeros((HALO, pad_ref.shape[1]), F32)
    pad_ref[0:HALO, :] = zeros
    pad_ref[L + HALO:L + 2 * HALO, :] = zeros

    def fill(r0):
        pad_ref[pl.ds(r0 + HALO, rc), :] = src_ref[pl.ds(r0, rc), :].astype(F32)
    _chunks(L, rc, fill)

    def conv(r0):
        w = pad_ref[pl.ds(r0, win), :]
        acc = None
        for k in range(w_ref.shape[0]):
            s = left - k
            term = (w if s == 0 else pltpu.roll(w, s % win, axis=0)) * w_ref[k:k + 1, :]
            acc = term if acc is None else acc + term
        out = acc[HALO:HALO + rc]
        emit(r0, out if b_ref is None else out + b_ref[...])
    _chunks(L, rc, conv)


def _long_conv_rows(u_ref, t1_ref, t2_ref, mid_ref, skip, tb_ref, ue_ref, uo_ref, spec_e_ref, spec_o_ref,
                    xec_ref, xes_ref, xoc_ref, xos_ref, c_ref, emit):
    nslab, L, _ = u_ref.shape
    h = L // 2
    rm = min(h, 64)
    ri = min(h, 1024)
    rc = min(L, 256)

    ua_mid, ub_mid = _split_rows(u_ref, ue_ref, uo_ref)
    spec_e_ref[...] = jnp.dot(tb_ref[0:2 * h, :], ue_ref[...], preferred_element_type=F32)
    spec_o_ref[...] = jnp.dot(tb_ref[2 * h:4 * h, :], uo_ref[...], preferred_element_type=F32)

    def mix(r0):
        ae = spec_e_ref[pl.ds(r0, rm), :]
        be = spec_e_ref[pl.ds(h + r0, rm), :]
        ao = spec_o_ref[pl.ds(r0, rm), :]
        bo = spec_o_ref[pl.ds(h + r0, rm), :]
        p0, p1, q0, q1 = ae + ao, ae - ao, be + bo, bo - be
        t1_lo = t1_ref[pl.ds(r0, rm), :]
        t2_lo = t2_ref[pl.ds(r0, rm), :]
        t1_hi = t1_ref[pl.ds(h + r0, rm), :]
        t2_hi = t2_ref[pl.ds(h + r0, rm), :]
        za0 = p0 * t1_lo + q0 * t2_lo
        zb0 = q0 * t1_lo - p0 * t2_lo
        za1 = p1 * t1_hi + q1 * t2_hi
        zb1 = q1 * t1_hi - p1 * t2_hi
        xec_ref[pl.ds(r0, rm), :] = (za0 + za1).astype(BF16)
        xes_ref[pl.ds(r0, rm), :] = (zb0 - zb1).astype(BF16)
        xoc_ref[pl.ds(r0, rm), :] = (za0 - za1).astype(BF16)
        xos_ref[pl.ds(r0, rm), :] = (zb0 + zb1).astype(BF16)
    _chunks(h, rm, mix)

    t1_mid = mid_ref[0:1, :]
    t2_mid = mid_ref[1:2, :]
    za_mid = ua_mid * t1_mid + ub_mid * t2_mid
    zb_mid = ub_mid * t1_mid - ua_mid * t2_mid
    sgn = _sign_col(ri)

    def inverse(r0):
        ye = jnp.dot(tb_ref[pl.ds(r0, ri), :], xec_ref[...], preferred_element_type=F32)
        ye = ye + jnp.dot(tb_ref[pl.ds(h + r0, ri), :], xes_ref[...], preferred_element_type=F32)
        yo = jnp.dot(tb_ref[pl.ds(4 * h + r0, ri), :], xoc_ref[...], preferred_element_type=F32)
        yo = yo + jnp.dot(tb_ref[pl.ds(5 * h + r0, ri), :], xos_ref[...], preferred_element_type=F32)
        ye = ye + sgn * za_mid
        yo = yo + sgn * zb_mid
        for s in range(nslab):
            c_ref[s, pl.ds(2 * r0, ri, stride=2), :] = ye[:, s * LANES:(s + 1) * LANES]
            c_ref[s, pl.ds(2 * r0 + 1, ri, stride=2), :] = yo[:, s * LANES:(s + 1) * LANES]
    _chunks(h, ri, inverse)

    def finish(r0):
        emit(r0, _get_slabs(c_ref, r0, rc) + _get_slabs(u_ref, r0, rc) * skip)
    _chunks(L, rc, finish)


def _hyena_body(p1_ref, p2_ref, pv_ref, cw1_ref, cw2_ref, cwv_ref, cb1_ref, cb2_ref, cbv_ref,
                bias_ref, t1a_ref, t2a_ref, mida_ref, t1b_ref, t2b_ref, midb_ref, tb_ref, o_ref,
                pad_ref, u_ref, x1_ref, x2_ref, ue_ref, uo_ref, spec_e_ref, spec_o_ref,
                xec_ref, xes_ref, xoc_ref, xos_ref, c_ref):
    def put(ref):
        def emit(r0, rows):
            ref[pl.ds(r0, rows.shape[0]), :] = rows
        return emit

    def put_u(r0, rows):
        _put_slabs(u_ref, r0, rows)

    _conv_rows(p1_ref, cw1_ref, cb1_ref, 1, pad_ref, put(x1_ref))
    _conv_rows(p2_ref, cw2_ref, cb2_ref, 1, pad_ref, put(x2_ref))
    _conv_rows(pv_ref, cwv_ref, cbv_ref, 1, pad_ref, put_u)
    work = (tb_ref, ue_ref, uo_ref, spec_e_ref, spec_o_ref, xec_ref, xes_ref, xoc_ref, xos_ref, c_ref)

    def first(r0, c1):
        put_u(r0, x1_ref[pl.ds(r0, c1.shape[0]), :] * c1)
    _long_conv_rows(u_ref, t1a_ref, t2a_ref, mida_ref, bias_ref[0:1, :], *work, first)

    def second(r0, c2):
        o_ref[pl.ds(r0, c2.shape[0]), :] = (x2_ref[pl.ds(r0, c2.shape[0]), :] * c2).astype(o_ref.dtype)
    _long_conv_rows(u_ref, t1b_ref, t2b_ref, midb_ref, bias_ref[1:2, :], *work, second)


def hyena_mixer(proj, conv_w, conv_b, bias, t1, t2, mid, fs):
    b, l, _ = proj.shape
    d = bias.shape[1]
    tn = _tile(d, 256)
    nj = d // tn
    k = conv_w.shape[0]
    hl = l // 2
    cb = conv_b.reshape(1, 3 * d)
    sec = lambda s: pl.BlockSpec((None, l, tn), lambda j, bb: (bb, 0, s * nj + j))
    cws = lambda s: pl.BlockSpec((k, tn), lambda j, bb: (0, s * nj + j))
    cbs = lambda s: pl.BlockSpec((1, tn), lambda j, bb: (0, s * nj + j))
    tab = lambda o, r: _resident((r, tn), lambda j, bb: (0, o * nj + j))
    half_bf16 = pltpu.VMEM((hl, tn), BF16)
    natural = pltpu.VMEM((tn // LANES, l, LANES), F32)
    return pl.pallas_call(
        _hyena_body,
        grid=(nj, b),
        in_specs=[sec(0), sec(1), sec(2), cws(0), cws(1), cws(2), cbs(0), cbs(1), cbs(2),
                  pl.BlockSpec((HYENA_ORDER, tn), lambda j, bb: (0, j)),
                  tab(0, l), tab(0, l), tab(0, 2), tab(1, l), tab(1, l), tab(1, 2),
                  _resident(fs.shape, lambda j, bb: (0, 0))],
        out_specs=pl.BlockSpec((None, l, tn), lambda j, bb: (bb, 0, j)),
        out_shape=jax.ShapeDtypeStruct((b, l, d), BF16),
        scratch_shapes=[pltpu.VMEM((l + 2 * HALO, tn), F32),
                        natural,
                        pltpu.VMEM((l, tn), F32), pltpu.VMEM((l, tn), F32),
                        half_bf16, half_bf16,
                        pltpu.VMEM((l, tn), F32), pltpu.VMEM((l, tn), F32),
                        half_bf16, half_bf16, half_bf16, half_bf16,
                        natural],
        compiler_params=_cparams(2),
        name="hyena",
    )(proj, proj, proj, conv_w, conv_w, conv_w, cb, cb, cb, bias, t1, t2, mid, t1, t2, mid, fs)


def _lru_gates_body(rec_ref, cw_ref, cb_ref, wa_ref, ba_ref, wx_ref, bx_ref, lam_ref,
                    af_ref, bf_ref, ab_ref, bb_ref):
    xc = _dwconv(rec_ref[...].astype(F32), cw_ref, 2) + cb_ref[...]
    xcb = xc.astype(BF16)
    xh = 0.5 * xc
    for dr, (a_out, b_out) in enumerate(((af_ref, bf_ref), (ab_ref, bb_ref))):
        tr = jnp.tanh(jnp.dot(xcb, wa_ref[dr], preferred_element_type=F32) + 0.5 * ba_ref[dr:dr + 1, :])
        ti = jnp.tanh(jnp.dot(xcb, wx_ref[dr], preferred_element_type=F32) + 0.5 * bx_ref[dr:dr + 1, :])
        ch = (-0.5 * LRU_C * math.log2(math.e)) * jnp.log1p(jnp.exp(-lam_ref[dr:dr + 1, :]))
        a = jnp.exp2(tr * ch + ch)
        a_out[...] = a
        b_out[...] = jnp.sqrt(1.0 - a * a) * (ti * xh + xh)


def _block_diag_tiles(w, tn):
    two, h, bs, _ = w.shape
    hpt = tn // bs
    wt = w.reshape(two, h // hpt, hpt, bs, bs)
    eye = jnp.eye(hpt, dtype=w.dtype)
    dense = jnp.einsum('tnhij,hg->tnhigj', wt, eye)
    return dense.reshape(two, h // hpt, tn, tn)


def lru_gates(proj, rec_sec, conv_w, conv_b, wa, ba, wx, bx, lam):
    b, l, n = proj.shape
    d = conv_w.shape[1]
    tn = _tile(d, 256)
    nj = d // tn
    k = conv_w.shape[0]
    wa_t = (0.5 * _block_diag_tiles(wa, tn)).astype(BF16)
    wx_t = (0.5 * _block_diag_tiles(wx, tn)).astype(BF16)
    vec = lambda r: pl.BlockSpec((r, tn), lambda bb, j: (0, j))
    wsp = pl.BlockSpec((2, None, tn, tn), lambda bb, j: (0, j, 0, 0))
    out = pl.BlockSpec((None, l, tn), lambda bb, j: (bb, 0, j))
    shp = jax.ShapeDtypeStruct((b, l, d), F32)
    return pl.pallas_call(
        _lru_gates_body,
        grid=(b, nj),
        in_specs=[pl.BlockSpec((None, l, tn), lambda bb, j: (bb, 0, rec_sec * nj + j)),
                  vec(k), vec(1), wsp, vec(2), wsp, vec(2), vec(2)],
        out_specs=[out, out, out, out],
        out_shape=[shp, shp, shp, shp],
        compiler_params=_cparams(2),
        name="lru_gates",
    )(proj, conv_w, conv_b.reshape(1, d), wa_t, ba, wx_t, bx, lam)


def _lru_scan_body(a_ref, b_ref, h0_ref, o_ref, carry_ref, *, reverse):
    tl = a_ref.shape[0]
    groups = tl // 8

    @pl.when(pl.program_id(1) == 0)
    def _():
        carry_ref[0:1, :] = h0_ref[...]

    row = lax.broadcasted_iota(jnp.int32, (8, a_ref.shape[1]), 0)

    def step(g, h):
        base = pl.multiple_of((groups - 1 - g if reverse else g) * 8, 8)
        a = a_ref[pl.ds(base, 8), :]
        b = b_ref[pl.ds(base, 8), :]
        for s in (1, 2, 4):
            has_prev = row < 8 - s if reverse else row >= s
            shift = 8 - s if reverse else s
            b = jnp.where(has_prev, a, 0.0) * pltpu.roll(b, shift, axis=0) + b
            a = a * jnp.where(has_prev, pltpu.roll(a, shift, axis=0), 1.0)
        hs = a * h + b
        o_ref[pl.ds(base, 8), :] = hs
        return hs[0:1, :] if reverse else hs[7:8, :]

    carry_ref[0:1, :] = lax.fori_loop(0, groups, step, carry_ref[0:1, :])


def lru_scan(a, b, h0, reverse):
    bsz, l, d = a.shape
    tl = _tile(l, 512)
    nt = l // tl
    tmap = (lambda bb, i: (bb, nt - 1 - i, 0)) if reverse else (lambda bb, i: (bb, i, 0))
    return pl.pallas_call(
        functools.partial(_lru_scan_body, reverse=reverse),
        grid=(bsz, nt),
        in_specs=[pl.BlockSpec((None, tl, d), tmap),
                  pl.BlockSpec((None, tl, d), tmap),
                  pl.BlockSpec((None, 1, d), lambda bb, i: (bb, 0, 0))],
        out_specs=pl.BlockSpec((None, tl, d), tmap),
        out_shape=jax.ShapeDtypeStruct((bsz, l, d), F32),
        scratch_shapes=[pltpu.VMEM((8, d), F32)],
        compiler_params=_cparams(2),
        name="lru_scan",
    )(a, b, h0.reshape(bsz, 1, d))


def _shortconv_body(bg_ref, cg_ref, xv_ref, w_ref, o_ref):
    prod = cg_ref[...].astype(F32) * xv_ref[...].astype(F32)
    o_ref[...] = (bg_ref[...].astype(F32) * _dwconv(prod, w_ref, 1)).astype(o_ref.dtype)


def shortconv_mixer(proj, first_sec, conv_w):
    b, l, _ = proj.shape
    k, d = conv_w.shape
    tn = _tile(d, 256)
    nj = d // tn
    sec = lambda s: pl.BlockSpec((None, l, tn), lambda bb, j: (bb, 0, (first_sec + s) * nj + j))
    return pl.pallas_call(
        _shortconv_body,
        grid=(b, nj),
        in_specs=[sec(0), sec(1), sec(2), pl.BlockSpec((k, tn), lambda bb, j: (0, j))],
        out_specs=pl.BlockSpec((None, l, tn), lambda bb, j: (bb, 0, j)),
        out_shape=jax.ShapeDtypeStruct((b, l, d), BF16),
        compiler_params=_cparams(2),
        name="shortconv",
    )(proj, proj, proj, conv_w)


def _gelu_tanh(x):
    return 0.5 * x * (1.0 + jnp.tanh(math.sqrt(2.0 / math.pi) * (x + 0.044715 * (x * x * x))))


def _merge_body(ya_ref, yc_ref, lg_ref, hf_ref, hb_ref, ga_ref, gb_ref, gc_ref, x_ref,
                mg_ref, g2_ref, sh_ref, sc_ref, wa_ref, wb_ref, wc_ref, wo_ref, wr_ref,
                xo_ref, h2_ref, aff_ref):
    yb = (_gelu_tanh(lg_ref[...].astype(F32)) * (hf_ref[...] + hb_ref[...])).astype(BF16)
    merged = _sigmoid(ga_ref[...].astype(F32)) * jnp.dot(ya_ref[...], wa_ref[...], preferred_element_type=F32)
    merged += _sigmoid(gb_ref[...].astype(F32)) * jnp.dot(yb, wb_ref[...], preferred_element_type=F32)
    merged += _sigmoid(gc_ref[...].astype(F32)) * jnp.dot(yc_ref[...], wc_ref[...], preferred_element_type=F32)
    out = jnp.dot(merged.astype(BF16), wo_ref[...], preferred_element_type=F32)
    x = x_ref[...] + mg_ref[...] * out
    xo_ref[...] = x
    h2 = _rms_modulate(x, g2_ref[...], sh_ref[...], sc_ref[...])
    h2_hi = h2.astype(BF16)
    h2_ref[...] = h2_hi
    h2_lo = (h2 - h2_hi.astype(F32)).astype(BF16)
    wr = wr_ref[...]
    wr_hi = wr.astype(BF16)
    wr_lo = (wr - wr_hi.astype(F32)).astype(BF16)
    logits = (jnp.dot(h2_hi, wr_hi, preferred_element_type=F32)
              + jnp.dot(h2_hi, wr_lo, preferred_element_type=F32)
              + jnp.dot(h2_lo, wr_hi, preferred_element_type=F32))
    e = jnp.exp(logits - jnp.max(logits, axis=-1, keepdims=True))
    aff_ref[...] = e / jnp.sum(e, axis=-1, keepdims=True)


def merge_mixers(proj, gate_sec, lg_sec, ya, yc, hf, hb, x, mod_gate, g2, shift2, scale2,
                 w_a, w_b, w_c, w_o, w_r):
    b, l, d = x.shape
    e = w_r.shape[1]
    tl = _tile(l, 512)
    row = lambda: pl.BlockSpec((None, tl, d), lambda bb, i: (bb, i, 0))
    sec = lambda s: pl.BlockSpec((None, tl, d), lambda bb, i: (bb, i, s))
    modv = lambda: pl.BlockSpec((None, 1, d), lambda bb, i: (bb, 0, 0))
    wsp = lambda: _resident((d, d), lambda bb, i: (0, 0))
    return pl.pallas_call(
        _merge_body,
        grid=(b, l // tl),
        in_specs=[row(), row(), sec(lg_sec), row(), row(),
                  sec(gate_sec), sec(gate_sec + 1), sec(gate_sec + 2), row(),
                  modv(), pl.BlockSpec((1, d), lambda bb, i: (0, 0)), modv(), modv(),
                  wsp(), wsp(), wsp(), wsp(), _resident((d, e), lambda bb, i: (0, 0))],
        out_specs=[row(), row(), pl.BlockSpec((None, tl, e), lambda bb, i: (bb, i, 0))],
        out_shape=[jax.ShapeDtypeStruct((b, l, d), F32),
                   jax.ShapeDtypeStruct((b, l, d), BF16),
                   jax.ShapeDtypeStruct((b, l, e), F32)],
        compiler_params=_cparams(2),
        name="merge",
    )(ya, yc, proj, hf, hb, proj, proj, proj, x, mod_gate, g2.reshape(1, d), shift2, scale2,
      w_a, w_b, w_c, w_o, w_r)


def _select_body(aff_ref, tri_ref, rank_ref, *, cap):
    bits = lax.bitcast_convert_type(aff_ref[...], jnp.int32)

    def refine(i, thr):
        cand = thr | jnp.left_shift(jnp.int32(1), 30 - i)
        cnt = jnp.sum((bits >= cand).astype(F32), axis=1, keepdims=True)
        return jnp.where(cnt >= cap, cand, thr)

    thr = lax.fori_loop(0, 31, refine, jnp.zeros((bits.shape[0], 1), jnp.int32))
    gt = bits > thr
    eq = bits == thr
    n_gt = jnp.sum(gt.astype(F32), axis=1, keepdims=True)
    eq_rank = jnp.dot(eq.astype(BF16), tri_ref[...], preferred_element_type=F32)
    sel = gt | (eq & (eq_rank <= cap - n_gt))
    rank = jnp.dot(sel.astype(BF16), tri_ref[...], preferred_element_type=F32) - 1.0
    rank_ref[...] = jnp.where(sel, rank, -1.0).astype(jnp.int32)


def select_tokens(aff_t, cap):
    r, n = aff_t.shape
    tri = jnp.asarray(_prefix_table(n), BF16)
    return pl.pallas_call(
        functools.partial(_select_body, cap=cap),
        grid=(1,),
        in_specs=[pl.BlockSpec((r, n), lambda i: (0, 0)), pl.BlockSpec((n, n), lambda i: (0, 0))],
        out_specs=pl.BlockSpec((r, n), lambda i: (0, 0)),
        out_shape=jax.ShapeDtypeStruct((r, n), jnp.int32),
        compiler_params=_cparams(1),
        name="select",
    )(aff_t, tri)


GATHER_GROUP = 8


def _gather_body(rank_ref, hs_ref, o_ref):
    grp, cap, d = o_ref.shape
    n = hs_ref.shape[0]
    g0 = pl.multiple_of(pl.program_id(1) * grp, grp)
    ranks = rank_ref[pl.ds(g0, grp), :]
    slot = lax.broadcasted_iota(jnp.int32, (cap, n), 0)
    onehot = jnp.concatenate([jnp.where(slot == ranks[e:e + 1, :], 1.0, 0.0).astype(BF16) for e in range(grp)],
                             axis=0)
    rows = jnp.dot(onehot, hs_ref[...], preferred_element_type=F32)
    o_ref[...] = rows.reshape(grp, cap, d).astype(o_ref.dtype)


def _gather_into_body(rank_ref, hs_ref, prev_ref, o_ref):
    del prev_ref
    _gather_body(rank_ref, hs_ref, o_ref)


def gather_tokens(rank_t, hs, cap, total_rows, row_off, into=None):
    s, e, n = rank_t.shape
    d = hs.shape[2]
    boff = row_off // cap
    in_specs = [pl.BlockSpec((None, e, n), lambda ss, ee: (ss, 0, 0)),
                pl.BlockSpec((None, n, d), lambda ss, ee: (ss, 0, 0))]
    args = [rank_t, hs]
    if into is not None:
        in_specs.append(pl.BlockSpec(memory_space=pl.ANY))
        args.append(into)
    return pl.pallas_call(
        _gather_body if into is None else _gather_into_body,
        grid=(s, e // GATHER_GROUP),
        in_specs=in_specs,
        out_specs=pl.BlockSpec((GATHER_GROUP, cap, d), lambda ss, ee: (ee, boff + ss, 0)),
        out_shape=jax.ShapeDtypeStruct((e, total_rows, d), BF16),
        input_output_aliases={} if into is None else {2: 0},
        compiler_params=_cparams(2),
        name="gather",
    )(*args)


def _ffn_body(x_ref, wg_ref, wu_ref, wd_ref, o_ref, acc_ref):
    f = pl.program_id(1)

    @pl.when(f == 0)
    def _():
        acc_ref[...] = jnp.zeros_like(acc_ref)

    x = x_ref[...]
    g = jnp.dot(x, wg_ref[...].astype(BF16), preferred_element_type=F32)
    u = jnp.dot(x, wu_ref[...].astype(BF16), preferred_element_type=F32)
    hid = (g * _sigmoid(g) * u).astype(BF16)
    acc_ref[...] += jnp.dot(hid, wd_ref[...].astype(BF16), preferred_element_type=F32)

    @pl.when(f == pl.num_programs(1) - 1)
    def _():
        o_ref[...] = acc_ref[...].astype(o_ref.dtype)


def expert_ffn(xs, wg, wu, wd, layer):
    e, m, d = xs.shape
    fdim = wg.shape[3]
    tf = _tile(fdim, 256)
    return pl.pallas_call(
        _ffn_body,
        grid=(e, fdim // tf),
        in_specs=[pl.BlockSpec((None, m, d), lambda ee, f: (ee, 0, 0)),
                  pl.BlockSpec((None, None, d, tf), lambda ee, f: (layer, ee, 0, f)),
                  pl.BlockSpec((None, None, d, tf), lambda ee, f: (layer, ee, 0, f)),
                  pl.BlockSpec((None, None, tf, d), lambda ee, f: (layer, ee, f, 0))],
        out_specs=pl.BlockSpec((None, m, d), lambda ee, f: (ee, 0, 0)),
        out_shape=jax.ShapeDtypeStruct((e, m, d), BF16),
        scratch_shapes=[pltpu.VMEM((m, d), F32)],
        compiler_params=_cparams(2),
        name="ffn",
    )(xs, wg, wu, wd)


def _scatter_body(rank_ref, aff_ref, ys_ref, x_ref, mg_ref, gf_ref, o_ref, *, final_norm):
    tq = rank_ref.shape[0]
    n_exp, cap, _ = ys_ref.shape
    slot = lax.broadcasted_iota(jnp.int32, (tq, cap), 1)
    rank = rank_ref[...]
    aff = aff_ref[...]
    acc = None
    for e in range(n_exp):
        q = jnp.where(slot == rank[:, e:e + 1], aff[:, e:e + 1], 0.0).astype(BF16)
        part = jnp.dot(q, ys_ref[e], preferred_element_type=F32)
        acc = part if acc is None else acc + part
    x = x_ref[...] + mg_ref[...] * acc
    if final_norm:
        x = x * lax.rsqrt(jnp.mean(x * x, axis=-1, keepdims=True) + NORM_EPS) * gf_ref[...]
    o_ref[...] = x


def scatter_residual(rank, aff, ys, row_off, cap, x, mod_gate, final_g):
    s, n, e = rank.shape
    d = x.shape[2]
    tq = _tile(n, 1024)
    boff = row_off // cap
    final_norm = final_g is not None
    gf = (final_g if final_norm else jnp.ones((d,), F32)).reshape(1, d)
    return pl.pallas_call(
        functools.partial(_scatter_body, final_norm=final_norm),
        grid=(s, n // tq),
        in_specs=[pl.BlockSpec((None, tq, e), lambda ss, i: (ss, i, 0)),
                  pl.BlockSpec((None, tq, e), lambda ss, i: (ss, i, 0)),
                  pl.BlockSpec((e, cap, d), lambda ss, i: (0, boff + ss, 0)),
                  pl.BlockSpec((None, tq, d), lambda ss, i: (ss, i, 0)),
                  pl.BlockSpec((None, 1, d), lambda ss, i: (ss, 0, 0)),
                  pl.BlockSpec((1, d), lambda ss, i: (0, 0))],
        out_specs=pl.BlockSpec((None, tq, d), lambda ss, i: (ss, i, 0)),
        out_shape=jax.ShapeDtypeStruct(x.shape, F32),
        compiler_params=_cparams(2),
        name="scatter",
    )(rank, aff, ys, x, mod_gate, gf)


def _mixer_and_route(x, h0_f, h0_b, mods, lw, fs, shared_mod=False, pos_embed=None):
    b, l, d = x.shape
    if shared_mod:
        flat = lambda a: a.reshape(1, b * l, a.shape[-1])
        unflat = lambda a: a.reshape(b, l, a.shape[-1])
        mods = [m[:1] for m in mods]
    else:
        flat = unflat = lambda a: a
    if pos_embed is None:
        proj = unflat(norm_proj(flat(x), lw['norm1_g'], mods[0], mods[1], lw['w_in']))
    else:
        proj, x = norm_proj(x, lw['norm1_g'], mods[0], mods[1], lw['w_in'], pos_embed)
    t1, t2, mid = hy_filter_tables(l, lw['hy_filt_w1'], lw['hy_filt_b1'], lw['hy_filt_w2'], lw['hy_filt_b2'],
                                  lw['hy_filt_w3'], lw['hy_filt_freq'], fs)
    ya = hyena_mixer(proj, lw['hy_conv_w'], lw['hy_conv_b'], lw['hy_bias'], t1, t2, mid, fs)
    a_f, b_f, a_b, b_b = lru_gates(proj, 4, lw['lru_conv_w'], lw['lru_conv_b'], lw['lru_wa'], lw['lru_ba'],
                                   lw['lru_wx'], lw['lru_bx'], lw['lru_lambda'])
    hf = lru_scan(a_f, b_f, h0_f, False)
    hb = lru_scan(a_b, b_b, h0_b, True)
    yc = shortconv_mixer(proj, 5, lw['sc_conv_w'])
    x_new, h2, aff = merge_mixers(flat(proj), 8, 3, flat(ya), flat(yc), flat(hf), flat(hb), flat(x), mods[2],
                                  lw['norm2_g'], mods[3], mods[4],
                                  lw['w_hy_out'], lw['w_lru_out'], lw['w_sc_out'], lw['w_o'], lw['w_router'])
    return unflat(x_new), unflat(h2), unflat(aff), hf[:, -1], hb[:, 0]


def _capacity(n, e):
    return EC_CAPACITY * n // e


def _route(h2, aff, total_rows, row_off, into):
    s, n, e = aff.shape
    cap = _capacity(n, e)
    aff_t = jnp.transpose(aff, (0, 2, 1))
    rank_t = select_tokens(aff_t.reshape(s * e, n), cap).reshape(s, e, n)
    xs = gather_tokens(rank_t, h2, cap, total_rows, row_off, into)
    return xs, jnp.transpose(rank_t, (0, 2, 1))


def kernel(x, c, ctx, c_ctx, w_ada, b_ada, norm1_g, norm2_g, w_in, hy_conv_w, hy_conv_b, hy_filt_w1, hy_filt_b1, hy_filt_w2, hy_filt_b2, hy_filt_w3, hy_filt_freq, hy_bias, lru_conv_w, lru_conv_b, lru_wa, lru_ba, lru_wx, lru_bx, lru_lambda, sc_conv_w, w_hy_out, w_lru_out, w_sc_out, w_o, w_router, w_exp_gate, w_exp_up, w_exp_down, final_norm_g):
    bsz, n_lat, d = x.shape
    n_ctx = ctx.shape[1]
    depth = w_ada.shape[0]
    rows = n_lat // GRID_W

    pos_embed = jnp.asarray(_pos_embed(rows, d))
    xc = ctx
    fs_lat = jnp.asarray(_dft_table(n_lat), BF16)
    fs_ctx = jnp.asarray(_dft_table(n_ctx), BF16)

    cc = jnp.zeros((2 * bsz, d), F32).at[:bsz].set(c).at[bsz].set(c_ctx)
    mods = ada_mods(cc, w_ada, b_ada)
    zeros_state = jnp.zeros((bsz, d), F32)

    for l in range(depth):
        last = l == depth - 1
        lw = {
            'norm1_g': norm1_g[l], 'norm2_g': norm2_g[l], 'w_in': w_in[l].astype(BF16),
            'hy_conv_w': hy_conv_w[l], 'hy_conv_b': hy_conv_b[l],
            'hy_filt_w1': hy_filt_w1[l], 'hy_filt_b1': hy_filt_b1[l], 'hy_filt_w2': hy_filt_w2[l],
            'hy_filt_b2': hy_filt_b2[l], 'hy_filt_w3': hy_filt_w3[l], 'hy_filt_freq': hy_filt_freq[l],
            'hy_bias': hy_bias[l], 'lru_conv_w': lru_conv_w[l], 'lru_conv_b': lru_conv_b[l],
            'lru_wa': lru_wa[l], 'lru_ba': lru_ba[l], 'lru_wx': lru_wx[l], 'lru_bx': lru_bx[l],
            'lru_lambda': lru_lambda[l], 'sc_conv_w': sc_conv_w[l],
            'w_hy_out': w_hy_out[l].astype(BF16), 'w_lru_out': w_lru_out[l].astype(BF16),
            'w_sc_out': w_sc_out[l].astype(BF16), 'w_o': w_o[l].astype(BF16), 'w_router': w_router[l],
        }
        mod_l = [mods[l, :bsz, k * d:(k + 1) * d].reshape(bsz, 1, d) for k in range(6)]
        mod_c = [jnp.broadcast_to(mods[l, bsz, k * d:(k + 1) * d].reshape(1, 1, d), (bsz, 1, d)) for k in range(6)]

        n_exp = w_router.shape[2]
        cap_l = _capacity(n_lat, n_exp)
        cap_c = _capacity(n_ctx, n_exp)
        rows_l = bsz * cap_l
        if last:
            proj_c = norm_proj(xc.reshape(1, bsz * n_ctx, d), lw['norm1_g'], mod_c[0][:1], mod_c[1][:1],
                               lw['w_in'][:, 4 * d:5 * d]).reshape(bsz, n_ctx, d)
            a_f, b_f, a_b, b_b = lru_gates(proj_c, 0, lw['lru_conv_w'], lw['lru_conv_b'], lw['lru_wa'], lw['lru_ba'],
                                           lw['lru_wx'], lw['lru_bx'], lw['lru_lambda'])
            state_f = lru_scan(a_f, b_f, zeros_state, False)[:, -1]
            state_b = lru_scan(a_b, b_b, zeros_state, True)[:, 0]
            total_rows = rows_l
            xs = None
        else:
            xc_mid, h2_c, aff_c, state_f, state_b = _mixer_and_route(xc, zeros_state, zeros_state, mod_c, lw, fs_ctx,
                                                                     shared_mod=True)
            total_rows = rows_l + bsz * cap_c
            xs, rank_c = _route(h2_c, aff_c, total_rows, rows_l, None)

        x_mid, h2_l, aff_l, _, _ = _mixer_and_route(x, state_f, state_b, mod_l, lw, fs_lat,
                                                    pos_embed=pos_embed if l == 0 else None)
        xs, rank_l = _route(h2_l, aff_l, total_rows, 0, xs)
        ys = expert_ffn(xs, w_exp_gate, w_exp_up, w_exp_down, l)

        if not last:
            xc = scatter_residual(rank_c, aff_c, ys, rows_l, cap_c, xc_mid, mod_c[5], None)
        x = scatter_residual(rank_l, aff_l, ys, 0, cap_l, x_mid, mod_l[5], final_norm_g if last else None)
    return x
```

```python
import functools
import math

import numpy as np
import jax
import jax.numpy as jnp
from jax import lax
from jax.experimental import pallas as pl
from jax.experimental.pallas import tpu as pltpu

F32 = jnp.float32
BF16 = jnp.bfloat16
HI = lax.Precision.HIGHEST

NORM_EPS = 1e-6
GRID_W = 64
N_EXPERTS = 16
EC_CAPACITY = 2
LRU_C = 8.0
HYENA_ORDER = 2
HYENA_EMB_DIM = 33
HYENA_FAST_DECAY = 0.3
HYENA_SLOW_DECAY = 1.5
HYENA_TARGET = 1e-2
MLP_PAD = 128

V7X_VMEM_BYTES = 64 * 1024 * 1024
VMEM_LIMIT = 56 * 1024 * 1024
assert VMEM_LIMIT < V7X_VMEM_BYTES


def _cparams(n_grid):
    return pltpu.CompilerParams(dimension_semantics=("arbitrary",) * n_grid,
                                vmem_limit_bytes=VMEM_LIMIT)


def _tile(n, pref):
    if n <= pref:
        return n
    t = pref
    while n % t:
        t //= 2
    return t


def _resident(block_shape, index_map):
    return pl.BlockSpec(block_shape, index_map, pipeline_mode=pl.Buffered(1))


@functools.lru_cache(maxsize=None)
def _pos_embed(rows, d):
    def sincos(pos, dim):
        half = dim // 2
        omega = 1.0 / (10000.0 ** (np.arange(half, dtype=np.float64) / half))
        ang = pos[:, None] * omega[None, :]
        return np.concatenate([np.sin(ang), np.cos(ang)], axis=-1)
    half = d // 2
    er = sincos(np.arange(rows, dtype=np.float64), half)
    ec = sincos(np.arange(GRID_W, dtype=np.float64), half)
    emb = np.concatenate([np.broadcast_to(er[:, None, :], (rows, GRID_W, half)),
                          np.broadcast_to(ec[None, :, :], (rows, GRID_W, half))], axis=-1)
    return emb.reshape(rows * GRID_W, d).astype(np.float32)


@functools.lru_cache(maxsize=None)
def _dft_table(L):
    h = L // 2
    k = np.arange(h, dtype=np.int64)
    ang_e = np.pi * ((k[:, None] * (2 * k[None, :])) % (2 * L)).astype(np.float64) / L
    ang_o = np.pi * ((k[:, None] * (2 * k[None, :] + 1)) % (2 * L)).astype(np.float64) / L
    ce, se, co, so = np.cos(ang_e), np.sin(ang_e), np.cos(ang_o), np.sin(ang_o)
    return np.concatenate([ce, se, co, so, co.T, so.T], axis=0).astype(np.float32)


@functools.lru_cache(maxsize=None)
def _filter_feats(L):
    bands = (HYENA_EMB_DIM - 1) // 2
    t01 = np.linspace(0.0, 1.0, L, dtype=np.float32).astype(np.float64)[:, None]
    w = ((2.0 * math.pi / L) * np.arange(L, dtype=np.float32))[:, None].astype(np.float64)
    f = np.linspace(1e-4, bands - 1, bands, dtype=np.float32).astype(np.float64)[None, :]
    feats = np.concatenate([t01, np.cos(f * w), -np.sin(f * w)], axis=-1)
    out = np.zeros((L, MLP_PAD), np.float32)
    out[:, :HYENA_EMB_DIM] = feats
    return out


@functools.lru_cache(maxsize=None)
def _decay_window(L, d):
    max_decay = math.log(HYENA_TARGET) / HYENA_FAST_DECAY
    min_decay = math.log(HYENA_TARGET) / HYENA_SLOW_DECAY
    t01 = np.linspace(0.0, 1.0, L, dtype=np.float32).astype(np.float64)[:, None]
    deltas = np.linspace(min_decay, max_decay, d, dtype=np.float32).astype(np.float64)
    return np.exp(-t01 * np.abs(deltas)[None, :]).astype(np.float32)


@functools.lru_cache(maxsize=None)
def _prefix_table(n):
    return np.triu(np.ones((n, n), np.float32))


def _shift_rows(x, s):
    if s == 0:
        return x
    n = x.shape[0]
    rolled = pltpu.roll(x, s % n, axis=0)
    row = lax.broadcasted_iota(jnp.int32, x.shape, 0)
    keep = row >= s if s > 0 else row < n + s
    return jnp.where(keep, rolled, 0.0)


def _dwconv(x, w_ref, left):
    acc = None
    for k in range(w_ref.shape[0]):
        term = _shift_rows(x, left - k) * w_ref[k:k + 1, :]
        acc = term if acc is None else acc + term
    return acc


def _sigmoid(x):
    return 0.5 * jnp.tanh(0.5 * x) + 0.5


def _rms_modulate(x, g, shift, scale):
    y = x * lax.rsqrt(jnp.mean(x * x, axis=-1, keepdims=True) + NORM_EPS)
    return (y * g) * (1.0 + scale) + shift


def _ada_body(c_ref, w_ref, b_ref, o_ref):
    c = c_ref[...]
    s = c * _sigmoid(c)
    o_ref[...] = jnp.dot(s, w_ref[...], precision=HI, preferred_element_type=F32) + b_ref[...]


def ada_mods(cc, w_ada, b_ada):
    depth, d, n = w_ada.shape
    r = cc.shape[0]
    tn = _tile(n, 1536)
    return pl.pallas_call(
        _ada_body,
        grid=(depth, n // tn),
        in_specs=[pl.BlockSpec((r, d), lambda l, j: (0, 0)),
                  pl.BlockSpec((None, d, tn), lambda l, j: (l, 0, j)),
                  pl.BlockSpec((None, 1, tn), lambda l, j: (l, 0, j))],
        out_specs=pl.BlockSpec((None, r, tn), lambda l, j: (l, 0, j)),
        out_shape=jax.ShapeDtypeStruct((depth, r, n), F32),
        compiler_params=_cparams(2),
        name="ada",
    )(cc, w_ada, b_ada.reshape(depth, 1, n))


def _norm_proj_body(x_ref, g_ref, sh_ref, sc_ref, w_ref, o_ref, h_ref):
    @pl.when(pl.program_id(2) == 0)
    def _():
        h_ref[...] = _rms_modulate(x_ref[...], g_ref[...], sh_ref[...], sc_ref[...]).astype(BF16)

    o_ref[...] = jnp.dot(h_ref[...], w_ref[...], preferred_element_type=F32).astype(o_ref.dtype)


def _embed_norm_proj_body(x_ref, pe_ref, g_ref, sh_ref, sc_ref, w_ref, o_ref, xo_ref, h_ref):
    @pl.when(pl.program_id(2) == 0)
    def _():
        x = x_ref[...] + pe_ref[...]
        xo_ref[...] = x
        h_ref[...] = _rms_modulate(x, g_ref[...], sh_ref[...], sc_ref[...]).astype(BF16)

    o_ref[...] = jnp.dot(h_ref[...], w_ref[...], preferred_element_type=F32).astype(o_ref.dtype)


PROJ_COLS = 2816


def norm_proj(x, g, shift, scale, w, pos_embed=None):
    b, l, d = x.shape
    n = w.shape[1]
    tl = _tile(l, 1024)
    tn = _tile(n, PROJ_COLS)
    row = pl.BlockSpec((None, tl, d), lambda bb, i, j: (bb, i, 0))
    tail = [pl.BlockSpec((1, d), lambda bb, i, j: (0, 0)),
            pl.BlockSpec((None, 1, d), lambda bb, i, j: (bb, 0, 0)),
            pl.BlockSpec((None, 1, d), lambda bb, i, j: (bb, 0, 0)),
            pl.BlockSpec((d, tn), lambda bb, i, j: (0, j))]
    proj_spec = pl.BlockSpec((None, tl, tn), lambda bb, i, j: (bb, i, j))
    proj_shape = jax.ShapeDtypeStruct((b, l, n), BF16)
    common = dict(grid=(b, l // tl, n // tn), scratch_shapes=[pltpu.VMEM((tl, d), BF16)],
                  compiler_params=_cparams(3), name="norm_proj")
    if pos_embed is None:
        return pl.pallas_call(_norm_proj_body, in_specs=[row] + tail, out_specs=proj_spec, out_shape=proj_shape,
                              **common)(x, g.reshape(1, d), shift, scale, w)
    return pl.pallas_call(
        _embed_norm_proj_body,
        in_specs=[row, pl.BlockSpec((tl, d), lambda bb, i, j: (i, 0))] + tail,
        out_specs=[proj_spec, row],
        out_shape=[proj_shape, jax.ShapeDtypeStruct(x.shape, F32)],
        **common)(x, pos_embed, g.reshape(1, d), shift, scale, w)


def _hy_hidden_body(feats_ref, w1_ref, b1_ref, w2_ref, b2_ref, fr_ref, h_ref):
    fr = fr_ref[...]
    h = jnp.sin(fr * (jnp.dot(feats_ref[...], w1_ref[...], precision=HI,
                              preferred_element_type=F32) + b1_ref[...]))
    h_ref[...] = jnp.sin(fr * (jnp.dot(h, w2_ref[...], precision=HI,
                                       preferred_element_type=F32) + b2_ref[...]))


LANES = 128


def _sign_col(rows):
    row = lax.broadcasted_iota(jnp.int32, (rows, 1), 0)
    return (1 - 2 * (row & 1)).astype(F32)


def _put_slabs(nat_ref, r0, rows):
    for s in range(nat_ref.shape[0]):
        nat_ref[s, pl.ds(r0, rows.shape[0]), :] = rows[:, s * LANES:(s + 1) * LANES]


def _get_slabs(nat_ref, r0, n):
    return jnp.concatenate([nat_ref[s, pl.ds(r0, n), :] for s in range(nat_ref.shape[0])], axis=1)


def _split_rows(nat_ref, e_ref, o_ref):
    nslab, L, _ = nat_ref.shape
    h = L // 2
    rd = min(h, 256)
    sgn = _sign_col(rd)

    def body(i, carry):
        m0 = pl.multiple_of(i * rd, rd)
        e = jnp.concatenate([nat_ref[s, pl.ds(2 * m0, rd, stride=2), :] for s in range(nslab)], axis=1)
        o = jnp.concatenate([nat_ref[s, pl.ds(2 * m0 + 1, rd, stride=2), :] for s in range(nslab)], axis=1)
        e_ref[pl.ds(m0, rd), :] = e.astype(BF16)
        o_ref[pl.ds(m0, rd), :] = o.astype(BF16)
        return (carry[0] + jnp.sum(e * sgn, axis=0, keepdims=True),
                carry[1] + jnp.sum(o * sgn, axis=0, keepdims=True))

    zero = jnp.zeros((1, nslab * LANES), F32)
    return lax.fori_loop(0, h // rd, body, (zero, zero))


def _hy_filter_body(h_ref, w3f_ref, w3b_ref, decay_ref, tb_ref, t1_ref, t2_ref, mid_ref,
                    p_ref, q_ref, pe_ref, po_ref, qe_ref, qo_ref):
    L = h_ref.shape[0]
    hl = L // 2
    hid = h_ref[...]
    decay = decay_ref[...]
    fwd = jnp.dot(hid, w3f_ref[...], precision=HI, preferred_element_type=F32) * decay
    bwd = jnp.dot(hid, w3b_ref[...], precision=HI, preferred_element_type=F32) * decay
    row = lax.broadcasted_iota(jnp.int32, fwd.shape, 0)
    bwd = jnp.where(row >= 1, bwd, 0.0)
    inv = 1.0 / (jnp.sum(jnp.abs(fwd), axis=0, keepdims=True)
                 + jnp.sum(jnp.abs(bwd), axis=0, keepdims=True))
    _put_slabs(p_ref, 0, fwd + bwd)
    _put_slabs(q_ref, 0, bwd - fwd)
    k_re_mid, _ = _split_rows(p_ref, pe_ref, po_ref)
    _, k_im_mid = _split_rows(q_ref, qe_ref, qo_ref)
    a_e = jnp.dot(tb_ref[0:hl, :], pe_ref[...], preferred_element_type=F32)
    a_o = jnp.dot(tb_ref[2 * hl:3 * hl, :], po_ref[...], preferred_element_type=F32)
    b_e = jnp.dot(tb_ref[hl:2 * hl, :], qe_ref[...], preferred_element_type=F32)
    b_o = jnp.dot(tb_ref[3 * hl:4 * hl, :], qo_ref[...], preferred_element_type=F32)
    n_fft = 2.0 * L
    row_h = lax.broadcasted_iota(jnp.int32, (hl, fwd.shape[1]), 0)
    w_k = jnp.where(row_h >= 1, 2.0 / n_fft, 1.0 / n_fft) * inv
    w_s = (2.0 / n_fft) * inv
    t1_ref[0:hl, :] = (a_e + a_o) * w_k
    t1_ref[hl:L, :] = (a_e - a_o) * w_k
    t2_ref[0:hl, :] = (b_e + b_o) * w_s
    t2_ref[hl:L, :] = (b_o - b_e) * w_s
    mid_ref[0:1, :] = k_re_mid * w_s
    mid_ref[1:2, :] = k_im_mid * w_s


def hy_filter_tables(L, w1, b1, w2, b2, w3, freq, fs):
    d = w3.shape[1] // (HYENA_ORDER * 2)
    fw = w1.shape[1]
    pad = lambda a, r, c: jnp.zeros((r, c), F32).at[:a.shape[0], :a.shape[1]].set(a)
    w1p = pad(w1, MLP_PAD, MLP_PAD)
    w2p = pad(w2, MLP_PAD, MLP_PAD)
    w3p = pad(w3, MLP_PAD, w3.shape[1])
    b1p = pad(b1.reshape(1, fw), 1, MLP_PAD)
    b2p = pad(b2.reshape(1, fw), 1, MLP_PAD)
    frp = pad(freq.reshape(1, fw), 1, MLP_PAD)
    feats = jnp.asarray(_filter_feats(L))
    decay = jnp.asarray(_decay_window(L, d))
    tn = _tile(d, 256)
    nj = d // tn
    full = lambda o, j: (0, 0)
    one = lambda i: (0, 0)
    hidden = pl.pallas_call(
        _hy_hidden_body,
        grid=(1,),
        in_specs=[pl.BlockSpec((L, MLP_PAD), one),
                  pl.BlockSpec((MLP_PAD, MLP_PAD), one),
                  pl.BlockSpec((1, MLP_PAD), one),
                  pl.BlockSpec((MLP_PAD, MLP_PAD), one),
                  pl.BlockSpec((1, MLP_PAD), one),
                  pl.BlockSpec((1, MLP_PAD), one)],
        out_specs=pl.BlockSpec((L, MLP_PAD), one),
        out_shape=jax.ShapeDtypeStruct((L, MLP_PAD), F32),
        compiler_params=_cparams(1),
        name="hy_hidden",
    )(feats, w1p, b1p, w2p, b2p, frp)
    hl = L // 2
    t1, t2, mid = pl.pallas_call(
        _hy_filter_body,
        grid=(HYENA_ORDER, nj),
        in_specs=[pl.BlockSpec((L, MLP_PAD), full),
                  pl.BlockSpec((MLP_PAD, tn), lambda o, j: (0, o * 2 * nj + j)),
                  pl.BlockSpec((MLP_PAD, tn), lambda o, j: (0, o * 2 * nj + nj + j)),
                  pl.BlockSpec((L, tn), lambda o, j: (0, j)),
                  _resident(fs.shape, full)],
        out_specs=[pl.BlockSpec((L, tn), lambda o, j: (0, o * nj + j)),
                   pl.BlockSpec((L, tn), lambda o, j: (0, o * nj + j)),
                   pl.BlockSpec((2, tn), lambda o, j: (0, o * nj + j))],
        out_shape=[jax.ShapeDtypeStruct((L, HYENA_ORDER * d), F32),
                   jax.ShapeDtypeStruct((L, HYENA_ORDER * d), F32),
                   jax.ShapeDtypeStruct((2, HYENA_ORDER * d), F32)],
        scratch_shapes=[pltpu.VMEM((tn // LANES, L, LANES), F32), pltpu.VMEM((tn // LANES, L, LANES), F32),
                        pltpu.VMEM((hl, tn), BF16), pltpu.VMEM((hl, tn), BF16),
                        pltpu.VMEM((hl, tn), BF16), pltpu.VMEM((hl, tn), BF16)],
        compiler_params=_cparams(2),
        name="hy_filter",
    )(hidden, w3p, w3p, decay, fs)
    return t1, t2, mid


HALO = 8


def _chunks(n_rows, chunk, fn):
    def body(c, carry):
        fn(pl.multiple_of(c * chunk, chunk))
        return carry
    lax.fori_loop(0, n_rows // chunk, body, 0)


def _conv_rows(src_ref, w_ref, b_ref, left, pad_ref, emit):
    L = src_ref.shape[0]
    rc = min(L, 256)
    win = rc + 2 * HALO
    zeros = jnp.zeros((HALO, pad_ref.shape[1]), F32)
    pad_ref[0:HALO, :] = zeros
    pad_ref[L + HALO:L + 2 * HALO, :] = zeros

    def fill(r0):
        pad_ref[pl.ds(r0 + HALO, rc), :] = src_ref[pl.ds(r0, rc), :].astype(F32)
    _chunks(L, rc, fill)

    def conv(r0):
        w = pad_ref[pl.ds(r0, win), :]
        acc = None
        for k in range(w_ref.shape[0]):
            s = left - k
            term = (w if s == 0 else pltpu.roll(w, s % win, axis=0)) * w_ref[k:k + 1, :]
            acc = term if acc is None else acc + term
        out = acc[HALO:HALO + rc]
        emit(r0, out if b_ref is None else out + b_ref[...])
    _chunks(L, rc, conv)


def _long_conv_rows(u_ref, t1_ref, t2_ref, mid_ref, skip, tb_ref, ue_ref, uo_ref, spec_e_ref, spec_o_ref,
                    xec_ref, xes_ref, xoc_ref, xos_ref, c_ref, emit):
    nslab, L, _ = u_ref.shape
    h = L // 2
    rm = min(h, 64)
    ri = min(h, 1024)
    rc = min(L, 256)

    ua_mid, ub_mid = _split_rows(u_ref, ue_ref, uo_ref)
    spec_e_ref[...] = jnp.dot(tb_ref[0:2 * h, :], ue_ref[...], preferred_element_type=F32)
    spec_o_ref[...] = jnp.dot(tb_ref[2 * h:4 * h, :], uo_ref[...], preferred_element_type=F32)

    def mix(r0):
        ae = spec_e_ref[pl.ds(r0, rm), :]
        be = spec_e_ref[pl.ds(h + r0, rm), :]
        ao = spec_o_ref[pl.ds(r0, rm), :]
        bo = spec_o_ref[pl.ds(h + r0, rm), :]
        p0, p1, q0, q1 = ae + ao, ae - ao, be + bo, bo - be
        t1_lo = t1_ref[pl.ds(r0, rm), :]
        t2_lo = t2_ref[pl.ds(r0, rm), :]
        t1_hi = t1_ref[pl.ds(h + r0, rm), :]
        t2_hi = t2_ref[pl.ds(h + r0, rm), :]
        za0 = p0 * t1_lo + q0 * t2_lo
        zb0 = q0 * t1_lo - p0 * t2_lo
        za1 = p1 * t1_hi + q1 * t2_hi
        zb1 = q1 * t1_hi - p1 * t2_hi
        xec_ref[pl.ds(r0, rm), :] = (za0 + za1).astype(BF16)
        xes_ref[pl.ds(r0, rm), :] = (zb0 - zb1).astype(BF16)
        xoc_ref[pl.ds(r0, rm), :] = (za0 - za1).astype(BF16)
        xos_ref[pl.ds(r0, rm), :] = (zb0 + zb1).astype(BF16)
    _chunks(h, rm, mix)

    t1_mid = mid_ref[0:1, :]
    t2_mid = mid_ref[1:2, :]
    za_mid = ua_mid * t1_mid + ub_mid * t2_mid
    zb_mid = ub_mid * t1_mid - ua_mid * t2_mid
    sgn = _sign_col(ri)

    def inverse(r0):
        ye = jnp.dot(tb_ref[pl.ds(r0, ri), :], xec_ref[...], preferred_element_type=F32)
        ye = ye + jnp.dot(tb_ref[pl.ds(h + r0, ri), :], xes_ref[...], preferred_element_type=F32)
        yo = jnp.dot(tb_ref[pl.ds(4 * h + r0, ri), :], xoc_ref[...], preferred_element_type=F32)
        yo = yo + jnp.dot(tb_ref[pl.ds(5 * h + r0, ri), :], xos_ref[...], preferred_element_type=F32)
        ye = ye + sgn * za_mid
        yo = yo + sgn * zb_mid
        for s in range(nslab):
            c_ref[s, pl.ds(2 * r0, ri, stride=2), :] = ye[:, s * LANES:(s + 1) * LANES]
            c_ref[s, pl.ds(2 * r0 + 1, ri, stride=2), :] = yo[:, s * LANES:(s + 1) * LANES]
    _chunks(h, ri, inverse)

    def finish(r0):
        emit(r0, _get_slabs(c_ref, r0, rc) + _get_slabs(u_ref, r0, rc) * skip)
    _chunks(L, rc, finish)


def _hyena_body(p1_ref, p2_ref, pv_ref, cw1_ref, cw2_ref, cwv_ref, cb1_ref, cb2_ref, cbv_ref,
                bias_ref, t1a_ref, t2a_ref, mida_ref, t1b_ref, t2b_ref, midb_ref, tb_ref, o_ref,
                pad_ref, u_ref, x1_ref, x2_ref, ue_ref, uo_ref, spec_e_ref, spec_o_ref,
                xec_ref, xes_ref, xoc_ref, xos_ref, c_ref):
    def put(ref):
        def emit(r0, rows):
            ref[pl.ds(r0, rows.shape[0]), :] = rows
        return emit

    def put_u(r0, rows):
        _put_slabs(u_ref, r0, rows)

    _conv_rows(p1_ref, cw1_ref, cb1_ref, 1, pad_ref, put(x1_ref))
    _conv_rows(p2_ref, cw2_ref, cb2_ref, 1, pad_ref, put(x2_ref))
    _conv_rows(pv_ref, cwv_ref, cbv_ref, 1, pad_ref, put_u)
    work = (tb_ref, ue_ref, uo_ref, spec_e_ref, spec_o_ref, xec_ref, xes_ref, xoc_ref, xos_ref, c_ref)

    def first(r0, c1):
        put_u(r0, x1_ref[pl.ds(r0, c1.shape[0]), :] * c1)
    _long_conv_rows(u_ref, t1a_ref, t2a_ref, mida_ref, bias_ref[0:1, :], *work, first)

    def second(r0, c2):
        o_ref[pl.ds(r0, c2.shape[0]), :] = (x2_ref[pl.ds(r0, c2.shape[0]), :] * c2).astype(o_ref.dtype)
    _long_conv_rows(u_ref, t1b_ref, t2b_ref, midb_ref, bias_ref[1:2, :], *work, second)


def hyena_mixer(proj, conv_w, conv_b, bias, t1, t2, mid, fs):
    b, l, _ = proj.shape
    d = bias.shape[1]
    tn = _tile(d, 256)
    nj = d // tn
    k = conv_w.shape[0]
    hl = l // 2
    cb = conv_b.reshape(1, 3 * d)
    sec = lambda s: pl.BlockSpec((None, l, tn), lambda j, bb: (bb, 0, s * nj + j))
    cws = lambda s: pl.BlockSpec((k, tn), lambda j, bb: (0, s * nj + j))
    cbs = lambda s: pl.BlockSpec((1, tn), lambda j, bb: (0, s * nj + j))
    tab = lambda o, r: _resident((r, tn), lambda j, bb: (0, o * nj + j))
    half_bf16 = pltpu.VMEM((hl, tn), BF16)
    natural = pltpu.VMEM((tn // LANES, l, LANES), F32)
    return pl.pallas_call(
        _hyena_body,
        grid=(nj, b),
        in_specs=[sec(0), sec(1), sec(2), cws(0), cws(1), cws(2), cbs(0), cbs(1), cbs(2),
                  pl.BlockSpec((HYENA_ORDER, tn), lambda j, bb: (0, j)),
                  tab(0, l), tab(0, l), tab(0, 2), tab(1, l), tab(1, l), tab(1, 2),
                  _resident(fs.shape, lambda j, bb: (0, 0))],
        out_specs=pl.BlockSpec((None, l, tn), lambda j, bb: (bb, 0, j)),
        out_shape=jax.ShapeDtypeStruct((b, l, d), BF16),
        scratch_shapes=[pltpu.VMEM((l + 2 * HALO, tn), F32),
                        natural,
                        pltpu.VMEM((l, tn), F32), pltpu.VMEM((l, tn), F32),
                        half_bf16, half_bf16,
                        pltpu.VMEM((l, tn), F32), pltpu.VMEM((l, tn), F32),
                        half_bf16, half_bf16, half_bf16, half_bf16,
                        natural],
        compiler_params=_cparams(2),
        name="hyena",
    )(proj, proj, proj, conv_w, conv_w, conv_w, cb, cb, cb, bias, t1, t2, mid, t1, t2, mid, fs)


SQRT_GUARD = 1e-30


def _lru_gates_body(rec_ref, cw_ref, cb_ref, wa_ref, ba_ref, wx_ref, bx_ref, lam_ref,
                    af_ref, bf_ref, ab_ref, bb_ref):
    xc = _dwconv(rec_ref[...].astype(F32), cw_ref, 2) + cb_ref[...]
    xcb = xc.astype(BF16)
    xh = 0.5 * xc
    for dr, (a_out, b_out) in enumerate(((af_ref, bf_ref), (ab_ref, bb_ref))):
        tr = jnp.tanh(jnp.dot(xcb, wa_ref[dr], preferred_element_type=F32) + 0.5 * ba_ref[dr:dr + 1, :])
        ti = jnp.tanh(jnp.dot(xcb, wx_ref[dr], preferred_element_type=F32) + 0.5 * bx_ref[dr:dr + 1, :])
        ch = (-0.5 * LRU_C * math.log2(math.e)) * jnp.log1p(jnp.exp(-lam_ref[dr:dr + 1, :]))
        a = jnp.exp2(tr * ch + ch)
        a_out[...] = a
        om = 1.0 - a * a
        b_out[...] = (om * lax.rsqrt(jnp.maximum(om, SQRT_GUARD))) * (ti * xh + xh)


def _block_diag_tiles(w, tn):
    two, h, bs, _ = w.shape
    hpt = tn // bs
    wt = w.reshape(two, h // hpt, hpt, bs, bs)
    eye = jnp.eye(hpt, dtype=w.dtype)
    dense = jnp.einsum('tnhij,hg->tnhigj', wt, eye)
    return dense.reshape(two, h // hpt, tn, tn)


def lru_gates(proj, rec_sec, conv_w, conv_b, wa, ba, wx, bx, lam):
    b, l, n = proj.shape
    d = conv_w.shape[1]
    tn = _tile(d, 256)
    nj = d // tn
    k = conv_w.shape[0]
    wa_t = (0.5 * _block_diag_tiles(wa, tn)).astype(BF16)
    wx_t = (0.5 * _block_diag_tiles(wx, tn)).astype(BF16)
    vec = lambda r: pl.BlockSpec((r, tn), lambda bb, j: (0, j))
    wsp = pl.BlockSpec((2, None, tn, tn), lambda bb, j: (0, j, 0, 0))
    out = pl.BlockSpec((None, l, tn), lambda bb, j: (bb, 0, j))
    shp = jax.ShapeDtypeStruct((b, l, d), F32)
    return pl.pallas_call(
        _lru_gates_body,
        grid=(b, nj),
        in_specs=[pl.BlockSpec((None, l, tn), lambda bb, j: (bb, 0, rec_sec * nj + j)),
                  vec(k), vec(1), wsp, vec(2), wsp, vec(2), vec(2)],
        out_specs=[out, out, out, out],
        out_shape=[shp, shp, shp, shp],
        compiler_params=_cparams(2),
        name="lru_gates",
    )(proj, conv_w, conv_b.reshape(1, d), wa_t, ba, wx_t, bx, lam)


def _lru_scan_body(a_ref, b_ref, h0_ref, o_ref, carry_ref, *, reverse):
    tl = a_ref.shape[0]
    groups = tl // 8

    @pl.when(pl.program_id(1) == 0)
    def _():
        carry_ref[0:1, :] = h0_ref[...]

    row = lax.broadcasted_iota(jnp.int32, (8, a_ref.shape[1]), 0)

    def step(g, h):
        base = pl.multiple_of((groups - 1 - g if reverse else g) * 8, 8)
        a = a_ref[pl.ds(base, 8), :]
        b = b_ref[pl.ds(base, 8), :]
        for s in (1, 2, 4):
            has_prev = row < 8 - s if reverse else row >= s
            shift = 8 - s if reverse else s
            b = jnp.where(has_prev, a, 0.0) * pltpu.roll(b, shift, axis=0) + b
            a = a * jnp.where(has_prev, pltpu.roll(a, shift, axis=0), 1.0)
        hs = a * h + b
        o_ref[pl.ds(base, 8), :] = hs
        return hs[0:1, :] if reverse else hs[7:8, :]

    carry_ref[0:1, :] = lax.fori_loop(0, groups, step, carry_ref[0:1, :])


def lru_scan(a, b, h0, reverse):
    bsz, l, d = a.shape
    tl = _tile(l, 512)
    nt = l // tl
    tmap = (lambda bb, i: (bb, nt - 1 - i, 0)) if reverse else (lambda bb, i: (bb, i, 0))
    return pl.pallas_call(
        functools.partial(_lru_scan_body, reverse=reverse),
        grid=(bsz, nt),
        in_specs=[pl.BlockSpec((None, tl, d), tmap),
                  pl.BlockSpec((None, tl, d), tmap),
                  pl.BlockSpec((None, 1, d), lambda bb, i: (bb, 0, 0))],
        out_specs=pl.BlockSpec((None, tl, d), tmap),
        out_shape=jax.ShapeDtypeStruct((bsz, l, d), F32),
        scratch_shapes=[pltpu.VMEM((8, d), F32)],
        compiler_params=_cparams(2),
        name="lru_scan",
    )(a, b, h0.reshape(bsz, 1, d))


def _shortconv_body(bg_ref, cg_ref, xv_ref, w_ref, o_ref):
    prod = cg_ref[...].astype(F32) * xv_ref[...].astype(F32)
    o_ref[...] = (bg_ref[...].astype(F32) * _dwconv(prod, w_ref, 1)).astype(o_ref.dtype)


def shortconv_mixer(proj, first_sec, conv_w):
    b, l, _ = proj.shape
    k, d = conv_w.shape
    tn = _tile(d, 256)
    nj = d // tn
    sec = lambda s: pl.BlockSpec((None, l, tn), lambda bb, j: (bb, 0, (first_sec + s) * nj + j))
    return pl.pallas_call(
        _shortconv_body,
        grid=(b, nj),
        in_specs=[sec(0), sec(1), sec(2), pl.BlockSpec((k, tn), lambda bb, j: (0, j))],
        out_specs=pl.BlockSpec((None, l, tn), lambda bb, j: (bb, 0, j)),
        out_shape=jax.ShapeDtypeStruct((b, l, d), BF16),
        compiler_params=_cparams(2),
        name="shortconv",
    )(proj, proj, proj, conv_w)


MERGE_SPLIT = 2


def _gelu_tanh(x):
    return 0.5 * x * (1.0 + jnp.tanh(math.sqrt(2.0 / math.pi) * (x + 0.044715 * (x * x * x))))


def _merge_body(ya_ref, yc_ref, lg_ref, hf_ref, hb_ref, ga_ref, gb_ref, gc_ref, x_ref,
                mg_ref, g2_ref, sh_ref, sc_ref, wa_ref, wb_ref, wc_ref, wo_ref, wr_ref,
                xo_ref, h2_ref, aff_ref):
    tl = x_ref.shape[0]
    half = tl // MERGE_SPLIT
    wr = wr_ref[...]
    wr_hi = wr.astype(BF16)
    wr_lo = (wr - wr_hi.astype(F32)).astype(BF16)
    for part in range(MERGE_SPLIT):
        rows = pl.ds(part * half, half)
        yb = (_gelu_tanh(lg_ref[rows, :].astype(F32)) * (hf_ref[rows, :] + hb_ref[rows, :])).astype(BF16)
        merged = _sigmoid(ga_ref[rows, :].astype(F32)) * jnp.dot(ya_ref[rows, :], wa_ref[...], preferred_element_type=F32)
        merged += _sigmoid(gb_ref[rows, :].astype(F32)) * jnp.dot(yb, wb_ref[...], preferred_element_type=F32)
        merged += _sigmoid(gc_ref[rows, :].astype(F32)) * jnp.dot(yc_ref[rows, :], wc_ref[...], preferred_element_type=F32)
        out = jnp.dot(merged.astype(BF16), wo_ref[...], preferred_element_type=F32)
        x = x_ref[rows, :] + mg_ref[...] * out
        xo_ref[rows, :] = x
        h2 = _rms_modulate(x, g2_ref[...], sh_ref[...], sc_ref[...])
        h2_hi = h2.astype(BF16)
        h2_ref[rows, :] = h2_hi
        h2_lo = (h2 - h2_hi.astype(F32)).astype(BF16)
        logits = (jnp.dot(h2_hi, wr_hi, preferred_element_type=F32)
                  + jnp.dot(h2_hi, wr_lo, preferred_element_type=F32)
                  + jnp.dot(h2_lo, wr_hi, preferred_element_type=F32))
        e = jnp.exp(logits - jnp.max(logits, axis=-1, keepdims=True))
        aff_ref[rows, :] = e / jnp.sum(e, axis=-1, keepdims=True)


def merge_mixers(proj, gate_sec, lg_sec, ya, yc, hf, hb, x, mod_gate, g2, shift2, scale2,
                 w_a, w_b, w_c, w_o, w_r):
    b, l, d = x.shape
    e = w_r.shape[1]
    tl = _tile(l, 512)
    row = lambda: pl.BlockSpec((None, tl, d), lambda bb, i: (bb, i, 0))
    sec = lambda s: pl.BlockSpec((None, tl, d), lambda bb, i: (bb, i, s))
    modv = lambda: pl.BlockSpec((None, 1, d), lambda bb, i: (bb, 0, 0))
    wsp = lambda: _resident((d, d), lambda bb, i: (0, 0))
    return pl.pallas_call(
        _merge_body,
        grid=(b, l // tl),
        in_specs=[row(), row(), sec(lg_sec), row(), row(),
                  sec(gate_sec), sec(gate_sec + 1), sec(gate_sec + 2), row(),
                  modv(), pl.BlockSpec((1, d), lambda bb, i: (0, 0)), modv(), modv(),
                  wsp(), wsp(), wsp(), wsp(), _resident((d, e), lambda bb, i: (0, 0))],
        out_specs=[row(), row(), pl.BlockSpec((None, tl, e), lambda bb, i: (bb, i, 0))],
        out_shape=[jax.ShapeDtypeStruct((b, l, d), F32),
                   jax.ShapeDtypeStruct((b, l, d), BF16),
                   jax.ShapeDtypeStruct((b, l, e), F32)],
        compiler_params=_cparams(2),
        name="merge",
    )(ya, yc, proj, hf, hb, proj, proj, proj, x, mod_gate, g2.reshape(1, d), shift2, scale2,
      w_a, w_b, w_c, w_o, w_r)


def _select_body(aff_ref, tri_ref, rank_ref, *, cap):
    bits = lax.bitcast_convert_type(aff_ref[...], jnp.int32)

    def refine(i, thr):
        cand = thr | jnp.left_shift(jnp.int32(1), 30 - i)
        cnt = jnp.sum((bits >= cand).astype(F32), axis=1, keepdims=True)
        return jnp.where(cnt >= cap, cand, thr)

    thr = lax.fori_loop(0, 31, refine, jnp.zeros((bits.shape[0], 1), jnp.int32))
    gt = bits > thr
    eq = bits == thr
    n_gt = jnp.sum(gt.astype(F32), axis=1, keepdims=True)
    eq_rank = jnp.dot(eq.astype(BF16), tri_ref[...], preferred_element_type=F32)
    sel = gt | (eq & (eq_rank <= cap - n_gt))
    rank = jnp.dot(sel.astype(BF16), tri_ref[...], preferred_element_type=F32) - 1.0
    rank_ref[...] = jnp.where(sel, rank, -1.0).astype(jnp.int32)


def select_tokens(aff_t, cap):
    r, n = aff_t.shape
    tri = jnp.asarray(_prefix_table(n), BF16)
    return pl.pallas_call(
        functools.partial(_select_body, cap=cap),
        grid=(1,),
        in_specs=[pl.BlockSpec((r, n), lambda i: (0, 0)), pl.BlockSpec((n, n), lambda i: (0, 0))],
        out_specs=pl.BlockSpec((r, n), lambda i: (0, 0)),
        out_shape=jax.ShapeDtypeStruct((r, n), jnp.int32),
        compiler_params=_cparams(1),
        name="select",
    )(aff_t, tri)


GATHER_GROUP = 8


def _gather_body(rank_ref, hs_ref, o_ref):
    grp, cap, d = o_ref.shape
    n = hs_ref.shape[0]
    g0 = pl.multiple_of(pl.program_id(1) * grp, grp)
    ranks = rank_ref[pl.ds(g0, grp), :]
    slot = lax.broadcasted_iota(jnp.int32, (cap, n), 0)
    onehot = jnp.concatenate([jnp.where(slot == ranks[e:e + 1, :], 1.0, 0.0).astype(BF16) for e in range(grp)],
                             axis=0)
    rows = jnp.dot(onehot, hs_ref[...], preferred_element_type=F32)
    o_ref[...] = rows.reshape(grp, cap, d).astype(o_ref.dtype)


def _gather_into_body(rank_ref, hs_ref, prev_ref, o_ref):
    del prev_ref
    _gather_body(rank_ref, hs_ref, o_ref)


def gather_tokens(rank_t, hs, cap, total_rows, row_off, into=None):
    s, e, n = rank_t.shape
    d = hs.shape[2]
    boff = row_off // cap
    in_specs = [pl.BlockSpec((None, e, n), lambda ss, ee: (ss, 0, 0)),
                pl.BlockSpec((None, n, d), lambda ss, ee: (ss, 0, 0))]
    args = [rank_t, hs]
    if into is not None:
        in_specs.append(pl.BlockSpec(memory_space=pl.ANY))
        args.append(into)
    return pl.pallas_call(
        _gather_body if into is None else _gather_into_body,
        grid=(s, e // GATHER_GROUP),
        in_specs=in_specs,
        out_specs=pl.BlockSpec((GATHER_GROUP, cap, d), lambda ss, ee: (ee, boff + ss, 0)),
        out_shape=jax.ShapeDtypeStruct((e, total_rows, d), BF16),
        input_output_aliases={} if into is None else {2: 0},
        compiler_params=_cparams(2),
        name="gather",
    )(*args)


FFN_SPLIT = 2


def _ffn_body(x_ref, wg_ref, wu_ref, wd_ref, o_ref, acc_ref):
    f = pl.program_id(1)

    @pl.when(f == 0)
    def _():
        acc_ref[...] = jnp.zeros_like(acc_ref)

    wg = wg_ref[...].astype(BF16)
    wu = wu_ref[...].astype(BF16)
    wd = wd_ref[...].astype(BF16)
    m = x_ref.shape[0]
    step = m // FFN_SPLIT
    for part in range(FFN_SPLIT):
        rows = pl.ds(part * step, step)
        x = x_ref[rows, :]
        g = jnp.dot(x, wg, preferred_element_type=F32)
        u = jnp.dot(x, wu, preferred_element_type=F32)
        hid = (g * _sigmoid(g) * u).astype(BF16)
        acc_ref[rows, :] += jnp.dot(hid, wd, preferred_element_type=F32)

    @pl.when(f == pl.num_programs(1) - 1)
    def _():
        o_ref[...] = acc_ref[...].astype(o_ref.dtype)


def expert_ffn(xs, wg, wu, wd, layer):
    e, m, d = xs.shape
    fdim = wg.shape[3]
    tf = _tile(fdim, 256)
    return pl.pallas_call(
        _ffn_body,
        grid=(e, fdim // tf),
        in_specs=[pl.BlockSpec((None, m, d), lambda ee, f: (ee, 0, 0)),
                  pl.BlockSpec((None, None, d, tf), lambda ee, f: (layer, ee, 0, f)),
                  pl.BlockSpec((None, None, d, tf), lambda ee, f: (layer, ee, 0, f)),
                  pl.BlockSpec((None, None, tf, d), lambda ee, f: (layer, ee, f, 0))],
        out_specs=pl.BlockSpec((None, m, d), lambda ee, f: (ee, 0, 0)),
        out_shape=jax.ShapeDtypeStruct((e, m, d), BF16),
        scratch_shapes=[pltpu.VMEM((m, d), F32)],
        compiler_params=_cparams(2),
        name="ffn",
    )(xs, wg, wu, wd)


def _scatter_body(rank_ref, aff_ref, ys_ref, x_ref, mg_ref, gf_ref, o_ref, *, final_norm):
    tq = rank_ref.shape[0]
    n_exp, cap, _ = ys_ref.shape
    slot = lax.broadcasted_iota(jnp.int32, (tq, cap), 1)
    rank = rank_ref[...]
    aff = aff_ref[...]
    acc = None
    for e in range(n_exp):
        q = jnp.where(slot == rank[:, e:e + 1], aff[:, e:e + 1], 0.0).astype(BF16)
        part = jnp.dot(q, ys_ref[e], preferred_element_type=F32)
        acc = part if acc is None else acc + part
    x = x_ref[...] + mg_ref[...] * acc
    if final_norm:
        x = x * lax.rsqrt(jnp.mean(x * x, axis=-1, keepdims=True) + NORM_EPS) * gf_ref[...]
    o_ref[...] = x


def scatter_residual(rank, aff, ys, row_off, cap, x, mod_gate, final_g):
    s, n, e = rank.shape
    d = x.shape[2]
    tq = _tile(n, 1024)
    boff = row_off // cap
    final_norm = final_g is not None
    gf = (final_g if final_norm else jnp.ones((d,), F32)).reshape(1, d)
    return pl.pallas_call(
        functools.partial(_scatter_body, final_norm=final_norm),
        grid=(s, n // tq),
        in_specs=[pl.BlockSpec((None, tq, e), lambda ss, i: (ss, i, 0)),
                  pl.BlockSpec((None, tq, e), lambda ss, i: (ss, i, 0)),
                  pl.BlockSpec((e, cap, d), lambda ss, i: (0, boff + ss, 0)),
                  pl.BlockSpec((None, tq, d), lambda ss, i: (ss, i, 0)),
                  pl.BlockSpec((None, 1, d), lambda ss, i: (ss, 0, 0)),
                  pl.BlockSpec((1, d), lambda ss, i: (0, 0))],
        out_specs=pl.BlockSpec((None, tq, d), lambda ss, i: (ss, i, 0)),
        out_shape=jax.ShapeDtypeStruct(x.shape, F32),
        compiler_params=_cparams(2),
        name="scatter",
    )(rank, aff, ys, x, mod_gate, gf)


def _mixer_and_route(x, h0_f, h0_b, mods, lw, fs, shared_mod=False, pos_embed=None):
    b, l, d = x.shape
    if shared_mod:
        flat = lambda a: a.reshape(1, b * l, a.shape[-1])
        unflat = lambda a: a.reshape(b, l, a.shape[-1])
        mods = [m[:1] for m in mods]
    else:
        flat = unflat = lambda a: a
    if pos_embed is None:
        proj = unflat(norm_proj(flat(x), lw['norm1_g'], mods[0], mods[1], lw['w_in']))
    else:
        proj, x = norm_proj(x, lw['norm1_g'], mods[0], mods[1], lw['w_in'], pos_embed)
    t1, t2, mid = hy_filter_tables(l, lw['hy_filt_w1'], lw['hy_filt_b1'], lw['hy_filt_w2'], lw['hy_filt_b2'],
                                  lw['hy_filt_w3'], lw['hy_filt_freq'], fs)
    ya = hyena_mixer(proj, lw['hy_conv_w'], lw['hy_conv_b'], lw['hy_bias'], t1, t2, mid, fs)
    a_f, b_f, a_b, b_b = lru_gates(proj, 4, lw['lru_conv_w'], lw['lru_conv_b'], lw['lru_wa'], lw['lru_ba'],
                                   lw['lru_wx'], lw['lru_bx'], lw['lru_lambda'])
    hf = lru_scan(a_f, b_f, h0_f, False)
    hb = lru_scan(a_b, b_b, h0_b, True)
    yc = shortconv_mixer(proj, 5, lw['sc_conv_w'])
    x_new, h2, aff = merge_mixers(flat(proj), 8, 3, flat(ya), flat(yc), flat(hf), flat(hb), flat(x), mods[2],
                                  lw['norm2_g'], mods[3], mods[4],
                                  lw['w_hy_out'], lw['w_lru_out'], lw['w_sc_out'], lw['w_o'], lw['w_router'])
    return unflat(x_new), unflat(h2), unflat(aff), hf[:, -1], hb[:, 0]


def _capacity(n, e):
    return EC_CAPACITY * n // e


def _route(h2, aff, total_rows, row_off, into):
    s, n, e = aff.shape
    cap = _capacity(n, e)
    aff_t = jnp.transpose(aff, (0, 2, 1))
    rank_t = select_tokens(aff_t.reshape(s * e, n), cap).reshape(s, e, n)
    xs = gather_tokens(rank_t, h2, cap, total_rows, row_off, into)
    return xs, jnp.transpose(rank_t, (0, 2, 1))


def kernel(x, c, ctx, c_ctx, w_ada, b_ada, norm1_g, norm2_g, w_in, hy_conv_w, hy_conv_b, hy_filt_w1, hy_filt_b1, hy_filt_w2, hy_filt_b2, hy_filt_w3, hy_filt_freq, hy_bias, lru_conv_w, lru_conv_b, lru_wa, lru_ba, lru_wx, lru_bx, lru_lambda, sc_conv_w, w_hy_out, w_lru_out, w_sc_out, w_o, w_router, w_exp_gate, w_exp_up, w_exp_down, final_norm_g):
    bsz, n_lat, d = x.shape
    n_ctx = ctx.shape[1]
    depth = w_ada.shape[0]
    rows = n_lat // GRID_W

    pos_embed = jnp.asarray(_pos_embed(rows, d))
    xc = ctx
    fs_lat = jnp.asarray(_dft_table(n_lat), BF16)
    fs_ctx = jnp.asarray(_dft_table(n_ctx), BF16)

    cc = jnp.zeros((2 * bsz, d), F32).at[:bsz].set(c).at[bsz].set(c_ctx)
    mods = ada_mods(cc, w_ada, b_ada)
    zeros_state = jnp.zeros((bsz, d), F32)

    for l in range(depth):
        last = l == depth - 1
        lw = {
            'norm1_g': norm1_g[l], 'norm2_g': norm2_g[l], 'w_in': w_in[l].astype(BF16),
            'hy_conv_w': hy_conv_w[l], 'hy_conv_b': hy_conv_b[l],
            'hy_filt_w1': hy_filt_w1[l], 'hy_filt_b1': hy_filt_b1[l], 'hy_filt_w2': hy_filt_w2[l],
            'hy_filt_b2': hy_filt_b2[l], 'hy_filt_w3': hy_filt_w3[l], 'hy_filt_freq': hy_filt_freq[l],
            'hy_bias': hy_bias[l], 'lru_conv_w': lru_conv_w[l], 'lru_conv_b': lru_conv_b[l],
            'lru_wa': lru_wa[l], 'lru_ba': lru_ba[l], 'lru_wx': lru_wx[l], 'lru_bx': lru_bx[l],
            'lru_lambda': lru_lambda[l], 'sc_conv_w': sc_conv_w[l],
            'w_hy_out': w_hy_out[l].astype(BF16), 'w_lru_out': w_lru_out[l].astype(BF16),
            'w_sc_out': w_sc_out[l].astype(BF16), 'w_o': w_o[l].astype(BF16), 'w_router': w_router[l],
        }
        mod_l = [mods[l, :bsz, k * d:(k + 1) * d].reshape(bsz, 1, d) for k in range(6)]
        mod_c = [jnp.broadcast_to(mods[l, bsz, k * d:(k + 1) * d].reshape(1, 1, d), (bsz, 1, d)) for k in range(6)]

        n_exp = w_router.shape[2]
        cap_l = _capacity(n_lat, n_exp)
        cap_c = _capacity(n_ctx, n_exp)
        rows_l = bsz * cap_l
        if last:
            proj_c = norm_proj(xc.reshape(1, bsz * n_ctx, d), lw['norm1_g'], mod_c[0][:1], mod_c[1][:1],
                               lw['w_in'][:, 4 * d:5 * d]).reshape(bsz, n_ctx, d)
            a_f, b_f, a_b, b_b = lru_gates(proj_c, 0, lw['lru_conv_w'], lw['lru_conv_b'], lw['lru_wa'], lw['lru_ba'],
                                           lw['lru_wx'], lw['lru_bx'], lw['lru_lambda'])
            state_f = lru_scan(a_f, b_f, zeros_state, False)[:, -1]
            state_b = lru_scan(a_b, b_b, zeros_state, True)[:, 0]
            total_rows = rows_l
            xs = None
        else:
            xc_mid, h2_c, aff_c, state_f, state_b = _mixer_and_route(xc, zeros_state, zeros_state, mod_c, lw, fs_ctx,
                                                                     shared_mod=True)
            total_rows = rows_l + bsz * cap_c
            xs, rank_c = _route(h2_c, aff_c, total_rows, rows_l, None)

        x_mid, h2_l, aff_l, _, _ = _mixer_and_route(x, state_f, state_b, mod_l, lw, fs_lat,
                                                    pos_embed=pos_embed if l == 0 else None)
        xs, rank_l = _route(h2_l, aff_l, total_rows, 0, xs)
        ys = expert_ffn(xs, w_exp_gate, w_exp_up, w_exp_down, l)

        if not last:
            xc = scatter_residual(rank_c, aff_c, ys, rows_l, cap_c, xc_mid, mod_c[5], None)
        x = scatter_residual(rank_l, aff_l, ys, 0, cap_l, x_mid, mod_l[5], final_norm_g if last else None)
    return x
```

```python
import functools
import math

import numpy as np
import jax
import jax.numpy as jnp
from jax import lax
from jax.experimental import pallas as pl
from jax.experimental.pallas import tpu as pltpu

F32 = jnp.float32
BF16 = jnp.bfloat16
HI = lax.Precision.HIGHEST

NORM_EPS = 1e-6
GRID_W = 64
N_EXPERTS = 16
EC_CAPACITY = 2
LRU_C = 8.0
HYENA_ORDER = 2
HYENA_EMB_DIM = 33
HYENA_FAST_DECAY = 0.3
HYENA_SLOW_DECAY = 1.5
HYENA_TARGET = 1e-2
MLP_PAD = 128

V7X_VMEM_BYTES = 64 * 1024 * 1024
VMEM_LIMIT = 56 * 1024 * 1024
assert VMEM_LIMIT < V7X_VMEM_BYTES


def _cparams(n_grid):
    return pltpu.CompilerParams(dimension_semantics=("arbitrary",) * n_grid,
                                vmem_limit_bytes=VMEM_LIMIT)


def _tile(n, pref):
    if n <= pref:
        return n
    t = pref
    while n % t:
        t //= 2
    return t


def _resident(block_shape, index_map):
    return pl.BlockSpec(block_shape, index_map, pipeline_mode=pl.Buffered(1))


@functools.lru_cache(maxsize=None)
def _pos_embed(rows, d):
    def sincos(pos, dim):
        half = dim // 2
        omega = 1.0 / (10000.0 ** (np.arange(half, dtype=np.float64) / half))
        ang = pos[:, None] * omega[None, :]
        return np.concatenate([np.sin(ang), np.cos(ang)], axis=-1)
    half = d // 2
    er = sincos(np.arange(rows, dtype=np.float64), half)
    ec = sincos(np.arange(GRID_W, dtype=np.float64), half)
    emb = np.concatenate([np.broadcast_to(er[:, None, :], (rows, GRID_W, half)),
                          np.broadcast_to(ec[None, :, :], (rows, GRID_W, half))], axis=-1)
    return emb.reshape(rows * GRID_W, d).astype(np.float32)


@functools.lru_cache(maxsize=None)
def _dft_table(L):
    h = L // 2
    k = np.arange(h, dtype=np.int64)
    ang_e = np.pi * ((k[:, None] * (2 * k[None, :])) % (2 * L)).astype(np.float64) / L
    ang_o = np.pi * ((k[:, None] * (2 * k[None, :] + 1)) % (2 * L)).astype(np.float64) / L
    ce, se, co, so = np.cos(ang_e), np.sin(ang_e), np.cos(ang_o), np.sin(ang_o)
    return np.concatenate([ce, se, co, so, co.T, so.T], axis=0).astype(np.float32)


@functools.lru_cache(maxsize=None)
def _filter_feats(L):
    bands = (HYENA_EMB_DIM - 1) // 2
    t01 = np.linspace(0.0, 1.0, L, dtype=np.float32).astype(np.float64)[:, None]
    w = ((2.0 * math.pi / L) * np.arange(L, dtype=np.float32))[:, None].astype(np.float64)
    f = np.linspace(1e-4, bands - 1, bands, dtype=np.float32).astype(np.float64)[None, :]
    feats = np.concatenate([t01, np.cos(f * w), -np.sin(f * w)], axis=-1)
    out = np.zeros((L, MLP_PAD), np.float32)
    out[:, :HYENA_EMB_DIM] = feats
    return out


@functools.lru_cache(maxsize=None)
def _decay_window(L, d):
    max_decay = math.log(HYENA_TARGET) / HYENA_FAST_DECAY
    min_decay = math.log(HYENA_TARGET) / HYENA_SLOW_DECAY
    t01 = np.linspace(0.0, 1.0, L, dtype=np.float32).astype(np.float64)[:, None]
    deltas = np.linspace(min_decay, max_decay, d, dtype=np.float32).astype(np.float64)
    return np.exp(-t01 * np.abs(deltas)[None, :]).astype(np.float32)


@functools.lru_cache(maxsize=None)
def _prefix_table(n):
    return np.triu(np.ones((n, n), np.float32))


def _shift_rows(x, s):
    if s == 0:
        return x
    n = x.shape[0]
    rolled = pltpu.roll(x, s % n, axis=0)
    row = lax.broadcasted_iota(jnp.int32, x.shape, 0)
    keep = row >= s if s > 0 else row < n + s
    return jnp.where(keep, rolled, 0.0)


def _dwconv(x, w_ref, left):
    acc = None
    for k in range(w_ref.shape[0]):
        term = _shift_rows(x, left - k) * w_ref[k:k + 1, :]
        acc = term if acc is None else acc + term
    return acc


def _sigmoid(x):
    return 0.5 * jnp.tanh(0.5 * x) + 0.5


def _rms_modulate(x, g, shift, scale):
    y = x * lax.rsqrt(jnp.mean(x * x, axis=-1, keepdims=True) + NORM_EPS)
    return (y * g) * (1.0 + scale) + shift


def _ada_body(c_ref, w_ref, b_ref, o_ref):
    c = c_ref[...]
    s = c * _sigmoid(c)
    o_ref[...] = jnp.dot(s, w_ref[...], precision=HI, preferred_element_type=F32) + b_ref[...]


def ada_mods(cc, w_ada, b_ada):
    depth, d, n = w_ada.shape
    r = cc.shape[0]
    tn = _tile(n, 1536)
    return pl.pallas_call(
        _ada_body,
        grid=(depth, n // tn),
        in_specs=[pl.BlockSpec((r, d), lambda l, j: (0, 0)),
                  pl.BlockSpec((None, d, tn), lambda l, j: (l, 0, j)),
                  pl.BlockSpec((None, 1, tn), lambda l, j: (l, 0, j))],
        out_specs=pl.BlockSpec((None, r, tn), lambda l, j: (l, 0, j)),
        out_shape=jax.ShapeDtypeStruct((depth, r, n), F32),
        compiler_params=_cparams(2),
        name="ada",
    )(cc, w_ada, b_ada.reshape(depth, 1, n))


def _norm_proj_body(x_ref, g_ref, sh_ref, sc_ref, w_ref, o_ref, h_ref):
    @pl.when(pl.program_id(2) == 0)
    def _():
        h_ref[...] = _rms_modulate(x_ref[...], g_ref[...], sh_ref[...], sc_ref[...]).astype(BF16)

    o_ref[...] = jnp.dot(h_ref[...], w_ref[...], preferred_element_type=F32).astype(o_ref.dtype)


def _embed_norm_proj_body(x_ref, pe_ref, g_ref, sh_ref, sc_ref, w_ref, o_ref, h_ref):
    @pl.when(pl.program_id(2) == 0)
    def _():
        h_ref[...] = _rms_modulate(x_ref[...] + pe_ref[...], g_ref[...], sh_ref[...], sc_ref[...]).astype(BF16)

    o_ref[...] = jnp.dot(h_ref[...], w_ref[...], preferred_element_type=F32).astype(o_ref.dtype)


PROJ_COLS = 2816


def norm_proj(x, g, shift, scale, w, pos_embed=None):
    b, l, d = x.shape
    n = w.shape[1]
    tl = _tile(l, 1024)
    tn = _tile(n, PROJ_COLS)
    row = pl.BlockSpec((None, tl, d), lambda bb, i, j: (bb, i, 0))
    tail = [pl.BlockSpec((1, d), lambda bb, i, j: (0, 0)),
            pl.BlockSpec((None, 1, d), lambda bb, i, j: (bb, 0, 0)),
            pl.BlockSpec((None, 1, d), lambda bb, i, j: (bb, 0, 0)),
            pl.BlockSpec((d, tn), lambda bb, i, j: (0, j))]
    proj_spec = pl.BlockSpec((None, tl, tn), lambda bb, i, j: (bb, i, j))
    proj_shape = jax.ShapeDtypeStruct((b, l, n), BF16)
    common = dict(grid=(b, l // tl, n // tn), scratch_shapes=[pltpu.VMEM((tl, d), BF16)],
                  compiler_params=_cparams(3), name="norm_proj")
    if pos_embed is None:
        return pl.pallas_call(_norm_proj_body, in_specs=[row] + tail, out_specs=proj_spec, out_shape=proj_shape,
                              **common)(x, g.reshape(1, d), shift, scale, w)
    return pl.pallas_call(
        _embed_norm_proj_body,
        in_specs=[row, pl.BlockSpec((tl, d), lambda bb, i, j: (i, 0))] + tail,
        out_specs=proj_spec, out_shape=proj_shape,
        **common)(x, pos_embed, g.reshape(1, d), shift, scale, w)


def _hy_hidden_body(feats_ref, w1_ref, b1_ref, w2_ref, b2_ref, fr_ref, h_ref):
    fr = fr_ref[...]
    h = jnp.sin(fr * (jnp.dot(feats_ref[...], w1_ref[...], precision=HI,
                              preferred_element_type=F32) + b1_ref[...]))
    h_ref[...] = jnp.sin(fr * (jnp.dot(h, w2_ref[...], precision=HI,
                                       preferred_element_type=F32) + b2_ref[...]))


LANES = 128


def _sign_col(rows):
    row = lax.broadcasted_iota(jnp.int32, (rows, 1), 0)
    return (1 - 2 * (row & 1)).astype(F32)


def _put_slabs(nat_ref, r0, rows):
    for s in range(nat_ref.shape[0]):
        nat_ref[s, pl.ds(r0, rows.shape[0]), :] = rows[:, s * LANES:(s + 1) * LANES]


def _get_slabs(nat_ref, r0, n):
    return jnp.concatenate([nat_ref[s, pl.ds(r0, n), :] for s in range(nat_ref.shape[0])], axis=1)


def _split_rows(nat_ref, e_ref, o_ref):
    nslab, L, _ = nat_ref.shape
    h = L // 2
    rd = min(h, 256)
    sgn = _sign_col(rd)

    def body(i, carry):
        m0 = pl.multiple_of(i * rd, rd)
        e = jnp.concatenate([nat_ref[s, pl.ds(2 * m0, rd, stride=2), :] for s in range(nslab)], axis=1)
        o = jnp.concatenate([nat_ref[s, pl.ds(2 * m0 + 1, rd, stride=2), :] for s in range(nslab)], axis=1)
        e_ref[pl.ds(m0, rd), :] = e.astype(BF16)
        o_ref[pl.ds(m0, rd), :] = o.astype(BF16)
        return (carry[0] + jnp.sum(e * sgn, axis=0, keepdims=True),
                carry[1] + jnp.sum(o * sgn, axis=0, keepdims=True))

    zero = jnp.zeros((1, nslab * LANES), F32)
    return lax.fori_loop(0, h // rd, body, (zero, zero))


def _hy_filter_body(h_ref, w3f_ref, w3b_ref, decay_ref, tb_ref, t1_ref, t2_ref, mid_ref,
                    p_ref, q_ref, pe_ref, po_ref, qe_ref, qo_ref):
    L = h_ref.shape[0]
    hl = L // 2
    hid = h_ref[...]
    decay = decay_ref[...]
    fwd = jnp.dot(hid, w3f_ref[...], precision=HI, preferred_element_type=F32) * decay
    bwd = jnp.dot(hid, w3b_ref[...], precision=HI, preferred_element_type=F32) * decay
    row = lax.broadcasted_iota(jnp.int32, fwd.shape, 0)
    bwd = jnp.where(row >= 1, bwd, 0.0)
    inv = 1.0 / (jnp.sum(jnp.abs(fwd), axis=0, keepdims=True)
                 + jnp.sum(jnp.abs(bwd), axis=0, keepdims=True))
    _put_slabs(p_ref, 0, fwd + bwd)
    _put_slabs(q_ref, 0, bwd - fwd)
    k_re_mid, _ = _split_rows(p_ref, pe_ref, po_ref)
    _, k_im_mid = _split_rows(q_ref, qe_ref, qo_ref)
    a_e = jnp.dot(tb_ref[0:hl, :], pe_ref[...], preferred_element_type=F32)
    a_o = jnp.dot(tb_ref[2 * hl:3 * hl, :], po_ref[...], preferred_element_type=F32)
    b_e = jnp.dot(tb_ref[hl:2 * hl, :], qe_ref[...], preferred_element_type=F32)
    b_o = jnp.dot(tb_ref[3 * hl:4 * hl, :], qo_ref[...], preferred_element_type=F32)
    n_fft = 2.0 * L
    row_h = lax.broadcasted_iota(jnp.int32, (hl, fwd.shape[1]), 0)
    w_k = jnp.where(row_h >= 1, 2.0 / n_fft, 1.0 / n_fft) * inv
    w_s = (2.0 / n_fft) * inv
    t1_ref[0:hl, :] = (a_e + a_o) * w_k
    t1_ref[hl:L, :] = (a_e - a_o) * w_k
    t2_ref[0:hl, :] = (b_e + b_o) * w_s
    t2_ref[hl:L, :] = (b_o - b_e) * w_s
    mid_ref[0:1, :] = k_re_mid * w_s
    mid_ref[1:2, :] = k_im_mid * w_s


def hy_filter_tables(L, w1, b1, w2, b2, w3, freq, fs):
    d = w3.shape[1] // (HYENA_ORDER * 2)
    fw = w1.shape[1]
    pad = lambda a, r, c: jnp.zeros((r, c), F32).at[:a.shape[0], :a.shape[1]].set(a)
    w1p = pad(w1, MLP_PAD, MLP_PAD)
    w2p = pad(w2, MLP_PAD, MLP_PAD)
    w3p = pad(w3, MLP_PAD, w3.shape[1])
    b1p = pad(b1.reshape(1, fw), 1, MLP_PAD)
    b2p = pad(b2.reshape(1, fw), 1, MLP_PAD)
    frp = pad(freq.reshape(1, fw), 1, MLP_PAD)
    feats = jnp.asarray(_filter_feats(L))
    decay = jnp.asarray(_decay_window(L, d))
    tn = _tile(d, 256)
    nj = d // tn
    full = lambda o, j: (0, 0)
    one = lambda i: (0, 0)
    hidden = pl.pallas_call(
        _hy_hidden_body,
        grid=(1,),
        in_specs=[pl.BlockSpec((L, MLP_PAD), one),
                  pl.BlockSpec((MLP_PAD, MLP_PAD), one),
                  pl.BlockSpec((1, MLP_PAD), one),
                  pl.BlockSpec((MLP_PAD, MLP_PAD), one),
                  pl.BlockSpec((1, MLP_PAD), one),
                  pl.BlockSpec((1, MLP_PAD), one)],
        out_specs=pl.BlockSpec((L, MLP_PAD), one),
        out_shape=jax.ShapeDtypeStruct((L, MLP_PAD), F32),
        compiler_params=_cparams(1),
        name="hy_hidden",
    )(feats, w1p, b1p, w2p, b2p, frp)
    hl = L // 2
    t1, t2, mid = pl.pallas_call(
        _hy_filter_body,
        grid=(HYENA_ORDER, nj),
        in_specs=[pl.BlockSpec((L, MLP_PAD), full),
                  pl.BlockSpec((MLP_PAD, tn), lambda o, j: (0, o * 2 * nj + j)),
                  pl.BlockSpec((MLP_PAD, tn), lambda o, j: (0, o * 2 * nj + nj + j)),
                  pl.BlockSpec((L, tn), lambda o, j: (0, j)),
                  _resident(fs.shape, full)],
        out_specs=[pl.BlockSpec((L, tn), lambda o, j: (0, o * nj + j)),
                   pl.BlockSpec((L, tn), lambda o, j: (0, o * nj + j)),
                   pl.BlockSpec((2, tn), lambda o, j: (0, o * nj + j))],
        out_shape=[jax.ShapeDtypeStruct((L, HYENA_ORDER * d), F32),
                   jax.ShapeDtypeStruct((L, HYENA_ORDER * d), F32),
                   jax.ShapeDtypeStruct((2, HYENA_ORDER * d), F32)],
        scratch_shapes=[pltpu.VMEM((tn // LANES, L, LANES), F32), pltpu.VMEM((tn // LANES, L, LANES), F32),
                        pltpu.VMEM((hl, tn), BF16), pltpu.VMEM((hl, tn), BF16),
                        pltpu.VMEM((hl, tn), BF16), pltpu.VMEM((hl, tn), BF16)],
        compiler_params=_cparams(2),
        name="hy_filter",
    )(hidden, w3p, w3p, decay, fs)
    return t1, t2, mid


HALO = 8


def _chunks(n_rows, chunk, fn):
    def body(c, carry):
        fn(pl.multiple_of(c * chunk, chunk))
        return carry
    lax.fori_loop(0, n_rows // chunk, body, 0)


def _conv_rows(src_ref, w_ref, b_ref, left, pad_ref, emit):
    L = src_ref.shape[0]
    rc = min(L, 256)
    win = rc + 2 * HALO
    zeros = jnp.zeros((HALO, pad_ref.shape[1]), F32)
    pad_ref[0:HALO, :] = zeros
    pad_ref[L + HALO:L + 2 * HALO, :] = zeros

    def fill(r0):
        pad_ref[pl.ds(r0 + HALO, rc), :] = src_ref[pl.ds(r0, rc), :].astype(F32)
    _chunks(L, rc, fill)

    def conv(r0):
        w = pad_ref[pl.ds(r0, win), :]
        acc = None
        for k in range(w_ref.shape[0]):
            s = left - k
            term = (w if s == 0 else pltpu.roll(w, s % win, axis=0)) * w_ref[k:k + 1, :]
            acc = term if acc is None else acc + term
        out = acc[HALO:HALO + rc]
        emit(r0, out if b_ref is None else out + b_ref[...])
    _chunks(L, rc, conv)


def _long_conv_rows(u_ref, t1_ref, t2_ref, mid_ref, skip, tb_ref, ue_ref, uo_ref, spec_e_ref, spec_o_ref,
                    xec_ref, xes_ref, xoc_ref, xos_ref, c_ref, emit):
    nslab, L, _ = u_ref.shape
    h = L // 2
    rm = min(h, 64)
    ri = min(h, 1024)
    rc = min(L, 256)

    ua_mid, ub_mid = _split_rows(u_ref, ue_ref, uo_ref)
    spec_e_ref[...] = jnp.dot(tb_ref[0:2 * h, :], ue_ref[...], preferred_element_type=F32)
    spec_o_ref[...] = jnp.dot(tb_ref[2 * h:4 * h, :], uo_ref[...], preferred_element_type=F32)

    def mix(r0):
        ae = spec_e_ref[pl.ds(r0, rm), :]
        be = spec_e_ref[pl.ds(h + r0, rm), :]
        ao = spec_o_ref[pl.ds(r0, rm), :]
        bo = spec_o_ref[pl.ds(h + r0, rm), :]
        p0, p1, q0, q1 = ae + ao, ae - ao, be + bo, bo - be
        t1_lo = t1_ref[pl.ds(r0, rm), :]
        t2_lo = t2_ref[pl.ds(r0, rm), :]
        t1_hi = t1_ref[pl.ds(h + r0, rm), :]
        t2_hi = t2_ref[pl.ds(h + r0, rm), :]
        za0 = p0 * t1_lo + q0 * t2_lo
        zb0 = q0 * t1_lo - p0 * t2_lo
        za1 = p1 * t1_hi + q1 * t2_hi
        zb1 = q1 * t1_hi - p1 * t2_hi
        xec_ref[pl.ds(r0, rm), :] = (za0 + za1).astype(BF16)
        xes_ref[pl.ds(r0, rm), :] = (zb0 - zb1).astype(BF16)
        xoc_ref[pl.ds(r0, rm), :] = (za0 - za1).astype(BF16)
        xos_ref[pl.ds(r0, rm), :] = (zb0 + zb1).astype(BF16)
    _chunks(h, rm, mix)

    t1_mid = mid_ref[0:1, :]
    t2_mid = mid_ref[1:2, :]
    za_mid = ua_mid * t1_mid + ub_mid * t2_mid
    zb_mid = ub_mid * t1_mid - ua_mid * t2_mid
    sgn = _sign_col(ri)

    def inverse(r0):
        ye = jnp.dot(tb_ref[pl.ds(r0, ri), :], xec_ref[...], preferred_element_type=F32)
        ye = ye + jnp.dot(tb_ref[pl.ds(h + r0, ri), :], xes_ref[...], preferred_element_type=F32)
        yo = jnp.dot(tb_ref[pl.ds(4 * h + r0, ri), :], xoc_ref[...], preferred_element_type=F32)
        yo = yo + jnp.dot(tb_ref[pl.ds(5 * h + r0, ri), :], xos_ref[...], preferred_element_type=F32)
        ye = ye + sgn * za_mid
        yo = yo + sgn * zb_mid
        for s in range(nslab):
            c_ref[s, pl.ds(2 * r0, ri, stride=2), :] = ye[:, s * LANES:(s + 1) * LANES]
            c_ref[s, pl.ds(2 * r0 + 1, ri, stride=2), :] = yo[:, s * LANES:(s + 1) * LANES]
    _chunks(h, ri, inverse)

    def finish(r0):
        emit(r0, _get_slabs(c_ref, r0, rc) + _get_slabs(u_ref, r0, rc) * skip)
    _chunks(L, rc, finish)


def _hyena_body(p1_ref, p2_ref, pv_ref, cw1_ref, cw2_ref, cwv_ref, cb1_ref, cb2_ref, cbv_ref,
                bias_ref, t1a_ref, t2a_ref, mida_ref, t1b_ref, t2b_ref, midb_ref, tb_ref, o_ref,
                pad_ref, u_ref, x1_ref, x2_ref, ue_ref, uo_ref, spec_e_ref, spec_o_ref,
                xec_ref, xes_ref, xoc_ref, xos_ref, c_ref):
    def put(ref):
        def emit(r0, rows):
            ref[pl.ds(r0, rows.shape[0]), :] = rows
        return emit

    def put_u(r0, rows):
        _put_slabs(u_ref, r0, rows)

    _conv_rows(p1_ref, cw1_ref, cb1_ref, 1, pad_ref, put(x1_ref))
    _conv_rows(p2_ref, cw2_ref, cb2_ref, 1, pad_ref, put(x2_ref))
    _conv_rows(pv_ref, cwv_ref, cbv_ref, 1, pad_ref, put_u)
    work = (tb_ref, ue_ref, uo_ref, spec_e_ref, spec_o_ref, xec_ref, xes_ref, xoc_ref, xos_ref, c_ref)

    def first(r0, c1):
        put_u(r0, x1_ref[pl.ds(r0, c1.shape[0]), :] * c1)
    _long_conv_rows(u_ref, t1a_ref, t2a_ref, mida_ref, bias_ref[0:1, :], *work, first)

    def second(r0, c2):
        o_ref[pl.ds(r0, c2.shape[0]), :] = (x2_ref[pl.ds(r0, c2.shape[0]), :] * c2).astype(o_ref.dtype)
    _long_conv_rows(u_ref, t1b_ref, t2b_ref, midb_ref, bias_ref[1:2, :], *work, second)


def hyena_mixer(proj, conv_w, conv_b, bias, t1, t2, mid, fs):
    b, l, _ = proj.shape
    d = bias.shape[1]
    tn = _tile(d, 256)
    nj = d // tn
    k = conv_w.shape[0]
    hl = l // 2
    cb = conv_b.reshape(1, 3 * d)
    sec = lambda s: pl.BlockSpec((None, l, tn), lambda j, bb: (bb, 0, s * nj + j))
    cws = lambda s: pl.BlockSpec((k, tn), lambda j, bb: (0, s * nj + j))
    cbs = lambda s: pl.BlockSpec((1, tn), lambda j, bb: (0, s * nj + j))
    tab = lambda o, r: _resident((r, tn), lambda j, bb: (0, o * nj + j))
    half_bf16 = pltpu.VMEM((hl, tn), BF16)
    natural = pltpu.VMEM((tn // LANES, l, LANES), F32)
    return pl.pallas_call(
        _hyena_body,
        grid=(nj, b),
        in_specs=[sec(0), sec(1), sec(2), cws(0), cws(1), cws(2), cbs(0), cbs(1), cbs(2),
                  pl.BlockSpec((HYENA_ORDER, tn), lambda j, bb: (0, j)),
                  tab(0, l), tab(0, l), tab(0, 2), tab(1, l), tab(1, l), tab(1, 2),
                  _resident(fs.shape, lambda j, bb: (0, 0))],
        out_specs=pl.BlockSpec((None, l, tn), lambda j, bb: (bb, 0, j)),
        out_shape=jax.ShapeDtypeStruct((b, l, d), BF16),
        scratch_shapes=[pltpu.VMEM((l + 2 * HALO, tn), F32),
                        natural,
                        pltpu.VMEM((l, tn), F32), pltpu.VMEM((l, tn), F32),
                        half_bf16, half_bf16,
                        pltpu.VMEM((l, tn), F32), pltpu.VMEM((l, tn), F32),
                        half_bf16, half_bf16, half_bf16, half_bf16,
                        natural],
        compiler_params=_cparams(2),
        name="hyena",
    )(proj, proj, proj, conv_w, conv_w, conv_w, cb, cb, cb, bias, t1, t2, mid, t1, t2, mid, fs)


SQRT_GUARD = 1e-30
LRU_GATE_TILE = 256
LRU_LANES = 512


def _lru_gates(rec_ref, cw_ref, cb_ref, wa_ref, ba_ref, wx_ref, bx_ref, lam_ref, outs):
    tg = wa_ref.shape[-1]
    for t in range(rec_ref.shape[1] // tg):
        cols = slice(t * tg, (t + 1) * tg)
        xc = _dwconv(rec_ref[:, cols].astype(F32), cw_ref.at[:, cols], 2) + cb_ref[:, cols]
        xcb = xc.astype(BF16)
        xh = 0.5 * xc
        for dr, (a_out, b_out) in enumerate(outs):
            tr = jnp.tanh(jnp.dot(xcb, wa_ref[dr, t], preferred_element_type=F32) + 0.5 * ba_ref[dr:dr + 1, cols])
            ti = jnp.tanh(jnp.dot(xcb, wx_ref[dr, t], preferred_element_type=F32) + 0.5 * bx_ref[dr:dr + 1, cols])
            ch = (-0.5 * LRU_C * math.log2(math.e)) * jnp.log1p(jnp.exp(-lam_ref[dr:dr + 1, cols]))
            a = jnp.exp2(tr * ch + ch)
            a_out[:, cols] = a
            om = 1.0 - a * a
            b_out[:, cols] = (om * lax.rsqrt(jnp.maximum(om, SQRT_GUARD))) * (ti * xh + xh)


def _scan8(a, b, h, row, reverse):
    for s in (1, 2, 4):
        has_prev = row < 8 - s if reverse else row >= s
        shift = 8 - s if reverse else s
        b = jnp.where(has_prev, a, 0.0) * pltpu.roll(b, shift, axis=0) + b
        a = a * jnp.where(has_prev, pltpu.roll(a, shift, axis=0), 1.0)
    hs = a * h + b
    return hs, (hs[0:1, :] if reverse else hs[7:8, :])


def _lru_body(rec_ref, cw_ref, cb_ref, wa_ref, ba_ref, wx_ref, bx_ref, lam_ref, h0f_ref, h0b_ref,
              hf_ref, hb_ref, af_ref, ab_ref):
    _lru_gates(rec_ref, cw_ref, cb_ref, wa_ref, ba_ref, wx_ref, bx_ref, lam_ref,
               ((af_ref, hf_ref), (ab_ref, hb_ref)))
    groups = rec_ref.shape[0] // 8
    row = lax.broadcasted_iota(jnp.int32, (8, hf_ref.shape[1]), 0)

    def step(g, carry):
        rf = pl.multiple_of(g * 8, 8)
        rb = pl.multiple_of((groups - 1 - g) * 8, 8)
        hf, cf = _scan8(af_ref[pl.ds(rf, 8), :], hf_ref[pl.ds(rf, 8), :], carry[0], row, False)
        hb, cb = _scan8(ab_ref[pl.ds(rb, 8), :], hb_ref[pl.ds(rb, 8), :], carry[1], row, True)
        hf_ref[pl.ds(rf, 8), :] = hf
        hb_ref[pl.ds(rb, 8), :] = hb
        return cf, cb

    lax.fori_loop(0, groups, step, (h0f_ref[...], h0b_ref[...]))


def _block_diag_tiles(w, tn):
    two, h, bs, _ = w.shape
    hpt = tn // bs
    wt = w.reshape(two, h // hpt, hpt, bs, bs)
    eye = jnp.eye(hpt, dtype=w.dtype)
    dense = jnp.einsum('tnhij,hg->tnhigj', wt, eye)
    return dense.reshape(two, h // hpt, tn, tn)


def lru_mixer(proj, rec_sec, conv_w, conv_b, wa, ba, wx, bx, lam, h0_f, h0_b):
    b, l, n = proj.shape
    d = conv_w.shape[1]
    tg = _tile(d, LRU_GATE_TILE)
    tn = _tile(d, LRU_LANES)
    nj = d // tn
    tpj = tn // tg
    k = conv_w.shape[0]
    wa_t = (0.5 * _block_diag_tiles(wa, tg)).astype(BF16)
    wx_t = (0.5 * _block_diag_tiles(wx, tg)).astype(BF16)
    vec = lambda r: pl.BlockSpec((r, tn), lambda bb, j: (0, j))
    wsp = pl.BlockSpec((2, tpj, tg, tg), lambda bb, j: (0, j, 0, 0))
    h0s = pl.BlockSpec((None, 1, tn), lambda bb, j: (bb, 0, j))
    out = pl.BlockSpec((None, l, tn), lambda bb, j: (bb, 0, j))
    shp = jax.ShapeDtypeStruct((b, l, d), F32)
    return pl.pallas_call(
        _lru_body,
        grid=(b, nj),
        in_specs=[pl.BlockSpec((None, l, tn), lambda bb, j: (bb, 0, rec_sec * nj + j)),
                  vec(k), vec(1), wsp, vec(2), wsp, vec(2), vec(2), h0s, h0s],
        out_specs=[out, out],
        out_shape=[shp, shp],
        scratch_shapes=[pltpu.VMEM((l, tn), F32), pltpu.VMEM((l, tn), F32)],
        compiler_params=_cparams(2),
        name="lru",
    )(proj, conv_w, conv_b.reshape(1, d), wa_t, ba, wx_t, bx, lam, h0_f.reshape(b, 1, d), h0_b.reshape(b, 1, d))


def _shortconv_body(bg_ref, cg_ref, xv_ref, w_ref, o_ref):
    prod = cg_ref[...].astype(F32) * xv_ref[...].astype(F32)
    o_ref[...] = (bg_ref[...].astype(F32) * _dwconv(prod, w_ref, 1)).astype(o_ref.dtype)


def shortconv_mixer(proj, first_sec, conv_w):
    b, l, _ = proj.shape
    k, d = conv_w.shape
    tn = _tile(d, 256)
    nj = d // tn
    sec = lambda s: pl.BlockSpec((None, l, tn), lambda bb, j: (bb, 0, (first_sec + s) * nj + j))
    return pl.pallas_call(
        _shortconv_body,
        grid=(b, nj),
        in_specs=[sec(0), sec(1), sec(2), pl.BlockSpec((k, tn), lambda bb, j: (0, j))],
        out_specs=pl.BlockSpec((None, l, tn), lambda bb, j: (bb, 0, j)),
        out_shape=jax.ShapeDtypeStruct((b, l, d), BF16),
        compiler_params=_cparams(2),
        name="shortconv",
    )(proj, proj, proj, conv_w)


MERGE_SPLIT = 2


def _gelu_tanh(x):
    return 0.5 * x * (1.0 + jnp.tanh(math.sqrt(2.0 / math.pi) * (x + 0.044715 * (x * x * x))))


def _merge_body(*refs, has_pos_embed):
    (ya_ref, yc_ref, lg_ref, hf_ref, hb_ref, ga_ref, gb_ref, gc_ref, x_ref), refs = refs[:9], refs[9:]
    if has_pos_embed:
        pe_ref, refs = refs[0], refs[1:]
    (mg_ref, g2_ref, sh_ref, sc_ref, wa_ref, wb_ref, wc_ref, wo_ref, wr_ref, xo_ref, h2_ref, aff_ref) = refs
    tl = x_ref.shape[0]
    half = tl // MERGE_SPLIT
    wr = wr_ref[...]
    wr_hi = wr.astype(BF16)
    wr_lo = (wr - wr_hi.astype(F32)).astype(BF16)
    for part in range(MERGE_SPLIT):
        rows = pl.ds(part * half, half)
        yb = (_gelu_tanh(lg_ref[rows, :].astype(F32)) * (hf_ref[rows, :] + hb_ref[rows, :])).astype(BF16)
        merged = _sigmoid(ga_ref[rows, :].astype(F32)) * jnp.dot(ya_ref[rows, :], wa_ref[...], preferred_element_type=F32)
        merged += _sigmoid(gb_ref[rows, :].astype(F32)) * jnp.dot(yb, wb_ref[...], preferred_element_type=F32)
        merged += _sigmoid(gc_ref[rows, :].astype(F32)) * jnp.dot(yc_ref[rows, :], wc_ref[...], preferred_element_type=F32)
        out = jnp.dot(merged.astype(BF16), wo_ref[...], preferred_element_type=F32)
        x_in = x_ref[rows, :] + pe_ref[rows, :] if has_pos_embed else x_ref[rows, :]
        x = x_in + mg_ref[...] * out
        xo_ref[rows, :] = x
        h2 = _rms_modulate(x, g2_ref[...], sh_ref[...], sc_ref[...])
        h2_hi = h2.astype(BF16)
        h2_ref[rows, :] = h2_hi
        h2_lo = (h2 - h2_hi.astype(F32)).astype(BF16)
        logits = (jnp.dot(h2_hi, wr_hi, preferred_element_type=F32)
                  + jnp.dot(h2_hi, wr_lo, preferred_element_type=F32)
                  + jnp.dot(h2_lo, wr_hi, preferred_element_type=F32))
        e = jnp.exp(logits - jnp.max(logits, axis=-1, keepdims=True))
        aff_ref[rows, :] = e / jnp.sum(e, axis=-1, keepdims=True)


def merge_mixers(proj, gate_sec, lg_sec, ya, yc, hf, hb, x, mod_gate, g2, shift2, scale2,
                 w_a, w_b, w_c, w_o, w_r, pos_embed=None):
    b, l, d = x.shape
    e = w_r.shape[1]
    tl = _tile(l, 512)
    row = lambda: pl.BlockSpec((None, tl, d), lambda bb, i: (bb, i, 0))
    sec = lambda s: pl.BlockSpec((None, tl, d), lambda bb, i: (bb, i, s))
    modv = lambda: pl.BlockSpec((None, 1, d), lambda bb, i: (bb, 0, 0))
    wsp = lambda: _resident((d, d), lambda bb, i: (0, 0))
    has_pos = pos_embed is not None
    pos_spec = [pl.BlockSpec((tl, d), lambda bb, i: (i, 0))] if has_pos else []
    pos_arg = [pos_embed] if has_pos else []
    return pl.pallas_call(
        functools.partial(_merge_body, has_pos_embed=has_pos),
        grid=(b, l // tl),
        in_specs=[row(), row(), sec(lg_sec), row(), row(),
                  sec(gate_sec), sec(gate_sec + 1), sec(gate_sec + 2), row()] + pos_spec
        + [modv(), pl.BlockSpec((1, d), lambda bb, i: (0, 0)), modv(), modv(),
           wsp(), wsp(), wsp(), wsp(), _resident((d, e), lambda bb, i: (0, 0))],
        out_specs=[row(), row(), pl.BlockSpec((None, tl, e), lambda bb, i: (bb, i, 0))],
        out_shape=[jax.ShapeDtypeStruct((b, l, d), F32),
                   jax.ShapeDtypeStruct((b, l, d), BF16),
                   jax.ShapeDtypeStruct((b, l, e), F32)],
        compiler_params=_cparams(2),
        name="merge",
    )(ya, yc, proj, hf, hb, proj, proj, proj, x, *pos_arg, mod_gate, g2.reshape(1, d), shift2, scale2,
      w_a, w_b, w_c, w_o, w_r)


def _select_body(aff_ref, tri_ref, rank_ref, *, cap):
    bits = lax.bitcast_convert_type(aff_ref[...], jnp.int32)

    def refine(i, thr):
        cand = thr | jnp.left_shift(jnp.int32(1), 30 - i)
        cnt = jnp.sum((bits >= cand).astype(F32), axis=1, keepdims=True)
        return jnp.where(cnt >= cap, cand, thr)

    thr = lax.fori_loop(0, 31, refine, jnp.zeros((bits.shape[0], 1), jnp.int32))
    gt = bits > thr
    eq = bits == thr
    n_gt = jnp.sum(gt.astype(F32), axis=1, keepdims=True)
    eq_rank = jnp.dot(eq.astype(BF16), tri_ref[...], preferred_element_type=F32)
    sel = gt | (eq & (eq_rank <= cap - n_gt))
    rank = jnp.dot(sel.astype(BF16), tri_ref[...], preferred_element_type=F32) - 1.0
    rank_ref[...] = jnp.where(sel, rank, -1.0).astype(jnp.int32)


def select_tokens(aff_t, cap):
    r, n = aff_t.shape
    tri = jnp.asarray(_prefix_table(n), BF16)
    return pl.pallas_call(
        functools.partial(_select_body, cap=cap),
        grid=(1,),
        in_specs=[pl.BlockSpec((r, n), lambda i: (0, 0)), pl.BlockSpec((n, n), lambda i: (0, 0))],
        out_specs=pl.BlockSpec((r, n), lambda i: (0, 0)),
        out_shape=jax.ShapeDtypeStruct((r, n), jnp.int32),
        compiler_params=_cparams(1),
        name="select",
    )(aff_t, tri)


GATHER_GROUP = 8


def _gather_body(rank_ref, hs_ref, o_ref):
    grp, cap, d = o_ref.shape
    n = hs_ref.shape[0]
    g0 = pl.multiple_of(pl.program_id(1) * grp, grp)
    ranks = rank_ref[pl.ds(g0, grp), :]
    slot = lax.broadcasted_iota(jnp.int32, (cap, n), 0)
    onehot = jnp.concatenate([jnp.where(slot == ranks[e:e + 1, :], 1.0, 0.0).astype(BF16) for e in range(grp)],
                             axis=0)
    rows = jnp.dot(onehot, hs_ref[...], preferred_element_type=F32)
    o_ref[...] = rows.reshape(grp, cap, d).astype(o_ref.dtype)


def gather_tokens(rank_t, hs, cap):
    s, e, n = rank_t.shape
    d = hs.shape[2]
    return pl.pallas_call(
        _gather_body,
        grid=(s, e // GATHER_GROUP),
        in_specs=[pl.BlockSpec((None, e, n), lambda ss, ee: (ss, 0, 0)),
                  pl.BlockSpec((None, n, d), lambda ss, ee: (ss, 0, 0))],
        out_specs=pl.BlockSpec((GATHER_GROUP, cap, d), lambda ss, ee: (ee, ss, 0)),
        out_shape=jax.ShapeDtypeStruct((e, s * cap, d), BF16),
        compiler_params=_cparams(2),
        name="gather",
    )(rank_t, hs)


FFN_SPLIT = 2
FFN_SPLIT_MIN_ROWS = 1024


def _ffn_body(*refs):
    x_refs, (wg_ref, wu_ref, wd_ref, o_ref, acc_ref) = refs[:-5], refs[-5:]
    f = pl.program_id(1)

    @pl.when(f == 0)
    def _():
        acc_ref[...] = jnp.zeros_like(acc_ref)

    wg = wg_ref[...].astype(BF16)
    wu = wu_ref[...].astype(BF16)
    wd = wd_ref[...].astype(BF16)
    row0 = 0
    for x_ref in x_refs:
        m = x_ref.shape[0]
        split = FFN_SPLIT if m >= FFN_SPLIT_MIN_ROWS else 1
        step = m // split
        for part in range(split):
            x = x_ref[pl.ds(part * step, step), :]
            g = jnp.dot(x, wg, preferred_element_type=F32)
            u = jnp.dot(x, wu, preferred_element_type=F32)
            hid = (g * _sigmoid(g) * u).astype(BF16)
            acc_ref[pl.ds(row0 + part * step, step), :] += jnp.dot(hid, wd, preferred_element_type=F32)
        row0 += m

    @pl.when(f == pl.num_programs(1) - 1)
    def _():
        o_ref[...] = acc_ref[...].astype(o_ref.dtype)


def expert_ffn(xs_list, wg, wu, wd, layer):
    e, _, d = xs_list[0].shape
    m = sum(x.shape[1] for x in xs_list)
    fdim = wg.shape[3]
    tf = _tile(fdim, 256)
    return pl.pallas_call(
        _ffn_body,
        grid=(e, fdim // tf),
        in_specs=[pl.BlockSpec((None, x.shape[1], d), lambda ee, f: (ee, 0, 0)) for x in xs_list]
        + [pl.BlockSpec((None, None, d, tf), lambda ee, f: (layer, ee, 0, f)),
           pl.BlockSpec((None, None, d, tf), lambda ee, f: (layer, ee, 0, f)),
           pl.BlockSpec((None, None, tf, d), lambda ee, f: (layer, ee, f, 0))],
        out_specs=pl.BlockSpec((None, m, d), lambda ee, f: (ee, 0, 0)),
        out_shape=jax.ShapeDtypeStruct((e, m, d), BF16),
        scratch_shapes=[pltpu.VMEM((m, d), F32)],
        compiler_params=_cparams(2),
        name="ffn",
    )(*xs_list, wg, wu, wd)


def _scatter_body(rank_ref, aff_ref, ys_ref, x_ref, mg_ref, gf_ref, o_ref, *, final_norm):
    tq = rank_ref.shape[0]
    n_exp, cap, _ = ys_ref.shape
    slot = lax.broadcasted_iota(jnp.int32, (tq, cap), 1)
    rank = rank_ref[...]
    aff = aff_ref[...]
    acc = None
    for e in range(n_exp):
        q = jnp.where(slot == rank[:, e:e + 1], aff[:, e:e + 1], 0.0).astype(BF16)
        part = jnp.dot(q, ys_ref[e], preferred_element_type=F32)
        acc = part if acc is None else acc + part
    x = x_ref[...] + mg_ref[...] * acc
    if final_norm:
        x = x * lax.rsqrt(jnp.mean(x * x, axis=-1, keepdims=True) + NORM_EPS) * gf_ref[...]
    o_ref[...] = x


def scatter_residual(rank, aff, ys, row_off, cap, x, mod_gate, final_g):
    s, n, e = rank.shape
    d = x.shape[2]
    tq = _tile(n, 1024)
    boff = row_off // cap
    final_norm = final_g is not None
    gf = (final_g if final_norm else jnp.ones((d,), F32)).reshape(1, d)
    return pl.pallas_call(
        functools.partial(_scatter_body, final_norm=final_norm),
        grid=(s, n // tq),
        in_specs=[pl.BlockSpec((None, tq, e), lambda ss, i: (ss, i, 0)),
                  pl.BlockSpec((None, tq, e), lambda ss, i: (ss, i, 0)),
                  pl.BlockSpec((e, cap, d), lambda ss, i: (0, boff + ss, 0)),
                  pl.BlockSpec((None, tq, d), lambda ss, i: (ss, i, 0)),
                  pl.BlockSpec((None, 1, d), lambda ss, i: (ss, 0, 0)),
                  pl.BlockSpec((1, d), lambda ss, i: (0, 0))],
        out_specs=pl.BlockSpec((None, tq, d), lambda ss, i: (ss, i, 0)),
        out_shape=jax.ShapeDtypeStruct(x.shape, F32),
        compiler_params=_cparams(2),
        name="scatter",
    )(rank, aff, ys, x, mod_gate, gf)


def _mixer_and_route(x, h0_f, h0_b, mods, lw, fs, shared_mod=False, pos_embed=None):
    b, l, d = x.shape
    if shared_mod:
        flat = lambda a: a.reshape(1, b * l, a.shape[-1])
        unflat = lambda a: a.reshape(b, l, a.shape[-1])
        mods = [m[:1] for m in mods]
    else:
        flat = unflat = lambda a: a
    if pos_embed is None:
        proj = unflat(norm_proj(flat(x), lw['norm1_g'], mods[0], mods[1], lw['w_in']))
    else:
        proj = norm_proj(x, lw['norm1_g'], mods[0], mods[1], lw['w_in'], pos_embed)
    t1, t2, mid = hy_filter_tables(l, lw['hy_filt_w1'], lw['hy_filt_b1'], lw['hy_filt_w2'], lw['hy_filt_b2'],
                                  lw['hy_filt_w3'], lw['hy_filt_freq'], fs)
    ya = hyena_mixer(proj, lw['hy_conv_w'], lw['hy_conv_b'], lw['hy_bias'], t1, t2, mid, fs)
    hf, hb = lru_mixer(proj, 4, lw['lru_conv_w'], lw['lru_conv_b'], lw['lru_wa'], lw['lru_ba'],
                       lw['lru_wx'], lw['lru_bx'], lw['lru_lambda'], h0_f, h0_b)
    yc = shortconv_mixer(proj, 5, lw['sc_conv_w'])
    x_new, h2, aff = merge_mixers(flat(proj), 8, 3, flat(ya), flat(yc), flat(hf), flat(hb), flat(x), mods[2],
                                  lw['norm2_g'], mods[3], mods[4],
                                  lw['w_hy_out'], lw['w_lru_out'], lw['w_sc_out'], lw['w_o'], lw['w_router'],
                                  pos_embed)
    return unflat(x_new), unflat(h2), unflat(aff), hf[:, -1], hb[:, 0]


def _capacity(n, e):
    return EC_CAPACITY * n // e


def _route(h2, aff):
    s, n, e = aff.shape
    cap = _capacity(n, e)
    aff_t = jnp.transpose(aff, (0, 2, 1))
    rank_t = select_tokens(aff_t.reshape(s * e, n), cap).reshape(s, e, n)
    return gather_tokens(rank_t, h2, cap), jnp.transpose(rank_t, (0, 2, 1))


def kernel(x, c, ctx, c_ctx, w_ada, b_ada, norm1_g, norm2_g, w_in, hy_conv_w, hy_conv_b, hy_filt_w1, hy_filt_b1, hy_filt_w2, hy_filt_b2, hy_filt_w3, hy_filt_freq, hy_bias, lru_conv_w, lru_conv_b, lru_wa, lru_ba, lru_wx, lru_bx, lru_lambda, sc_conv_w, w_hy_out, w_lru_out, w_sc_out, w_o, w_router, w_exp_gate, w_exp_up, w_exp_down, final_norm_g):
    bsz, n_lat, d = x.shape
    n_ctx = ctx.shape[1]
    depth = w_ada.shape[0]
    rows = n_lat // GRID_W

    pos_embed = jnp.asarray(_pos_embed(rows, d))
    xc = ctx
    fs_lat = jnp.asarray(_dft_table(n_lat), BF16)
    fs_ctx = jnp.asarray(_dft_table(n_ctx), BF16)

    cc = jnp.zeros((2 * bsz, d), F32).at[:bsz].set(c).at[bsz].set(c_ctx)
    mods = ada_mods(cc, w_ada, b_ada)
    zeros_state = jnp.zeros((bsz, d), F32)

    for l in range(depth):
        last = l == depth - 1
        lw = {
            'norm1_g': norm1_g[l], 'norm2_g': norm2_g[l], 'w_in': w_in[l].astype(BF16),
            'hy_conv_w': hy_conv_w[l], 'hy_conv_b': hy_conv_b[l],
            'hy_filt_w1': hy_filt_w1[l], 'hy_filt_b1': hy_filt_b1[l], 'hy_filt_w2': hy_filt_w2[l],
            'hy_filt_b2': hy_filt_b2[l], 'hy_filt_w3': hy_filt_w3[l], 'hy_filt_freq': hy_filt_freq[l],
            'hy_bias': hy_bias[l], 'lru_conv_w': lru_conv_w[l], 'lru_conv_b': lru_conv_b[l],
            'lru_wa': lru_wa[l], 'lru_ba': lru_ba[l], 'lru_wx': lru_wx[l], 'lru_bx': lru_bx[l],
            'lru_lambda': lru_lambda[l], 'sc_conv_w': sc_conv_w[l],
            'w_hy_out': w_hy_out[l].astype(BF16), 'w_lru_out': w_lru_out[l].astype(BF16),
            'w_sc_out': w_sc_out[l].astype(BF16), 'w_o': w_o[l].astype(BF16), 'w_router': w_router[l],
        }
        mod_l = [mods[l, :bsz, k * d:(k + 1) * d].reshape(bsz, 1, d) for k in range(6)]
        mod_c = [jnp.broadcast_to(mods[l, bsz, k * d:(k + 1) * d].reshape(1, 1, d), (bsz, 1, d)) for k in range(6)]

        n_exp = w_router.shape[2]
        cap_l = _capacity(n_lat, n_exp)
        cap_c = _capacity(n_ctx, n_exp)
        rows_l = bsz * cap_l
        if last:
            proj_c = norm_proj(xc.reshape(1, bsz * n_ctx, d), lw['norm1_g'], mod_c[0][:1], mod_c[1][:1],
                               lw['w_in'][:, 4 * d:5 * d]).reshape(bsz, n_ctx, d)
            hf_c, hb_c = lru_mixer(proj_c, 0, lw['lru_conv_w'], lw['lru_conv_b'], lw['lru_wa'], lw['lru_ba'],
                                   lw['lru_wx'], lw['lru_bx'], lw['lru_lambda'], zeros_state, zeros_state)
            state_f, state_b = hf_c[:, -1], hb_c[:, 0]
            xs_streams = []
        else:
            xc_mid, h2_c, aff_c, state_f, state_b = _mixer_and_route(xc, zeros_state, zeros_state, mod_c, lw, fs_ctx,
                                                                     shared_mod=True)
            xs_c, rank_c = _route(h2_c, aff_c)
            xs_streams = [xs_c]

        x_mid, h2_l, aff_l, _, _ = _mixer_and_route(x, state_f, state_b, mod_l, lw, fs_lat,
                                                    pos_embed=pos_embed if l == 0 else None)
        xs_l, rank_l = _route(h2_l, aff_l)
        ys = expert_ffn([xs_l] + xs_streams, w_exp_gate, w_exp_up, w_exp_down, l)

        if not last:
            xc = scatter_residual(rank_c, aff_c, ys, rows_l, cap_c, xc_mid, mod_c[5], None)
        x = scatter_residual(rank_l, aff_l, ys, 0, cap_l, x_mid, mod_l[5], final_norm_g if last else None)
    return x
```

```python
import functools
import math

import numpy as np
import jax
import jax.numpy as jnp
from jax import lax
from jax.experimental import pallas as pl
from jax.experimental.pallas import tpu as pltpu

F32 = jnp.float32
BF16 = jnp.bfloat16
HI = lax.Precision.HIGHEST

NORM_EPS = 1e-6
GRID_W = 64
N_EXPERTS = 16
EC_CAPACITY = 2
LRU_C = 8.0
HYENA_ORDER = 2
HYENA_EMB_DIM = 33
HYENA_FAST_DECAY = 0.3
HYENA_SLOW_DECAY = 1.5
HYENA_TARGET = 1e-2
MLP_PAD = 128

V7X_VMEM_BYTES = 64 * 1024 * 1024
VMEM_LIMIT = 56 * 1024 * 1024
assert VMEM_LIMIT < V7X_VMEM_BYTES


def _cparams(n_grid):
    return pltpu.CompilerParams(dimension_semantics=("arbitrary",) * n_grid,
                                vmem_limit_bytes=VMEM_LIMIT)


def _tile(n, pref):
    if n <= pref:
        return n
    t = pref
    while n % t:
        t //= 2
    return t


def _resident(block_shape, index_map):
    return pl.BlockSpec(block_shape, index_map, pipeline_mode=pl.Buffered(1))


@functools.lru_cache(maxsize=None)
def _pos_embed(rows, d):
    def sincos(pos, dim):
        half = dim // 2
        omega = 1.0 / (10000.0 ** (np.arange(half, dtype=np.float64) / half))
        ang = pos[:, None] * omega[None, :]
        return np.concatenate([np.sin(ang), np.cos(ang)], axis=-1)
    half = d // 2
    er = sincos(np.arange(rows, dtype=np.float64), half)
    ec = sincos(np.arange(GRID_W, dtype=np.float64), half)
    emb = np.concatenate([np.broadcast_to(er[:, None, :], (rows, GRID_W, half)),
                          np.broadcast_to(ec[None, :, :], (rows, GRID_W, half))], axis=-1)
    return emb.reshape(rows * GRID_W, d).astype(np.float32)


@functools.lru_cache(maxsize=None)
def _dft_table(L):
    h = L // 2
    k = np.arange(h, dtype=np.int64)
    ang_e = np.pi * ((k[:, None] * (2 * k[None, :])) % (2 * L)).astype(np.float64) / L
    ang_o = np.pi * ((k[:, None] * (2 * k[None, :] + 1)) % (2 * L)).astype(np.float64) / L
    ce, se, co, so = np.cos(ang_e), np.sin(ang_e), np.cos(ang_o), np.sin(ang_o)
    return np.concatenate([ce, se, co, so, co.T, so.T], axis=0).astype(np.float32)


@functools.lru_cache(maxsize=None)
def _filter_feats(L):
    bands = (HYENA_EMB_DIM - 1) // 2
    t01 = np.linspace(0.0, 1.0, L, dtype=np.float32).astype(np.float64)[:, None]
    w = ((2.0 * math.pi / L) * np.arange(L, dtype=np.float32))[:, None].astype(np.float64)
    f = np.linspace(1e-4, bands - 1, bands, dtype=np.float32).astype(np.float64)[None, :]
    feats = np.concatenate([t01, np.cos(f * w), -np.sin(f * w)], axis=-1)
    out = np.zeros((L, MLP_PAD), np.float32)
    out[:, :HYENA_EMB_DIM] = feats
    return out


@functools.lru_cache(maxsize=None)
def _decay_window(L, d):
    max_decay = math.log(HYENA_TARGET) / HYENA_FAST_DECAY
    min_decay = math.log(HYENA_TARGET) / HYENA_SLOW_DECAY
    t01 = np.linspace(0.0, 1.0, L, dtype=np.float32).astype(np.float64)[:, None]
    deltas = np.linspace(min_decay, max_decay, d, dtype=np.float32).astype(np.float64)
    return np.exp(-t01 * np.abs(deltas)[None, :]).astype(np.float32)


@functools.lru_cache(maxsize=None)
def _prefix_table(n):
    return np.triu(np.ones((n, n), np.float32))


def _shift_rows(x, s):
    if s == 0:
        return x
    n = x.shape[0]
    rolled = pltpu.roll(x, s % n, axis=0)
    row = lax.broadcasted_iota(jnp.int32, x.shape, 0)
    keep = row >= s if s > 0 else row < n + s
    return jnp.where(keep, rolled, 0.0)


def _dwconv(x, w_ref, left):
    acc = None
    for k in range(w_ref.shape[0]):
        term = _shift_rows(x, left - k) * w_ref[k:k + 1, :]
        acc = term if acc is None else acc + term
    return acc


def _sigmoid(x):
    return 0.5 * jnp.tanh(0.5 * x) + 0.5


def _split_hi_lo(x):
    hi = x.astype(BF16)
    return hi, (x - hi.astype(F32)).astype(BF16)


def _dot_hi_lo(a, b):
    a_hi, a_lo = _split_hi_lo(a)
    b_hi, b_lo = _split_hi_lo(b)
    return (jnp.dot(a_hi, b_hi, preferred_element_type=F32) + jnp.dot(a_hi, b_lo, preferred_element_type=F32)
            + jnp.dot(a_lo, b_hi, preferred_element_type=F32))


def _rms_modulate(x, g, shift, scale):
    y = x * lax.rsqrt(jnp.mean(x * x, axis=-1, keepdims=True) + NORM_EPS)
    return (y * g) * (1.0 + scale) + shift


def _ada_body(c_ref, w_ref, b_ref, o_ref):
    c = c_ref[...]
    s = c * _sigmoid(c)
    o_ref[...] = jnp.dot(s, w_ref[...], precision=HI, preferred_element_type=F32) + b_ref[...]


def ada_mods(cc, w_ada, b_ada):
    depth, d, n = w_ada.shape
    r = cc.shape[0]
    tn = _tile(n, 1536)
    return pl.pallas_call(
        _ada_body,
        grid=(depth, n // tn),
        in_specs=[pl.BlockSpec((r, d), lambda l, j: (0, 0)),
                  pl.BlockSpec((None, d, tn), lambda l, j: (l, 0, j)),
                  pl.BlockSpec((None, 1, tn), lambda l, j: (l, 0, j))],
        out_specs=pl.BlockSpec((None, r, tn), lambda l, j: (l, 0, j)),
        out_shape=jax.ShapeDtypeStruct((depth, r, n), F32),
        compiler_params=_cparams(2),
        name="ada",
    )(cc, w_ada, b_ada.reshape(depth, 1, n))


def _norm_proj_body(x_ref, g_ref, sh_ref, sc_ref, w_ref, o_ref, h_ref):
    @pl.when(pl.program_id(2) == 0)
    def _():
        h_ref[...] = _rms_modulate(x_ref[...], g_ref[...], sh_ref[...], sc_ref[...]).astype(BF16)

    o_ref[...] = jnp.dot(h_ref[...], w_ref[...], preferred_element_type=F32).astype(o_ref.dtype)


def _embed_norm_proj_body(x_ref, pe_ref, g_ref, sh_ref, sc_ref, w_ref, o_ref, h_ref):
    @pl.when(pl.program_id(2) == 0)
    def _():
        h_ref[...] = _rms_modulate(x_ref[...] + pe_ref[...], g_ref[...], sh_ref[...], sc_ref[...]).astype(BF16)

    o_ref[...] = jnp.dot(h_ref[...], w_ref[...], preferred_element_type=F32).astype(o_ref.dtype)


PROJ_COLS = 2816


def norm_proj(x, g, shift, scale, w, pos_embed=None):
    b, l, d = x.shape
    n = w.shape[1]
    tl = _tile(l, 1024)
    tn = _tile(n, PROJ_COLS)
    row = pl.BlockSpec((None, tl, d), lambda bb, i, j: (bb, i, 0))
    tail = [pl.BlockSpec((1, d), lambda bb, i, j: (0, 0)),
            pl.BlockSpec((None, 1, d), lambda bb, i, j: (bb, 0, 0)),
            pl.BlockSpec((None, 1, d), lambda bb, i, j: (bb, 0, 0)),
            pl.BlockSpec((d, tn), lambda bb, i, j: (0, j))]
    proj_spec = pl.BlockSpec((None, tl, tn), lambda bb, i, j: (bb, i, j))
    proj_shape = jax.ShapeDtypeStruct((b, l, n), BF16)
    common = dict(grid=(b, l // tl, n // tn), scratch_shapes=[pltpu.VMEM((tl, d), BF16)],
                  compiler_params=_cparams(3), name="norm_proj")
    if pos_embed is None:
        return pl.pallas_call(_norm_proj_body, in_specs=[row] + tail, out_specs=proj_spec, out_shape=proj_shape,
                              **common)(x, g.reshape(1, d), shift, scale, w)
    return pl.pallas_call(
        _embed_norm_proj_body,
        in_specs=[row, pl.BlockSpec((tl, d), lambda bb, i, j: (i, 0))] + tail,
        out_specs=proj_spec, out_shape=proj_shape,
        **common)(x, pos_embed, g.reshape(1, d), shift, scale, w)


def _hy_hidden_body(feats_ref, w1_ref, b1_ref, w2_ref, b2_ref, fr_ref, h_ref):
    fr = fr_ref[...]
    h = jnp.sin(fr * (jnp.dot(feats_ref[...], w1_ref[...], precision=HI,
                              preferred_element_type=F32) + b1_ref[...]))
    h_ref[...] = jnp.sin(fr * (jnp.dot(h, w2_ref[...], precision=HI,
                                       preferred_element_type=F32) + b2_ref[...]))


LANES = 128


def _sign_col(rows):
    row = lax.broadcasted_iota(jnp.int32, (rows, 1), 0)
    return (1 - 2 * (row & 1)).astype(F32)


def _put_slabs(nat_ref, r0, rows):
    for s in range(nat_ref.shape[0]):
        nat_ref[s, pl.ds(r0, rows.shape[0]), :] = rows[:, s * LANES:(s + 1) * LANES]


def _get_slabs(nat_ref, r0, n):
    return jnp.concatenate([nat_ref[s, pl.ds(r0, n), :] for s in range(nat_ref.shape[0])], axis=1)


def _split_rows(nat_ref, e_ref, o_ref):
    nslab, L, _ = nat_ref.shape
    h = L // 2
    rd = min(h, 256)
    sgn = _sign_col(rd)

    def body(i, carry):
        m0 = pl.multiple_of(i * rd, rd)
        e = jnp.concatenate([nat_ref[s, pl.ds(2 * m0, rd, stride=2), :] for s in range(nslab)], axis=1)
        o = jnp.concatenate([nat_ref[s, pl.ds(2 * m0 + 1, rd, stride=2), :] for s in range(nslab)], axis=1)
        e_ref[pl.ds(m0, rd), :] = e.astype(BF16)
        o_ref[pl.ds(m0, rd), :] = o.astype(BF16)
        return (carry[0] + jnp.sum(e * sgn, axis=0, keepdims=True),
                carry[1] + jnp.sum(o * sgn, axis=0, keepdims=True))

    zero = jnp.zeros((1, nslab * LANES), F32)
    return lax.fori_loop(0, h // rd, body, (zero, zero))


def _hy_filter_body(h_ref, w3f_ref, w3b_ref, decay_ref, tb_ref, t1_ref, t2_ref, mid_ref,
                    p_ref, q_ref, pe_ref, po_ref, qe_ref, qo_ref):
    L = h_ref.shape[0]
    hl = L // 2
    hid = h_ref[...]
    decay = decay_ref[...]
    fwd = _dot_hi_lo(hid, w3f_ref[...]) * decay
    bwd = _dot_hi_lo(hid, w3b_ref[...]) * decay
    row = lax.broadcasted_iota(jnp.int32, fwd.shape, 0)
    bwd = jnp.where(row >= 1, bwd, 0.0)
    inv = 1.0 / (jnp.sum(jnp.abs(fwd), axis=0, keepdims=True)
                 + jnp.sum(jnp.abs(bwd), axis=0, keepdims=True))
    _put_slabs(p_ref, 0, fwd + bwd)
    _put_slabs(q_ref, 0, bwd - fwd)
    k_re_mid, _ = _split_rows(p_ref, pe_ref, po_ref)
    _, k_im_mid = _split_rows(q_ref, qe_ref, qo_ref)
    a_e = jnp.dot(tb_ref[0:hl, :], pe_ref[...], preferred_element_type=F32)
    a_o = jnp.dot(tb_ref[2 * hl:3 * hl, :], po_ref[...], preferred_element_type=F32)
    b_e = jnp.dot(tb_ref[hl:2 * hl, :], qe_ref[...], preferred_element_type=F32)
    b_o = jnp.dot(tb_ref[3 * hl:4 * hl, :], qo_ref[...], preferred_element_type=F32)
    n_fft = 2.0 * L
    row_h = lax.broadcasted_iota(jnp.int32, (hl, fwd.shape[1]), 0)
    w_k = jnp.where(row_h >= 1, 2.0 / n_fft, 1.0 / n_fft) * inv
    w_s = (2.0 / n_fft) * inv
    t1_ref[0:hl, :] = (a_e + a_o) * w_k
    t1_ref[hl:L, :] = (a_e - a_o) * w_k
    t2_ref[0:hl, :] = (b_e + b_o) * w_s
    t2_ref[hl:L, :] = (b_o - b_e) * w_s
    mid_ref[0:1, :] = k_re_mid * w_s
    mid_ref[1:2, :] = k_im_mid * w_s


def hy_filter_tables(L, w1, b1, w2, b2, w3, freq, fs):
    d = w3.shape[1] // (HYENA_ORDER * 2)
    fw = w1.shape[1]
    pad = lambda a, r, c: jnp.zeros((r, c), F32).at[:a.shape[0], :a.shape[1]].set(a)
    w1p = pad(w1, MLP_PAD, MLP_PAD)
    w2p = pad(w2, MLP_PAD, MLP_PAD)
    w3p = pad(w3, MLP_PAD, w3.shape[1])
    b1p = pad(b1.reshape(1, fw), 1, MLP_PAD)
    b2p = pad(b2.reshape(1, fw), 1, MLP_PAD)
    frp = pad(freq.reshape(1, fw), 1, MLP_PAD)
    feats = jnp.asarray(_filter_feats(L))
    decay = jnp.asarray(_decay_window(L, d))
    tn = _tile(d, 256)
    nj = d // tn
    full = lambda o, j: (0, 0)
    one = lambda i: (0, 0)
    hidden = pl.pallas_call(
        _hy_hidden_body,
        grid=(1,),
        in_specs=[pl.BlockSpec((L, MLP_PAD), one),
                  pl.BlockSpec((MLP_PAD, MLP_PAD), one),
                  pl.BlockSpec((1, MLP_PAD), one),
                  pl.BlockSpec((MLP_PAD, MLP_PAD), one),
                  pl.BlockSpec((1, MLP_PAD), one),
                  pl.BlockSpec((1, MLP_PAD), one)],
        out_specs=pl.BlockSpec((L, MLP_PAD), one),
        out_shape=jax.ShapeDtypeStruct((L, MLP_PAD), F32),
        compiler_params=_cparams(1),
        name="hy_hidden",
    )(feats, w1p, b1p, w2p, b2p, frp)
    hl = L // 2
    t1, t2, mid = pl.pallas_call(
        _hy_filter_body,
        grid=(HYENA_ORDER, nj),
        in_specs=[pl.BlockSpec((L, MLP_PAD), full),
                  pl.BlockSpec((MLP_PAD, tn), lambda o, j: (0, o * 2 * nj + j)),
                  pl.BlockSpec((MLP_PAD, tn), lambda o, j: (0, o * 2 * nj + nj + j)),
                  pl.BlockSpec((L, tn), lambda o, j: (0, j)),
                  _resident(fs.shape, full)],
        out_specs=[pl.BlockSpec((L, tn), lambda o, j: (0, o * nj + j)),
                   pl.BlockSpec((L, tn), lambda o, j: (0, o * nj + j)),
                   pl.BlockSpec((2, tn), lambda o, j: (0, o * nj + j))],
        out_shape=[jax.ShapeDtypeStruct((L, HYENA_ORDER * d), F32),
                   jax.ShapeDtypeStruct((L, HYENA_ORDER * d), F32),
                   jax.ShapeDtypeStruct((2, HYENA_ORDER * d), F32)],
        scratch_shapes=[pltpu.VMEM((tn // LANES, L, LANES), F32), pltpu.VMEM((tn // LANES, L, LANES), F32),
                        pltpu.VMEM((hl, tn), BF16), pltpu.VMEM((hl, tn), BF16),
                        pltpu.VMEM((hl, tn), BF16), pltpu.VMEM((hl, tn), BF16)],
        compiler_params=_cparams(2),
        name="hy_filter",
    )(hidden, w3p, w3p, decay, fs)
    return t1, t2, mid


HALO = 8


def _chunks(n_rows, chunk, fn):
    def body(c, carry):
        fn(pl.multiple_of(c * chunk, chunk))
        return carry
    lax.fori_loop(0, n_rows // chunk, body, 0)


def _conv3_even_odd(src_ref, w_ref, b_ref, pad_ref, emit):
    assert w_ref.shape[0] == 3
    L = src_ref.shape[0]
    nslab = pad_ref.shape[0]
    h = L // 2
    rc = min(L, 256)
    rd = min(h, 128)
    zeros = jnp.zeros((HALO, LANES), F32)
    for s in range(nslab):
        pad_ref[s, 0:HALO, :] = zeros
        pad_ref[s, L + HALO:L + 2 * HALO, :] = zeros

    def fill(r0):
        _put_slabs(pad_ref, r0 + HALO, src_ref[pl.ds(r0, rc), :].astype(F32))
    _chunks(L, rc, fill)

    def taps(m0, off):
        return jnp.concatenate([pad_ref[s, pl.ds(2 * m0 + (HALO - 1 + off), rd, stride=2), :]
                                for s in range(nslab)], axis=1)

    def conv(m0):
        t0, t1, t2, t3 = (taps(m0, off) for off in range(4))
        w0, w1, w2 = w_ref[0:1, :], w_ref[1:2, :], w_ref[2:3, :]
        bias = b_ref[...]
        emit(m0, t0 * w0 + t1 * w1 + t2 * w2 + bias, t1 * w0 + t2 * w1 + t3 * w2 + bias)
    _chunks(h, rd, conv)


def _long_conv_even_odd(u_refs, t1_ref, t2_ref, mid_ref, skip, tb_ref, spec_e_ref, spec_o_ref,
                        xec_ref, xes_ref, xoc_ref, xos_ref, emit):
    ue32_ref, uo32_ref, ue_ref, uo_ref, umid_ref = u_refs
    h = ue_ref.shape[0]
    rm = min(h, 64)
    ri = min(h, 1024)

    spec_e_ref[...] = jnp.dot(tb_ref[0:2 * h, :], ue_ref[...], preferred_element_type=F32)
    spec_o_ref[...] = jnp.dot(tb_ref[2 * h:4 * h, :], uo_ref[...], preferred_element_type=F32)

    def mix(r0):
        ae = spec_e_ref[pl.ds(r0, rm), :]
        be = spec_e_ref[pl.ds(h + r0, rm), :]
        ao = spec_o_ref[pl.ds(r0, rm), :]
        bo = spec_o_ref[pl.ds(h + r0, rm), :]
        p0, p1, q0, q1 = ae + ao, ae - ao, be + bo, bo - be
        t1_lo = t1_ref[pl.ds(r0, rm), :]
        t2_lo = t2_ref[pl.ds(r0, rm), :]
        t1_hi = t1_ref[pl.ds(h + r0, rm), :]
        t2_hi = t2_ref[pl.ds(h + r0, rm), :]
        za0 = p0 * t1_lo + q0 * t2_lo
        zb0 = q0 * t1_lo - p0 * t2_lo
        za1 = p1 * t1_hi + q1 * t2_hi
        zb1 = q1 * t1_hi - p1 * t2_hi
        xec_ref[pl.ds(r0, rm), :] = (za0 + za1).astype(BF16)
        xes_ref[pl.ds(r0, rm), :] = (zb0 - zb1).astype(BF16)
        xoc_ref[pl.ds(r0, rm), :] = (za0 - za1).astype(BF16)
        xos_ref[pl.ds(r0, rm), :] = (zb0 + zb1).astype(BF16)
    _chunks(h, rm, mix)

    ua_mid = umid_ref[0:1, :]
    ub_mid = umid_ref[1:2, :]
    t1_mid = mid_ref[0:1, :]
    t2_mid = mid_ref[1:2, :]
    za_mid = ua_mid * t1_mid + ub_mid * t2_mid
    zb_mid = ub_mid * t1_mid - ua_mid * t2_mid
    sgn = _sign_col(ri)

    def inverse(r0):
        ye = jnp.dot(tb_ref[pl.ds(r0, ri), :], xec_ref[...], preferred_element_type=F32)
        ye = ye + jnp.dot(tb_ref[pl.ds(h + r0, ri), :], xes_ref[...], preferred_element_type=F32)
        yo = jnp.dot(tb_ref[pl.ds(4 * h + r0, ri), :], xoc_ref[...], preferred_element_type=F32)
        yo = yo + jnp.dot(tb_ref[pl.ds(5 * h + r0, ri), :], xos_ref[...], preferred_element_type=F32)
        emit(r0, ye + sgn * za_mid + ue32_ref[pl.ds(r0, ri), :] * skip,
             yo + sgn * zb_mid + uo32_ref[pl.ds(r0, ri), :] * skip)
    _chunks(h, ri, inverse)


def _hyena_body(p1_ref, p2_ref, pv_ref, cw1_ref, cw2_ref, cwv_ref, cb1_ref, cb2_ref, cbv_ref,
                bias_ref, t1a_ref, t2a_ref, mida_ref, t1b_ref, t2b_ref, midb_ref, tb_ref, o_ref,
                pad_ref, x1e_ref, x1o_ref, x2e_ref, x2o_ref, ue32_ref, uo32_ref, ue_ref, uo_ref, umid_ref,
                spec_e_ref, spec_o_ref, xec_ref, xes_ref, xoc_ref, xos_ref, out_ref):
    L = p1_ref.shape[0]

    def put(e_ref, o_ref_):
        def emit(m0, e, o):
            e_ref[pl.ds(m0, e.shape[0]), :] = e
            o_ref_[pl.ds(m0, o.shape[0]), :] = o
        return emit

    def put_u(m0, e, o):
        n = e.shape[0]
        ue32_ref[pl.ds(m0, n), :] = e
        uo32_ref[pl.ds(m0, n), :] = o
        ue_ref[pl.ds(m0, n), :] = e.astype(BF16)
        uo_ref[pl.ds(m0, n), :] = o.astype(BF16)
        sgn = _sign_col(n)
        umid_ref[0:1, :] += jnp.sum(e * sgn, axis=0, keepdims=True)
        umid_ref[1:2, :] += jnp.sum(o * sgn, axis=0, keepdims=True)

    _conv3_even_odd(p1_ref, cw1_ref, cb1_ref, pad_ref, put(x1e_ref, x1o_ref))
    _conv3_even_odd(p2_ref, cw2_ref, cb2_ref, pad_ref, put(x2e_ref, x2o_ref))
    umid_ref[...] = jnp.zeros_like(umid_ref)
    _conv3_even_odd(pv_ref, cwv_ref, cbv_ref, pad_ref, put_u)
    u_refs = (ue32_ref, uo32_ref, ue_ref, uo_ref, umid_ref)
    work = (tb_ref, spec_e_ref, spec_o_ref, xec_ref, xes_ref, xoc_ref, xos_ref)

    def first(m0, ce, co):
        @pl.when(m0 == 0)
        def _():
            umid_ref[...] = jnp.zeros_like(umid_ref)
        n = ce.shape[0]
        put_u(m0, x1e_ref[pl.ds(m0, n), :] * ce, x1o_ref[pl.ds(m0, n), :] * co)
    _long_conv_even_odd(u_refs, t1a_ref, t2a_ref, mida_ref, bias_ref[0:1, :], *work, first)

    def second(m0, ce, co):
        n = ce.shape[0]
        oe = x2e_ref[pl.ds(m0, n), :] * ce
        oo = x2o_ref[pl.ds(m0, n), :] * co
        for s in range(out_ref.shape[0]):
            out_ref[s, pl.ds(2 * m0, n, stride=2), :] = oe[:, s * LANES:(s + 1) * LANES]
            out_ref[s, pl.ds(2 * m0 + 1, n, stride=2), :] = oo[:, s * LANES:(s + 1) * LANES]
    _long_conv_even_odd(u_refs, t1b_ref, t2b_ref, midb_ref, bias_ref[1:2, :], *work, second)

    rc = min(L, 256)

    def finish(r0):
        o_ref[pl.ds(r0, rc), :] = _get_slabs(out_ref, r0, rc).astype(o_ref.dtype)
    _chunks(L, rc, finish)


def hyena_mixer(proj, conv_w, conv_b, bias, t1, t2, mid, fs):
    b, l, _ = proj.shape
    d = bias.shape[1]
    tn = _tile(d, 256)
    nj = d // tn
    k = conv_w.shape[0]
    hl = l // 2
    cb = conv_b.reshape(1, 3 * d)
    sec = lambda s: pl.BlockSpec((None, l, tn), lambda j, bb: (bb, 0, s * nj + j))
    cws = lambda s: pl.BlockSpec((k, tn), lambda j, bb: (0, s * nj + j))
    cbs = lambda s: pl.BlockSpec((1, tn), lambda j, bb: (0, s * nj + j))
    tab = lambda o, r: _resident((r, tn), lambda j, bb: (0, o * nj + j))
    half_f32 = pltpu.VMEM((hl, tn), F32)
    half_bf16 = pltpu.VMEM((hl, tn), BF16)
    return pl.pallas_call(
        _hyena_body,
        grid=(nj, b),
        in_specs=[sec(0), sec(1), sec(2), cws(0), cws(1), cws(2), cbs(0), cbs(1), cbs(2),
                  pl.BlockSpec((HYENA_ORDER, tn), lambda j, bb: (0, j)),
                  tab(0, l), tab(0, l), tab(0, 2), tab(1, l), tab(1, l), tab(1, 2),
                  _resident(fs.shape, lambda j, bb: (0, 0))],
        out_specs=pl.BlockSpec((None, l, tn), lambda j, bb: (bb, 0, j)),
        out_shape=jax.ShapeDtypeStruct((b, l, d), BF16),
        scratch_shapes=[pltpu.VMEM((tn // LANES, l + 2 * HALO, LANES), F32),
                        half_f32, half_f32, half_f32, half_f32,
                        half_f32, half_f32, half_bf16, half_bf16,
                        pltpu.VMEM((8, tn), F32),
                        pltpu.VMEM((l, tn), F32), pltpu.VMEM((l, tn), F32),
                        half_bf16, half_bf16, half_bf16, half_bf16,
                        pltpu.VMEM((tn // LANES, l, LANES), F32)],
        compiler_params=_cparams(2),
        name="hyena",
    )(proj, proj, proj, conv_w, conv_w, conv_w, cb, cb, cb, bias, t1, t2, mid, t1, t2, mid, fs)


SQRT_GUARD = 1e-30
LRU_GATE_TILE = 256
LRU_LANES = 512


def _lru_gates(rec_ref, cw_ref, cb_ref, wa_ref, ba_ref, wx_ref, bx_ref, lam_ref, outs):
    tg = wa_ref.shape[-1]
    for t in range(rec_ref.shape[1] // tg):
        cols = slice(t * tg, (t + 1) * tg)
        xc = _dwconv(rec_ref[:, cols].astype(F32), cw_ref.at[:, cols], 2) + cb_ref[:, cols]
        xcb = xc.astype(BF16)
        xh = 0.5 * xc
        for dr, (a_out, b_out) in enumerate(outs):
            tr = jnp.tanh(jnp.dot(xcb, wa_ref[dr, t], preferred_element_type=F32) + 0.5 * ba_ref[dr:dr + 1, cols])
            ti = jnp.tanh(jnp.dot(xcb, wx_ref[dr, t], preferred_element_type=F32) + 0.5 * bx_ref[dr:dr + 1, cols])
            ch = (-0.5 * LRU_C * math.log2(math.e)) * jnp.log1p(jnp.exp(-lam_ref[dr:dr + 1, cols]))
            a = jnp.exp2(tr * ch + ch)
            a_out[:, cols] = a
            om = 1.0 - a * a
            b_out[:, cols] = (om * lax.rsqrt(jnp.maximum(om, SQRT_GUARD))) * (ti * xh + xh)


def _scan8(a, b, h, row, reverse):
    for s in (1, 2, 4):
        has_prev = row < 8 - s if reverse else row >= s
        shift = 8 - s if reverse else s
        b = jnp.where(has_prev, a, 0.0) * pltpu.roll(b, shift, axis=0) + b
        a = a * jnp.where(has_prev, pltpu.roll(a, shift, axis=0), 1.0)
    hs = a * h + b
    return hs, (hs[0:1, :] if reverse else hs[7:8, :])


def _lru_body(rec_ref, cw_ref, cb_ref, wa_ref, ba_ref, wx_ref, bx_ref, lam_ref, h0f_ref, h0b_ref,
              hf_ref, hb_ref, af_ref, ab_ref):
    _lru_gates(rec_ref, cw_ref, cb_ref, wa_ref, ba_ref, wx_ref, bx_ref, lam_ref,
               ((af_ref, hf_ref), (ab_ref, hb_ref)))
    groups = rec_ref.shape[0] // 8
    row = lax.broadcasted_iota(jnp.int32, (8, hf_ref.shape[1]), 0)

    def step(g, carry):
        rf = pl.multiple_of(g * 8, 8)
        rb = pl.multiple_of((groups - 1 - g) * 8, 8)
        hf, cf = _scan8(af_ref[pl.ds(rf, 8), :], hf_ref[pl.ds(rf, 8), :], carry[0], row, False)
        hb, cb = _scan8(ab_ref[pl.ds(rb, 8), :], hb_ref[pl.ds(rb, 8), :], carry[1], row, True)
        hf_ref[pl.ds(rf, 8), :] = hf
        hb_ref[pl.ds(rb, 8), :] = hb
        return cf, cb

    lax.fori_loop(0, groups, step, (h0f_ref[...], h0b_ref[...]))


def _block_diag_tiles(w, tn):
    two, h, bs, _ = w.shape
    hpt = tn // bs
    wt = w.reshape(two, h // hpt, hpt, bs, bs)
    eye = jnp.eye(hpt, dtype=w.dtype)
    dense = jnp.einsum('tnhij,hg->tnhigj', wt, eye)
    return dense.reshape(two, h // hpt, tn, tn)


def lru_mixer(proj, rec_sec, conv_w, conv_b, wa, ba, wx, bx, lam, h0_f, h0_b):
    b, l, n = proj.shape
    d = conv_w.shape[1]
    tg = _tile(d, LRU_GATE_TILE)
    tn = _tile(d, LRU_LANES)
    nj = d // tn
    tpj = tn // tg
    k = conv_w.shape[0]
    wa_t = (0.5 * _block_diag_tiles(wa, tg)).astype(BF16)
    wx_t = (0.5 * _block_diag_tiles(wx, tg)).astype(BF16)
    vec = lambda r: pl.BlockSpec((r, tn), lambda bb, j: (0, j))
    wsp = pl.BlockSpec((2, tpj, tg, tg), lambda bb, j: (0, j, 0, 0))
    h0s = pl.BlockSpec((None, 1, tn), lambda bb, j: (bb, 0, j))
    out = pl.BlockSpec((None, l, tn), lambda bb, j: (bb, 0, j))
    shp = jax.ShapeDtypeStruct((b, l, d), F32)
    return pl.pallas_call(
        _lru_body,
        grid=(b, nj),
        in_specs=[pl.BlockSpec((None, l, tn), lambda bb, j: (bb, 0, rec_sec * nj + j)),
                  vec(k), vec(1), wsp, vec(2), wsp, vec(2), vec(2), h0s, h0s],
        out_specs=[out, out],
        out_shape=[shp, shp],
        scratch_shapes=[pltpu.VMEM((l, tn), F32), pltpu.VMEM((l, tn), F32)],
        compiler_params=_cparams(2),
        name="lru",
    )(proj, conv_w, conv_b.reshape(1, d), wa_t, ba, wx_t, bx, lam, h0_f.reshape(b, 1, d), h0_b.reshape(b, 1, d))


def _shortconv_body(bg_ref, cg_ref, xv_ref, w_ref, o_ref):
    prod = cg_ref[...].astype(F32) * xv_ref[...].astype(F32)
    o_ref[...] = (bg_ref[...].astype(F32) * _dwconv(prod, w_ref, 1)).astype(o_ref.dtype)


def shortconv_mixer(proj, first_sec, conv_w):
    b, l, _ = proj.shape
    k, d = conv_w.shape
    tn = _tile(d, 256)
    nj = d // tn
    sec = lambda s: pl.BlockSpec((None, l, tn), lambda bb, j: (bb, 0, (first_sec + s) * nj + j))
    return pl.pallas_call(
        _shortconv_body,
        grid=(b, nj),
        in_specs=[sec(0), sec(1), sec(2), pl.BlockSpec((k, tn), lambda bb, j: (0, j))],
        out_specs=pl.BlockSpec((None, l, tn), lambda bb, j: (bb, 0, j)),
        out_shape=jax.ShapeDtypeStruct((b, l, d), BF16),
        compiler_params=_cparams(2),
        name="shortconv",
    )(proj, proj, proj, conv_w)


MERGE_SPLIT = 2


def _gelu_tanh(x):
    return 0.5 * x * (1.0 + jnp.tanh(math.sqrt(2.0 / math.pi) * (x + 0.044715 * (x * x * x))))


def _merge_body(*refs, has_pos_embed):
    (ya_ref, yc_ref, lg_ref, hf_ref, hb_ref, ga_ref, gb_ref, gc_ref, x_ref), refs = refs[:9], refs[9:]
    if has_pos_embed:
        pe_ref, refs = refs[0], refs[1:]
    (mg_ref, g2_ref, sh_ref, sc_ref, wa_ref, wb_ref, wc_ref, wo_ref, wr_ref, xo_ref, h2_ref, aff_ref) = refs
    tl = x_ref.shape[0]
    half = tl // MERGE_SPLIT
    wr = wr_ref[...]
    wr_hi = wr.astype(BF16)
    wr_lo = (wr - wr_hi.astype(F32)).astype(BF16)
    for part in range(MERGE_SPLIT):
        rows = pl.ds(part * half, half)
        yb = (_gelu_tanh(lg_ref[rows, :].astype(F32)) * (hf_ref[rows, :] + hb_ref[rows, :])).astype(BF16)
        merged = _sigmoid(ga_ref[rows, :].astype(F32)) * jnp.dot(ya_ref[rows, :], wa_ref[...], preferred_element_type=F32)
        merged += _sigmoid(gb_ref[rows, :].astype(F32)) * jnp.dot(yb, wb_ref[...], preferred_element_type=F32)
        merged += _sigmoid(gc_ref[rows, :].astype(F32)) * jnp.dot(yc_ref[rows, :], wc_ref[...], preferred_element_type=F32)
        out = jnp.dot(merged.astype(BF16), wo_ref[...], preferred_element_type=F32)
        x_in = x_ref[rows, :] + pe_ref[rows, :] if has_pos_embed else x_ref[rows, :]
        x = x_in + mg_ref[...] * out
        xo_ref[rows, :] = x
        h2 = _rms_modulate(x, g2_ref[...], sh_ref[...], sc_ref[...])
        h2_hi = h2.astype(BF16)
        h2_ref[rows, :] = h2_hi
        h2_lo = (h2 - h2_hi.astype(F32)).astype(BF16)
        logits = (jnp.dot(h2_hi, wr_hi, preferred_element_type=F32)
                  + jnp.dot(h2_hi, wr_lo, preferred_element_type=F32)
                  + jnp.dot(h2_lo, wr_hi, preferred_element_type=F32))
        e = jnp.exp(logits - jnp.max(logits, axis=-1, keepdims=True))
        aff_ref[rows, :] = e / jnp.sum(e, axis=-1, keepdims=True)


def merge_mixers(proj, gate_sec, lg_sec, ya, yc, hf, hb, x, mod_gate, g2, shift2, scale2,
                 w_a, w_b, w_c, w_o, w_r, pos_embed=None):
    b, l, d = x.shape
    e = w_r.shape[1]
    tl = _tile(l, 512)
    row = lambda: pl.BlockSpec((None, tl, d), lambda bb, i: (bb, i, 0))
    sec = lambda s: pl.BlockSpec((None, tl, d), lambda bb, i: (bb, i, s))
    modv = lambda: pl.BlockSpec((None, 1, d), lambda bb, i: (bb, 0, 0))
    wsp = lambda: _resident((d, d), lambda bb, i: (0, 0))
    has_pos = pos_embed is not None
    pos_spec = [pl.BlockSpec((tl, d), lambda bb, i: (i, 0))] if has_pos else []
    pos_arg = [pos_embed] if has_pos else []
    return pl.pallas_call(
        functools.partial(_merge_body, has_pos_embed=has_pos),
        grid=(b, l // tl),
        in_specs=[row(), row(), sec(lg_sec), row(), row(),
                  sec(gate_sec), sec(gate_sec + 1), sec(gate_sec + 2), row()] + pos_spec
        + [modv(), pl.BlockSpec((1, d), lambda bb, i: (0, 0)), modv(), modv(),
           wsp(), wsp(), wsp(), wsp(), _resident((d, e), lambda bb, i: (0, 0))],
        out_specs=[row(), row(), pl.BlockSpec((None, tl, e), lambda bb, i: (bb, i, 0))],
        out_shape=[jax.ShapeDtypeStruct((b, l, d), F32),
                   jax.ShapeDtypeStruct((b, l, d), BF16),
                   jax.ShapeDtypeStruct((b, l, e), F32)],
        compiler_params=_cparams(2),
        name="merge",
    )(ya, yc, proj, hf, hb, proj, proj, proj, x, *pos_arg, mod_gate, g2.reshape(1, d), shift2, scale2,
      w_a, w_b, w_c, w_o, w_r)


def _select_body(aff_ref, tri_ref, rank_ref, *, cap):
    bits = lax.bitcast_convert_type(aff_ref[...], jnp.int32)

    def refine(i, thr):
        cand = thr | jnp.left_shift(jnp.int32(1), 30 - i)
        cnt = jnp.sum((bits >= cand).astype(F32), axis=1, keepdims=True)
        return jnp.where(cnt >= cap, cand, thr)

    thr = lax.fori_loop(0, 31, refine, jnp.zeros((bits.shape[0], 1), jnp.int32))
    gt = bits > thr
    eq = bits == thr
    n_gt = jnp.sum(gt.astype(F32), axis=1, keepdims=True)
    eq_rank = jnp.dot(eq.astype(BF16), tri_ref[...], preferred_element_type=F32)
    sel = gt | (eq & (eq_rank <= cap - n_gt))
    rank = jnp.dot(sel.astype(BF16), tri_ref[...], preferred_element_type=F32) - 1.0
    rank_ref[...] = jnp.where(sel, rank, -1.0).astype(jnp.int32)


def select_tokens(aff_t, cap):
    r, n = aff_t.shape
    tri = jnp.asarray(_prefix_table(n), BF16)
    return pl.pallas_call(
        functools.partial(_select_body, cap=cap),
        grid=(1,),
        in_specs=[pl.BlockSpec((r, n), lambda i: (0, 0)), pl.BlockSpec((n, n), lambda i: (0, 0))],
        out_specs=pl.BlockSpec((r, n), lambda i: (0, 0)),
        out_shape=jax.ShapeDtypeStruct((r, n), jnp.int32),
        compiler_params=_cparams(1),
        name="select",
    )(aff_t, tri)


GATHER_GROUP = 8


def _gather_body(rank_ref, hs_ref, o_ref):
    grp, cap, d = o_ref.shape
    n = hs_ref.shape[0]
    g0 = pl.multiple_of(pl.program_id(1) * grp, grp)
    ranks = rank_ref[pl.ds(g0, grp), :]
    slot = lax.broadcasted_iota(jnp.int32, (cap, n), 0)
    onehot = jnp.concatenate([jnp.where(slot == ranks[e:e + 1, :], 1.0, 0.0).astype(BF16) for e in range(grp)],
                             axis=0)
    rows = jnp.dot(onehot, hs_ref[...], preferred_element_type=F32)
    o_ref[...] = rows.reshape(grp, cap, d).astype(o_ref.dtype)


def gather_tokens(rank_t, hs, cap):
    s, e, n = rank_t.shape
    d = hs.shape[2]
    return pl.pallas_call(
        _gather_body,
        grid=(s, e // GATHER_GROUP),
        in_specs=[pl.BlockSpec((None, e, n), lambda ss, ee: (ss, 0, 0)),
                  pl.BlockSpec((None, n, d), lambda ss, ee: (ss, 0, 0))],
        out_specs=pl.BlockSpec((GATHER_GROUP, cap, d), lambda ss, ee: (ee, ss, 0)),
        out_shape=jax.ShapeDtypeStruct((e, s * cap, d), BF16),
        compiler_params=_cparams(2),
        name="gather",
    )(rank_t, hs)


FFN_SPLIT = 2
FFN_SPLIT_MIN_ROWS = 1024


def _ffn_body(*refs):
    x_refs, (wg_ref, wu_ref, wd_ref, o_ref, acc_ref) = refs[:-5], refs[-5:]
    f = pl.program_id(1)

    @pl.when(f == 0)
    def _():
        acc_ref[...] = jnp.zeros_like(acc_ref)

    wg = wg_ref[...].astype(BF16)
    wu = wu_ref[...].astype(BF16)
    wd = wd_ref[...].astype(BF16)
    row0 = 0
    for x_ref in x_refs:
        m = x_ref.shape[0]
        split = FFN_SPLIT if m >= FFN_SPLIT_MIN_ROWS else 1
        step = m // split
        for part in range(split):
            x = x_ref[pl.ds(part * step, step), :]
            g = jnp.dot(x, wg, preferred_element_type=F32)
            u = jnp.dot(x, wu, preferred_element_type=F32)
            hid = (g * _sigmoid(g) * u).astype(BF16)
            acc_ref[pl.ds(row0 + part * step, step), :] += jnp.dot(hid, wd, preferred_element_type=F32)
        row0 += m

    @pl.when(f == pl.num_programs(1) - 1)
    def _():
        o_ref[...] = acc_ref[...].astype(o_ref.dtype)


def expert_ffn(xs_list, wg, wu, wd, layer):
    e, _, d = xs_list[0].shape
    m = sum(x.shape[1] for x in xs_list)
    fdim = wg.shape[3]
    tf = _tile(fdim, 256)
    return pl.pallas_call(
        _ffn_body,
        grid=(e, fdim // tf),
        in_specs=[pl.BlockSpec((None, x.shape[1], d), lambda ee, f: (ee, 0, 0)) for x in xs_list]
        + [pl.BlockSpec((None, None, d, tf), lambda ee, f: (layer, ee, 0, f)),
           pl.BlockSpec((None, None, d, tf), lambda ee, f: (layer, ee, 0, f)),
           pl.BlockSpec((None, None, tf, d), lambda ee, f: (layer, ee, f, 0))],
        out_specs=pl.BlockSpec((None, m, d), lambda ee, f: (ee, 0, 0)),
        out_shape=jax.ShapeDtypeStruct((e, m, d), BF16),
        scratch_shapes=[pltpu.VMEM((m, d), F32)],
        compiler_params=_cparams(2),
        name="ffn",
    )(*xs_list, wg, wu, wd)


def _scatter_body(rank_ref, aff_ref, ys_ref, x_ref, mg_ref, gf_ref, o_ref, *, final_norm):
    tq = rank_ref.shape[0]
    n_exp, cap, _ = ys_ref.shape
    slot = lax.broadcasted_iota(jnp.int32, (tq, cap), 1)
    rank = rank_ref[...]
    aff = aff_ref[...]
    acc = None
    for e in range(n_exp):
        q = jnp.where(slot == rank[:, e:e + 1], aff[:, e:e + 1], 0.0).astype(BF16)
        part = jnp.dot(q, ys_ref[e], preferred_element_type=F32)
        acc = part if acc is None else acc + part
    x = x_ref[...] + mg_ref[...] * acc
    if final_norm:
        x = x * lax.rsqrt(jnp.mean(x * x, axis=-1, keepdims=True) + NORM_EPS) * gf_ref[...]
    o_ref[...] = x


def scatter_residual(rank, aff, ys, row_off, cap, x, mod_gate, final_g):
    s, n, e = rank.shape
    d = x.shape[2]
    tq = _tile(n, 1024)
    boff = row_off // cap
    final_norm = final_g is not None
    gf = (final_g if final_norm else jnp.ones((d,), F32)).reshape(1, d)
    return pl.pallas_call(
        functools.partial(_scatter_body, final_norm=final_norm),
        grid=(s, n // tq),
        in_specs=[pl.BlockSpec((None, tq, e), lambda ss, i: (ss, i, 0)),
                  pl.BlockSpec((None, tq, e), lambda ss, i: (ss, i, 0)),
                  pl.BlockSpec((e, cap, d), lambda ss, i: (0, boff + ss, 0)),
                  pl.BlockSpec((None, tq, d), lambda ss, i: (ss, i, 0)),
                  pl.BlockSpec((None, 1, d), lambda ss, i: (ss, 0, 0)),
                  pl.BlockSpec((1, d), lambda ss, i: (0, 0))],
        out_specs=pl.BlockSpec((None, tq, d), lambda ss, i: (ss, i, 0)),
        out_shape=jax.ShapeDtypeStruct(x.shape, F32),
        compiler_params=_cparams(2),
        name="scatter",
    )(rank, aff, ys, x, mod_gate, gf)


def _mixer_and_route(x, h0_f, h0_b, mods, lw, fs, shared_mod=False, pos_embed=None):
    b, l, d = x.shape
    if shared_mod:
        flat = lambda a: a.reshape(1, b * l, a.shape[-1])
        unflat = lambda a: a.reshape(b, l, a.shape[-1])
        mods = [m[:1] for m in mods]
    else:
        flat = unflat = lambda a: a
    if pos_embed is None:
        proj = unflat(norm_proj(flat(x), lw['norm1_g'], mods[0], mods[1], lw['w_in']))
    else:
        proj = norm_proj(x, lw['norm1_g'], mods[0], mods[1], lw['w_in'], pos_embed)
    t1, t2, mid = hy_filter_tables(l, lw['hy_filt_w1'], lw['hy_filt_b1'], lw['hy_filt_w2'], lw['hy_filt_b2'],
                                  lw['hy_filt_w3'], lw['hy_filt_freq'], fs)
    ya = hyena_mixer(proj, lw['hy_conv_w'], lw['hy_conv_b'], lw['hy_bias'], t1, t2, mid, fs)
    hf, hb = lru_mixer(proj, 4, lw['lru_conv_w'], lw['lru_conv_b'], lw['lru_wa'], lw['lru_ba'],
                       lw['lru_wx'], lw['lru_bx'], lw['lru_lambda'], h0_f, h0_b)
    yc = shortconv_mixer(proj, 5, lw['sc_conv_w'])
    x_new, h2, aff = merge_mixers(flat(proj), 8, 3, flat(ya), flat(yc), flat(hf), flat(hb), flat(x), mods[2],
                                  lw['norm2_g'], mods[3], mods[4],
                                  lw['w_hy_out'], lw['w_lru_out'], lw['w_sc_out'], lw['w_o'], lw['w_router'],
                                  pos_embed)
    return unflat(x_new), unflat(h2), unflat(aff), hf[:, -1], hb[:, 0]


def _capacity(n, e):
    return EC_CAPACITY * n // e


def _route(h2, aff):
    s, n, e = aff.shape
    cap = _capacity(n, e)
    aff_t = jnp.transpose(aff, (0, 2, 1))
    rank_t = select_tokens(aff_t.reshape(s * e, n), cap).reshape(s, e, n)
    return gather_tokens(rank_t, h2, cap), jnp.transpose(rank_t, (0, 2, 1))


def kernel(x, c, ctx, c_ctx, w_ada, b_ada, norm1_g, norm2_g, w_in, hy_conv_w, hy_conv_b, hy_filt_w1, hy_filt_b1, hy_filt_w2, hy_filt_b2, hy_filt_w3, hy_filt_freq, hy_bias, lru_conv_w, lru_conv_b, lru_wa, lru_ba, lru_wx, lru_bx, lru_lambda, sc_conv_w, w_hy_out, w_lru_out, w_sc_out, w_o, w_router, w_exp_gate, w_exp_up, w_exp_down, final_norm_g):
    bsz, n_lat, d = x.shape
    n_ctx = ctx.shape[1]
    depth = w_ada.shape[0]
    rows = n_lat // GRID_W

    pos_embed = jnp.asarray(_pos_embed(rows, d))
    xc = ctx
    fs_lat = jnp.asarray(_dft_table(n_lat), BF16)
    fs_ctx = jnp.asarray(_dft_table(n_ctx), BF16)

    cc = jnp.zeros((2 * bsz, d), F32).at[:bsz].set(c).at[bsz].set(c_ctx)
    mods = ada_mods(cc, w_ada, b_ada)
    zeros_state = jnp.zeros((bsz, d), F32)

    for l in range(depth):
        last = l == depth - 1
        lw = {
            'norm1_g': norm1_g[l], 'norm2_g': norm2_g[l], 'w_in': w_in[l].astype(BF16),
            'hy_conv_w': hy_conv_w[l], 'hy_conv_b': hy_conv_b[l],
            'hy_filt_w1': hy_filt_w1[l], 'hy_filt_b1': hy_filt_b1[l], 'hy_filt_w2': hy_filt_w2[l],
            'hy_filt_b2': hy_filt_b2[l], 'hy_filt_w3': hy_filt_w3[l], 'hy_filt_freq': hy_filt_freq[l],
            'hy_bias': hy_bias[l], 'lru_conv_w': lru_conv_w[l], 'lru_conv_b': lru_conv_b[l],
            'lru_wa': lru_wa[l], 'lru_ba': lru_ba[l], 'lru_wx': lru_wx[l], 'lru_bx': lru_bx[l],
            'lru_lambda': lru_lambda[l], 'sc_conv_w': sc_conv_w[l],
            'w_hy_out': w_hy_out[l].astype(BF16), 'w_lru_out': w_lru_out[l].astype(BF16),
            'w_sc_out': w_sc_out[l].astype(BF16), 'w_o': w_o[l].astype(BF16), 'w_router': w_router[l],
        }
        mod_l = [mods[l, :bsz, k * d:(k + 1) * d].reshape(bsz, 1, d) for k in range(6)]
        mod_c = [jnp.broadcast_to(mods[l, bsz, k * d:(k + 1) * d].reshape(1, 1, d), (bsz, 1, d)) for k in range(6)]

        n_exp = w_router.shape[2]
        cap_l = _capacity(n_lat, n_exp)
        cap_c = _capacity(n_ctx, n_exp)
        rows_l = bsz * cap_l
        if last:
            proj_c = norm_proj(xc.reshape(1, bsz * n_ctx, d), lw['norm1_g'], mod_c[0][:1], mod_c[1][:1],
                               lw['w_in'][:, 4 * d:5 * d]).reshape(bsz, n_ctx, d)
            hf_c, hb_c = lru_mixer(proj_c, 0, lw['lru_conv_w'], lw['lru_conv_b'], lw['lru_wa'], lw['lru_ba'],
                                   lw['lru_wx'], lw['lru_bx'], lw['lru_lambda'], zeros_state, zeros_state)
            state_f, state_b = hf_c[:, -1], hb_c[:, 0]
            xs_streams = []
        else:
            xc_mid, h2_c, aff_c, state_f, state_b = _mixer_and_route(xc, zeros_state, zeros_state, mod_c, lw, fs_ctx,
                                                                     shared_mod=True)
            xs_c, rank_c = _route(h2_c, aff_c)
            xs_streams = [xs_c]

        x_mid, h2_l, aff_l, _, _ = _mixer_and_route(x, state_f, state_b, mod_l, lw, fs_lat,
                                                    pos_embed=pos_embed if l == 0 else None)
        xs_l, rank_l = _route(h2_l, aff_l)
        ys = expert_ffn([xs_l] + xs_streams, w_exp_gate, w_exp_up, w_exp_down, l)

        if not last:
            xc = scatter_residual(rank_c, aff_c, ys, rows_l, cap_c, xc_mid, mod_c[5], None)
        x = scatter_residual(rank_l, aff_l, ys, 0, cap_l, x_mid, mod_l[5], final_norm_g if last else None)
    return x
```

```python
import functools
import math

import numpy as np
import jax
import jax.numpy as jnp
from jax import lax
from jax.experimental import pallas as pl
from jax.experimental.pallas import tpu as pltpu

F32 = jnp.float32
BF16 = jnp.bfloat16
HI = lax.Precision.HIGHEST

NORM_EPS = 1e-6
GRID_W = 64
N_EXPERTS = 16
EC_CAPACITY = 2
LRU_C = 8.0
HYENA_ORDER = 2
HYENA_EMB_DIM = 33
HYENA_FAST_DECAY = 0.3
HYENA_SLOW_DECAY = 1.5
HYENA_TARGET = 1e-2
MLP_PAD = 128

V7X_VMEM_BYTES = 64 * 1024 * 1024
VMEM_LIMIT = 56 * 1024 * 1024
assert VMEM_LIMIT < V7X_VMEM_BYTES


def _cparams(n_grid):
    return pltpu.CompilerParams(dimension_semantics=("arbitrary",) * n_grid,
                                vmem_limit_bytes=VMEM_LIMIT)


def _tile(n, pref):
    if n <= pref:
        return n
    t = pref
    while n % t:
        t //= 2
    return t


def _resident(block_shape, index_map):
    return pl.BlockSpec(block_shape, index_map, pipeline_mode=pl.Buffered(1))


@functools.lru_cache(maxsize=None)
def _pos_embed(rows, d):
    def sincos(pos, dim):
        half = dim // 2
        omega = 1.0 / (10000.0 ** (np.arange(half, dtype=np.float64) / half))
        ang = pos[:, None] * omega[None, :]
        return np.concatenate([np.sin(ang), np.cos(ang)], axis=-1)
    half = d // 2
    er = sincos(np.arange(rows, dtype=np.float64), half)
    ec = sincos(np.arange(GRID_W, dtype=np.float64), half)
    emb = np.concatenate([np.broadcast_to(er[:, None, :], (rows, GRID_W, half)),
                          np.broadcast_to(ec[None, :, :], (rows, GRID_W, half))], axis=-1)
    return emb.reshape(rows * GRID_W, d).astype(np.float32)


@functools.lru_cache(maxsize=None)
def _dft_table(L):
    h = L // 2
    k = np.arange(h, dtype=np.int64)
    ang_e = np.pi * ((k[:, None] * (2 * k[None, :])) % (2 * L)).astype(np.float64) / L
    ang_o = np.pi * ((k[:, None] * (2 * k[None, :] + 1)) % (2 * L)).astype(np.float64) / L
    ce, se, co, so = np.cos(ang_e), np.sin(ang_e), np.cos(ang_o), np.sin(ang_o)
    return np.concatenate([ce, se, co, so, co.T, so.T], axis=0).astype(np.float32)


@functools.lru_cache(maxsize=None)
def _filter_feats(L):
    bands = (HYENA_EMB_DIM - 1) // 2
    t01 = np.linspace(0.0, 1.0, L, dtype=np.float32).astype(np.float64)[:, None]
    w = ((2.0 * math.pi / L) * np.arange(L, dtype=np.float32))[:, None].astype(np.float64)
    f = np.linspace(1e-4, bands - 1, bands, dtype=np.float32).astype(np.float64)[None, :]
    feats = np.concatenate([t01, np.cos(f * w), -np.sin(f * w)], axis=-1)
    out = np.zeros((L, MLP_PAD), np.float32)
    out[:, :HYENA_EMB_DIM] = feats
    return out


@functools.lru_cache(maxsize=None)
def _decay_window(L, d):
    max_decay = math.log(HYENA_TARGET) / HYENA_FAST_DECAY
    min_decay = math.log(HYENA_TARGET) / HYENA_SLOW_DECAY
    t01 = np.linspace(0.0, 1.0, L, dtype=np.float32).astype(np.float64)[:, None]
    deltas = np.linspace(min_decay, max_decay, d, dtype=np.float32).astype(np.float64)
    return np.exp(-t01 * np.abs(deltas)[None, :]).astype(np.float32)


@functools.lru_cache(maxsize=None)
def _prefix_table(n):
    return np.triu(np.ones((n, n), np.float32))


def _shift_rows(x, s):
    if s == 0:
        return x
    n = x.shape[0]
    rolled = pltpu.roll(x, s % n, axis=0)
    row = lax.broadcasted_iota(jnp.int32, x.shape, 0)
    keep = row >= s if s > 0 else row < n + s
    return jnp.where(keep, rolled, 0.0)


def _dwconv(x, w_ref, left):
    acc = None
    for k in range(w_ref.shape[0]):
        term = _shift_rows(x, left - k) * w_ref[k:k + 1, :]
        acc = term if acc is None else acc + term
    return acc


def _sigmoid(x):
    return 0.5 * jnp.tanh(0.5 * x) + 0.5


def _split_hi_lo(x):
    hi = x.astype(BF16)
    return hi, (x - hi.astype(F32)).astype(BF16)


def _dot_hi_lo(a, b):
    a_hi, a_lo = _split_hi_lo(a)
    b_hi, b_lo = _split_hi_lo(b)
    return (jnp.dot(a_hi, b_hi, preferred_element_type=F32) + jnp.dot(a_hi, b_lo, preferred_element_type=F32)
            + jnp.dot(a_lo, b_hi, preferred_element_type=F32))


def _rms_modulate(x, g, shift, scale):
    y = x * lax.rsqrt(jnp.mean(x * x, axis=-1, keepdims=True) + NORM_EPS)
    return (y * g) * (1.0 + scale) + shift


def _ada_body(c_ref, w_ref, b_ref, o_ref):
    c = c_ref[...]
    s = c * _sigmoid(c)
    o_ref[...] = jnp.dot(s, w_ref[...], precision=HI, preferred_element_type=F32) + b_ref[...]


def ada_mods(cc, w_ada, b_ada):
    depth, d, n = w_ada.shape
    r = cc.shape[0]
    tn = _tile(n, 1536)
    return pl.pallas_call(
        _ada_body,
        grid=(depth, n // tn),
        in_specs=[pl.BlockSpec((r, d), lambda l, j: (0, 0)),
                  pl.BlockSpec((None, d, tn), lambda l, j: (l, 0, j)),
                  pl.BlockSpec((None, 1, tn), lambda l, j: (l, 0, j))],
        out_specs=pl.BlockSpec((None, r, tn), lambda l, j: (l, 0, j)),
        out_shape=jax.ShapeDtypeStruct((depth, r, n), F32),
        compiler_params=_cparams(2),
        name="ada",
    )(cc, w_ada, b_ada.reshape(depth, 1, n))


def _norm_proj_body(x_ref, g_ref, sh_ref, sc_ref, w_ref, o_ref, h_ref):
    @pl.when(pl.program_id(2) == 0)
    def _():
        h_ref[...] = _rms_modulate(x_ref[...], g_ref[...], sh_ref[...], sc_ref[...]).astype(BF16)

    o_ref[...] = jnp.dot(h_ref[...], w_ref[...], preferred_element_type=F32).astype(o_ref.dtype)


def _embed_norm_proj_body(x_ref, pe_ref, g_ref, sh_ref, sc_ref, w_ref, o_ref, h_ref):
    @pl.when(pl.program_id(2) == 0)
    def _():
        h_ref[...] = _rms_modulate(x_ref[...] + pe_ref[...], g_ref[...], sh_ref[...], sc_ref[...]).astype(BF16)

    o_ref[...] = jnp.dot(h_ref[...], w_ref[...], preferred_element_type=F32).astype(o_ref.dtype)


PROJ_COLS = 2816


def norm_proj(x, g, shift, scale, w, pos_embed=None):
    b, l, d = x.shape
    n = w.shape[1]
    tl = _tile(l, 1024)
    tn = _tile(n, PROJ_COLS)
    row = pl.BlockSpec((None, tl, d), lambda bb, i, j: (bb, i, 0))
    tail = [pl.BlockSpec((1, d), lambda bb, i, j: (0, 0)),
            pl.BlockSpec((None, 1, d), lambda bb, i, j: (bb, 0, 0)),
            pl.BlockSpec((None, 1, d), lambda bb, i, j: (bb, 0, 0)),
            pl.BlockSpec((d, tn), lambda bb, i, j: (0, j))]
    proj_spec = pl.BlockSpec((None, tl, tn), lambda bb, i, j: (bb, i, j))
    proj_shape = jax.ShapeDtypeStruct((b, l, n), BF16)
    common = dict(grid=(b, l // tl, n // tn), scratch_shapes=[pltpu.VMEM((tl, d), BF16)],
                  compiler_params=_cparams(3), name="norm_proj")
    if pos_embed is None:
        return pl.pallas_call(_norm_proj_body, in_specs=[row] + tail, out_specs=proj_spec, out_shape=proj_shape,
                              **common)(x, g.reshape(1, d), shift, scale, w)
    return pl.pallas_call(
        _embed_norm_proj_body,
        in_specs=[row, pl.BlockSpec((tl, d), lambda bb, i, j: (i, 0))] + tail,
        out_specs=proj_spec, out_shape=proj_shape,
        **common)(x, pos_embed, g.reshape(1, d), shift, scale, w)


def _hy_hidden_body(feats_ref, w1_ref, b1_ref, w2_ref, b2_ref, fr_ref, h_ref):
    fr = fr_ref[...]
    h = jnp.sin(fr * (jnp.dot(feats_ref[...], w1_ref[...], precision=HI,
                              preferred_element_type=F32) + b1_ref[...]))
    h_ref[...] = jnp.sin(fr * (jnp.dot(h, w2_ref[...], precision=HI,
                                       preferred_element_type=F32) + b2_ref[...]))


LANES = 128


def _sign_col(rows):
    row = lax.broadcasted_iota(jnp.int32, (rows, 1), 0)
    return (1 - 2 * (row & 1)).astype(F32)


def _put_slabs(nat_ref, r0, rows):
    for s in range(nat_ref.shape[0]):
        nat_ref[s, pl.ds(r0, rows.shape[0]), :] = rows[:, s * LANES:(s + 1) * LANES]


def _get_slabs(nat_ref, r0, n):
    return jnp.concatenate([nat_ref[s, pl.ds(r0, n), :] for s in range(nat_ref.shape[0])], axis=1)


def _split_rows(nat_ref, e_ref, o_ref):
    nslab, L, _ = nat_ref.shape
    h = L // 2
    rd = min(h, 256)
    sgn = _sign_col(rd)

    def body(i, carry):
        m0 = pl.multiple_of(i * rd, rd)
        e = jnp.concatenate([nat_ref[s, pl.ds(2 * m0, rd, stride=2), :] for s in range(nslab)], axis=1)
        o = jnp.concatenate([nat_ref[s, pl.ds(2 * m0 + 1, rd, stride=2), :] for s in range(nslab)], axis=1)
        e_ref[pl.ds(m0, rd), :] = e.astype(BF16)
        o_ref[pl.ds(m0, rd), :] = o.astype(BF16)
        return (carry[0] + jnp.sum(e * sgn, axis=0, keepdims=True),
                carry[1] + jnp.sum(o * sgn, axis=0, keepdims=True))

    zero = jnp.zeros((1, nslab * LANES), F32)
    return lax.fori_loop(0, h // rd, body, (zero, zero))


def _hy_filter_body(h_ref, w3f_ref, w3b_ref, decay_ref, tb_ref, t1_ref, t2_ref, mid_ref,
                    p_ref, q_ref, pe_ref, po_ref, qe_ref, qo_ref):
    L = h_ref.shape[0]
    hl = L // 2
    hid = h_ref[...]
    decay = decay_ref[...]
    fwd = _dot_hi_lo(hid, w3f_ref[...]) * decay
    bwd = _dot_hi_lo(hid, w3b_ref[...]) * decay
    row = lax.broadcasted_iota(jnp.int32, fwd.shape, 0)
    bwd = jnp.where(row >= 1, bwd, 0.0)
    inv = 1.0 / (jnp.sum(jnp.abs(fwd), axis=0, keepdims=True)
                 + jnp.sum(jnp.abs(bwd), axis=0, keepdims=True))
    _put_slabs(p_ref, 0, fwd + bwd)
    _put_slabs(q_ref, 0, bwd - fwd)
    k_re_mid, _ = _split_rows(p_ref, pe_ref, po_ref)
    _, k_im_mid = _split_rows(q_ref, qe_ref, qo_ref)
    a_e = jnp.dot(tb_ref[0:hl, :], pe_ref[...], preferred_element_type=F32)
    a_o = jnp.dot(tb_ref[2 * hl:3 * hl, :], po_ref[...], preferred_element_type=F32)
    b_e = jnp.dot(tb_ref[hl:2 * hl, :], qe_ref[...], preferred_element_type=F32)
    b_o = jnp.dot(tb_ref[3 * hl:4 * hl, :], qo_ref[...], preferred_element_type=F32)
    n_fft = 2.0 * L
    row_h = lax.broadcasted_iota(jnp.int32, (hl, fwd.shape[1]), 0)
    w_k = jnp.where(row_h >= 1, 2.0 / n_fft, 1.0 / n_fft) * inv
    w_s = (2.0 / n_fft) * inv
    t1_ref[0:hl, :] = (a_e + a_o) * w_k
    t1_ref[hl:L, :] = (a_e - a_o) * w_k
    t2_ref[0:hl, :] = (b_e + b_o) * w_s
    t2_ref[hl:L, :] = (b_o - b_e) * w_s
    mid_ref[0:1, :] = k_re_mid * w_s
    mid_ref[1:2, :] = k_im_mid * w_s


def hy_filter_tables(L, w1, b1, w2, b2, w3, freq, fs):
    d = w3.shape[1] // (HYENA_ORDER * 2)
    fw = w1.shape[1]
    pad = lambda a, r, c: jnp.zeros((r, c), F32).at[:a.shape[0], :a.shape[1]].set(a)
    w1p = pad(w1, MLP_PAD, MLP_PAD)
    w2p = pad(w2, MLP_PAD, MLP_PAD)
    w3p = pad(w3, MLP_PAD, w3.shape[1])
    b1p = pad(b1.reshape(1, fw), 1, MLP_PAD)
    b2p = pad(b2.reshape(1, fw), 1, MLP_PAD)
    frp = pad(freq.reshape(1, fw), 1, MLP_PAD)
    feats = jnp.asarray(_filter_feats(L))
    decay = jnp.asarray(_decay_window(L, d))
    tn = _tile(d, 256)
    nj = d // tn
    full = lambda o, j: (0, 0)
    one = lambda i: (0, 0)
    hidden = pl.pallas_call(
        _hy_hidden_body,
        grid=(1,),
        in_specs=[pl.BlockSpec((L, MLP_PAD), one),
                  pl.BlockSpec((MLP_PAD, MLP_PAD), one),
                  pl.BlockSpec((1, MLP_PAD), one),
                  pl.BlockSpec((MLP_PAD, MLP_PAD), one),
                  pl.BlockSpec((1, MLP_PAD), one),
                  pl.BlockSpec((1, MLP_PAD), one)],
        out_specs=pl.BlockSpec((L, MLP_PAD), one),
        out_shape=jax.ShapeDtypeStruct((L, MLP_PAD), F32),
        compiler_params=_cparams(1),
        name="hy_hidden",
    )(feats, w1p, b1p, w2p, b2p, frp)
    hl = L // 2
    t1, t2, mid = pl.pallas_call(
        _hy_filter_body,
        grid=(HYENA_ORDER, nj),
        in_specs=[pl.BlockSpec((L, MLP_PAD), full),
                  pl.BlockSpec((MLP_PAD, tn), lambda o, j: (0, o * 2 * nj + j)),
                  pl.BlockSpec((MLP_PAD, tn), lambda o, j: (0, o * 2 * nj + nj + j)),
                  pl.BlockSpec((L, tn), lambda o, j: (0, j)),
                  _resident(fs.shape, full)],
        out_specs=[pl.BlockSpec((L, tn), lambda o, j: (0, o * nj + j)),
                   pl.BlockSpec((L, tn), lambda o, j: (0, o * nj + j)),
                   pl.BlockSpec((2, tn), lambda o, j: (0, o * nj + j))],
        out_shape=[jax.ShapeDtypeStruct((L, HYENA_ORDER * d), F32),
                   jax.ShapeDtypeStruct((L, HYENA_ORDER * d), F32),
                   jax.ShapeDtypeStruct((2, HYENA_ORDER * d), F32)],
        scratch_shapes=[pltpu.VMEM((tn // LANES, L, LANES), F32), pltpu.VMEM((tn // LANES, L, LANES), F32),
                        pltpu.VMEM((hl, tn), BF16), pltpu.VMEM((hl, tn), BF16),
                        pltpu.VMEM((hl, tn), BF16), pltpu.VMEM((hl, tn), BF16)],
        compiler_params=_cparams(2),
        name="hy_filter",
    )(hidden, w3p, w3p, decay, fs)
    return t1, t2, mid


HALO = 8


def _chunks(n_rows, chunk, fn):
    def body(c, carry):
        fn(pl.multiple_of(c * chunk, chunk))
        return carry
    lax.fori_loop(0, n_rows // chunk, body, 0)


def _conv3_even_odd(src_ref, w_ref, b_ref, pad_ref, emit):
    assert w_ref.shape[0] == 3
    L = src_ref.shape[0]
    nslab = pad_ref.shape[0]
    h = L // 2
    rc = min(L, 256)
    rd = min(h, 128)
    zeros = jnp.zeros((HALO, LANES), F32)
    for s in range(nslab):
        pad_ref[s, 0:HALO, :] = zeros
        pad_ref[s, L + HALO:L + 2 * HALO, :] = zeros

    def fill(r0):
        _put_slabs(pad_ref, r0 + HALO, src_ref[pl.ds(r0, rc), :].astype(F32))
    _chunks(L, rc, fill)

    def taps(m0, off):
        return jnp.concatenate([pad_ref[s, pl.ds(2 * m0 + (HALO - 1 + off), rd, stride=2), :]
                                for s in range(nslab)], axis=1)

    def conv(m0):
        t0, t1, t2, t3 = (taps(m0, off) for off in range(4))
        w0, w1, w2 = w_ref[0:1, :], w_ref[1:2, :], w_ref[2:3, :]
        bias = b_ref[...]
        emit(m0, t0 * w0 + t1 * w1 + t2 * w2 + bias, t1 * w0 + t2 * w1 + t3 * w2 + bias)
    _chunks(h, rd, conv)


def _long_conv_even_odd(u_refs, t1_ref, t2_ref, mid_ref, skip, tb_ref, spec_e_ref, spec_o_ref,
                        xec_ref, xes_ref, xoc_ref, xos_ref, emit):
    ue32_ref, uo32_ref, ue_ref, uo_ref, umid_ref = u_refs
    h = ue_ref.shape[0]
    rm = min(h, 64)
    ri = min(h, 1024)

    spec_e_ref[...] = jnp.dot(tb_ref[0:2 * h, :], ue_ref[...], preferred_element_type=F32)
    spec_o_ref[...] = jnp.dot(tb_ref[2 * h:4 * h, :], uo_ref[...], preferred_element_type=F32)

    def mix(r0):
        ae = spec_e_ref[pl.ds(r0, rm), :]
        be = spec_e_ref[pl.ds(h + r0, rm), :]
        ao = spec_o_ref[pl.ds(r0, rm), :]
        bo = spec_o_ref[pl.ds(h + r0, rm), :]
        p0, p1, q0, q1 = ae + ao, ae - ao, be + bo, bo - be
        t1_lo = t1_ref[pl.ds(r0, rm), :]
        t2_lo = t2_ref[pl.ds(r0, rm), :]
        t1_hi = t1_ref[pl.ds(h + r0, rm), :]
        t2_hi = t2_ref[pl.ds(h + r0, rm), :]
        za0 = p0 * t1_lo + q0 * t2_lo
        zb0 = q0 * t1_lo - p0 * t2_lo
        za1 = p1 * t1_hi + q1 * t2_hi
        zb1 = q1 * t1_hi - p1 * t2_hi
        xec_ref[pl.ds(r0, rm), :] = (za0 + za1).astype(BF16)
        xes_ref[pl.ds(r0, rm), :] = (zb0 - zb1).astype(BF16)
        xoc_ref[pl.ds(r0, rm), :] = (za0 - za1).astype(BF16)
        xos_ref[pl.ds(r0, rm), :] = (zb0 + zb1).astype(BF16)
    _chunks(h, rm, mix)

    ua_mid = umid_ref[0:1, :]
    ub_mid = umid_ref[1:2, :]
    t1_mid = mid_ref[0:1, :]
    t2_mid = mid_ref[1:2, :]
    za_mid = ua_mid * t1_mid + ub_mid * t2_mid
    zb_mid = ub_mid * t1_mid - ua_mid * t2_mid
    sgn = _sign_col(ri)

    def inverse(r0):
        ye = jnp.dot(tb_ref[pl.ds(r0, ri), :], xec_ref[...], preferred_element_type=F32)
        ye = ye + jnp.dot(tb_ref[pl.ds(h + r0, ri), :], xes_ref[...], preferred_element_type=F32)
        yo = jnp.dot(tb_ref[pl.ds(4 * h + r0, ri), :], xoc_ref[...], preferred_element_type=F32)
        yo = yo + jnp.dot(tb_ref[pl.ds(5 * h + r0, ri), :], xos_ref[...], preferred_element_type=F32)
        emit(r0, ye + sgn * za_mid + ue32_ref[pl.ds(r0, ri), :] * skip,
             yo + sgn * zb_mid + uo32_ref[pl.ds(r0, ri), :] * skip)
    _chunks(h, ri, inverse)


def _hyena_body(p1_ref, p2_ref, pv_ref, cw1_ref, cw2_ref, cwv_ref, cb1_ref, cb2_ref, cbv_ref,
                bias_ref, t1a_ref, t2a_ref, mida_ref, t1b_ref, t2b_ref, midb_ref, tb_ref, o_ref,
                pad_ref, x1e_ref, x1o_ref, x2e_ref, x2o_ref, ue32_ref, uo32_ref, ue_ref, uo_ref, umid_ref,
                spec_e_ref, spec_o_ref, xec_ref, xes_ref, xoc_ref, xos_ref, out_ref):
    L = p1_ref.shape[0]

    def put(e_ref, o_ref_):
        def emit(m0, e, o):
            e_ref[pl.ds(m0, e.shape[0]), :] = e
            o_ref_[pl.ds(m0, o.shape[0]), :] = o
        return emit

    def put_u(m0, e, o):
        n = e.shape[0]
        ue32_ref[pl.ds(m0, n), :] = e
        uo32_ref[pl.ds(m0, n), :] = o
        ue_ref[pl.ds(m0, n), :] = e.astype(BF16)
        uo_ref[pl.ds(m0, n), :] = o.astype(BF16)
        sgn = _sign_col(n)
        umid_ref[0:1, :] += jnp.sum(e * sgn, axis=0, keepdims=True)
        umid_ref[1:2, :] += jnp.sum(o * sgn, axis=0, keepdims=True)

    _conv3_even_odd(p1_ref, cw1_ref, cb1_ref, pad_ref, put(x1e_ref, x1o_ref))
    _conv3_even_odd(p2_ref, cw2_ref, cb2_ref, pad_ref, put(x2e_ref, x2o_ref))
    umid_ref[...] = jnp.zeros_like(umid_ref)
    _conv3_even_odd(pv_ref, cwv_ref, cbv_ref, pad_ref, put_u)
    u_refs = (ue32_ref, uo32_ref, ue_ref, uo_ref, umid_ref)
    work = (tb_ref, spec_e_ref, spec_o_ref, xec_ref, xes_ref, xoc_ref, xos_ref)

    def first(m0, ce, co):
        @pl.when(m0 == 0)
        def _():
            umid_ref[...] = jnp.zeros_like(umid_ref)
        n = ce.shape[0]
        put_u(m0, x1e_ref[pl.ds(m0, n), :] * ce, x1o_ref[pl.ds(m0, n), :] * co)
    _long_conv_even_odd(u_refs, t1a_ref, t2a_ref, mida_ref, bias_ref[0:1, :], *work, first)

    def second(m0, ce, co):
        n = ce.shape[0]
        oe = x2e_ref[pl.ds(m0, n), :] * ce
        oo = x2o_ref[pl.ds(m0, n), :] * co
        for s in range(out_ref.shape[0]):
            out_ref[s, pl.ds(2 * m0, n, stride=2), :] = oe[:, s * LANES:(s + 1) * LANES]
            out_ref[s, pl.ds(2 * m0 + 1, n, stride=2), :] = oo[:, s * LANES:(s + 1) * LANES]
    _long_conv_even_odd(u_refs, t1b_ref, t2b_ref, midb_ref, bias_ref[1:2, :], *work, second)

    rc = min(L, 256)

    def finish(r0):
        o_ref[pl.ds(r0, rc), :] = _get_slabs(out_ref, r0, rc).astype(o_ref.dtype)
    _chunks(L, rc, finish)


def hyena_mixer(proj, conv_w, conv_b, bias, t1, t2, mid, fs):
    b, l, _ = proj.shape
    d = bias.shape[1]
    tn = _tile(d, 256)
    nj = d // tn
    k = conv_w.shape[0]
    hl = l // 2
    cb = conv_b.reshape(1, 3 * d)
    sec = lambda s: pl.BlockSpec((None, l, tn), lambda j, bb: (bb, 0, s * nj + j))
    cws = lambda s: pl.BlockSpec((k, tn), lambda j, bb: (0, s * nj + j))
    cbs = lambda s: pl.BlockSpec((1, tn), lambda j, bb: (0, s * nj + j))
    tab = lambda o, r: _resident((r, tn), lambda j, bb: (0, o * nj + j))
    half_f32 = pltpu.VMEM((hl, tn), F32)
    half_bf16 = pltpu.VMEM((hl, tn), BF16)
    return pl.pallas_call(
        _hyena_body,
        grid=(nj, b),
        in_specs=[sec(0), sec(1), sec(2), cws(0), cws(1), cws(2), cbs(0), cbs(1), cbs(2),
                  pl.BlockSpec((HYENA_ORDER, tn), lambda j, bb: (0, j)),
                  tab(0, l), tab(0, l), tab(0, 2), tab(1, l), tab(1, l), tab(1, 2),
                  _resident(fs.shape, lambda j, bb: (0, 0))],
        out_specs=pl.BlockSpec((None, l, tn), lambda j, bb: (bb, 0, j)),
        out_shape=jax.ShapeDtypeStruct((b, l, d), BF16),
        scratch_shapes=[pltpu.VMEM((tn // LANES, l + 2 * HALO, LANES), F32),
                        half_f32, half_f32, half_f32, half_f32,
                        half_f32, half_f32, half_bf16, half_bf16,
                        pltpu.VMEM((8, tn), F32),
                        pltpu.VMEM((l, tn), F32), pltpu.VMEM((l, tn), F32),
                        half_bf16, half_bf16, half_bf16, half_bf16,
                        pltpu.VMEM((tn // LANES, l, LANES), F32)],
        compiler_params=_cparams(2),
        name="hyena",
    )(proj, proj, proj, conv_w, conv_w, conv_w, cb, cb, cb, bias, t1, t2, mid, t1, t2, mid, fs)


SQRT_GUARD = 1e-30
LRU_CONV_LEFT = 2
LRU_GATE_TILE = 256
LRU_LANES = 512


def _dwconv_padded(src_ref, cols, w_ref, left, pad_ref):
    L = src_ref.shape[0]
    nslab = pad_ref.shape[0]
    zeros = jnp.zeros((HALO, LANES), F32)
    for s in range(nslab):
        pad_ref[s, 0:HALO, :] = zeros
        pad_ref[s, L + HALO:L + 2 * HALO, :] = zeros
    _put_slabs(pad_ref, HALO, src_ref[:, cols].astype(F32))
    acc = None
    for k in range(w_ref.shape[0]):
        tap = jnp.concatenate([pad_ref[s, pl.ds(HALO - left + k, L, stride=1), :] for s in range(nslab)], axis=1)
        term = tap * w_ref[k:k + 1, cols]
        acc = term if acc is None else acc + term
    return acc


def _lru_gates(rec_ref, cw_ref, cb_ref, wa_ref, ba_ref, wx_ref, bx_ref, lam_ref, store, pad_ref):
    tg = wa_ref.shape[-1]
    for t in range(rec_ref.shape[1] // tg):
        cols = slice(t * tg, (t + 1) * tg)
        xc = _dwconv_padded(rec_ref, cols, cw_ref, LRU_CONV_LEFT, pad_ref) + cb_ref[:, cols]
        xcb = xc.astype(BF16)
        xh = 0.5 * xc
        for dr in range(2):
            tr = jnp.tanh(jnp.dot(xcb, wa_ref[dr, t], preferred_element_type=F32) + 0.5 * ba_ref[dr:dr + 1, cols])
            ti = jnp.tanh(jnp.dot(xcb, wx_ref[dr, t], preferred_element_type=F32) + 0.5 * bx_ref[dr:dr + 1, cols])
            ch = (-0.5 * LRU_C * math.log2(math.e)) * jnp.log1p(jnp.exp(-lam_ref[dr:dr + 1, cols]))
            a = jnp.exp2(tr * ch + ch)
            om = 1.0 - a * a
            store(dr, t, a, (om * lax.rsqrt(jnp.maximum(om, SQRT_GUARD))) * (ti * xh + xh))


def _scan_window4(a_ref, b_ref, L, reverse):
    rc = min(L, 256)
    sgn = 1 if reverse else -1
    nchunk = L // rc

    def chunk(c, carry):
        r0 = pl.multiple_of((c if reverse else nchunk - 1 - c) * rc, rc) + HALO
        for s in range(a_ref.shape[0]):
            a0, a1, a2, a3 = (a_ref[s, pl.ds(r0 + sgn * k, rc, stride=1), :] for k in range(4))
            b0, b1, b2, b3 = (b_ref[s, pl.ds(r0 + sgn * k, rc, stride=1), :] for k in range(4))
            b_ref[s, pl.ds(r0, rc), :] = b0 + a0 * (b1 + a1 * (b2 + a2 * b3))
            a_ref[s, pl.ds(r0, rc), :] = (a0 * a1) * (a2 * a3)
        return carry
    lax.fori_loop(0, nchunk, chunk, 0)


def _lru_body(rec_ref, cw_ref, cb_ref, wa_ref, ba_ref, wx_ref, bx_ref, lam_ref, h0f_ref, h0b_ref,
              hf_ref, hb_ref, af_ref, bf_ref, ab_ref, bb_ref, pad_ref):
    L = rec_ref.shape[0]
    tg = wa_ref.shape[-1]
    spt = tg // LANES
    one = jnp.ones((HALO, LANES), F32)
    zero = jnp.zeros((HALO, LANES), F32)
    for a_ref, b_ref in ((af_ref, bf_ref), (ab_ref, bb_ref)):
        for s in range(a_ref.shape[0]):
            for lo in (0, L + HALO):
                a_ref[s, lo:lo + HALO, :] = one
                b_ref[s, lo:lo + HALO, :] = zero

    def store(dr, t, a, b):
        a_ref, b_ref = ((af_ref, bf_ref), (ab_ref, bb_ref))[dr]
        for s in range(spt):
            a_ref[t * spt + s, HALO:HALO + L, :] = a[:, s * LANES:(s + 1) * LANES]
            b_ref[t * spt + s, HALO:HALO + L, :] = b[:, s * LANES:(s + 1) * LANES]

    _lru_gates(rec_ref, cw_ref, cb_ref, wa_ref, ba_ref, wx_ref, bx_ref, lam_ref, store, pad_ref)
    _scan_window4(af_ref, bf_ref, L, False)
    _scan_window4(ab_ref, bb_ref, L, True)
    groups = L // 8
    nslab = af_ref.shape[0]

    def step(g, carry):
        rf = pl.multiple_of(g * 8, 8)
        rb = pl.multiple_of((groups - 1 - g) * 8, 8)
        new = []
        for s in range(nslab):
            lanes = slice(s * LANES, (s + 1) * LANES)
            a4, b4 = af_ref[s, pl.ds(rf + HALO, 8), :], bf_ref[s, pl.ds(rf + HALO, 8), :]
            a4p = af_ref[s, pl.ds(rf + HALO - 4, 8, stride=1), :]
            b4p = bf_ref[s, pl.ds(rf + HALO - 4, 8, stride=1), :]
            hf = (a4 * a4p) * carry[0][s] + (a4 * b4p + b4)
            hf_ref[pl.ds(rf, 8), lanes] = hf
            a4, b4 = ab_ref[s, pl.ds(rb + HALO, 8), :], bb_ref[s, pl.ds(rb + HALO, 8), :]
            a4n = ab_ref[s, pl.ds(rb + HALO + 4, 8, stride=1), :]
            b4n = bb_ref[s, pl.ds(rb + HALO + 4, 8, stride=1), :]
            hb = (a4 * a4n) * carry[1][s] + (a4 * b4n + b4)
            hb_ref[pl.ds(rb, 8), lanes] = hb
            new.append((hf, hb))
        return tuple(n[0] for n in new), tuple(n[1] for n in new)

    def tiles(h0_ref):
        return tuple(jnp.broadcast_to(h0_ref[:, s * LANES:(s + 1) * LANES], (8, LANES)) for s in range(nslab))

    lax.fori_loop(0, groups, step, (tiles(h0f_ref), tiles(h0b_ref)))


def _block_diag_tiles(w, tn):
    two, h, bs, _ = w.shape
    hpt = tn // bs
    wt = w.reshape(two, h // hpt, hpt, bs, bs)
    eye = jnp.eye(hpt, dtype=w.dtype)
    dense = jnp.einsum('tnhij,hg->tnhigj', wt, eye)
    return dense.reshape(two, h // hpt, tn, tn)


def lru_mixer(proj, rec_sec, conv_w, conv_b, wa, ba, wx, bx, lam, h0_f, h0_b):
    b, l, n = proj.shape
    d = conv_w.shape[1]
    tg = _tile(d, LRU_GATE_TILE)
    tn = _tile(d, LRU_LANES)
    nj = d // tn
    tpj = tn // tg
    k = conv_w.shape[0]
    wa_t = (0.5 * _block_diag_tiles(wa, tg)).astype(BF16)
    wx_t = (0.5 * _block_diag_tiles(wx, tg)).astype(BF16)
    vec = lambda r: pl.BlockSpec((r, tn), lambda bb, j: (0, j))
    wsp = pl.BlockSpec((2, tpj, tg, tg), lambda bb, j: (0, j, 0, 0))
    h0s = pl.BlockSpec((None, 1, tn), lambda bb, j: (bb, 0, j))
    out = pl.BlockSpec((None, l, tn), lambda bb, j: (bb, 0, j))
    shp = jax.ShapeDtypeStruct((b, l, d), F32)
    coef = pltpu.VMEM((tn // LANES, l + 2 * HALO, LANES), F32)
    return pl.pallas_call(
        _lru_body,
        grid=(b, nj),
        in_specs=[pl.BlockSpec((None, l, tn), lambda bb, j: (bb, 0, rec_sec * nj + j)),
                  vec(k), vec(1), wsp, vec(2), wsp, vec(2), vec(2), h0s, h0s],
        out_specs=[out, out],
        out_shape=[shp, shp],
        scratch_shapes=[coef, coef, coef, coef,
                        pltpu.VMEM((tg // LANES, l + 2 * HALO, LANES), F32)],
        compiler_params=_cparams(2),
        name="lru",
    )(proj, conv_w, conv_b.reshape(1, d), wa_t, ba, wx_t, bx, lam, h0_f.reshape(b, 1, d), h0_b.reshape(b, 1, d))


def _shortconv_body(bg_ref, cg_ref, xv_ref, w_ref, o_ref):
    prod = cg_ref[...].astype(F32) * xv_ref[...].astype(F32)
    o_ref[...] = (bg_ref[...].astype(F32) * _dwconv(prod, w_ref, 1)).astype(o_ref.dtype)


def shortconv_mixer(proj, first_sec, conv_w):
    b, l, _ = proj.shape
    k, d = conv_w.shape
    tn = _tile(d, 256)
    nj = d // tn
    sec = lambda s: pl.BlockSpec((None, l, tn), lambda bb, j: (bb, 0, (first_sec + s) * nj + j))
    return pl.pallas_call(
        _shortconv_body,
        grid=(b, nj),
        in_specs=[sec(0), sec(1), sec(2), pl.BlockSpec((k, tn), lambda bb, j: (0, j))],
        out_specs=pl.BlockSpec((None, l, tn), lambda bb, j: (bb, 0, j)),
        out_shape=jax.ShapeDtypeStruct((b, l, d), BF16),
        compiler_params=_cparams(2),
        name="shortconv",
    )(proj, proj, proj, conv_w)


MERGE_SPLIT = 2


def _gelu_tanh(x):
    return 0.5 * x * (1.0 + jnp.tanh(math.sqrt(2.0 / math.pi) * (x + 0.044715 * (x * x * x))))


def _merge_body(*refs, has_pos_embed):
    (ya_ref, yc_ref, lg_ref, hf_ref, hb_ref, ga_ref, gb_ref, gc_ref, x_ref), refs = refs[:9], refs[9:]
    if has_pos_embed:
        pe_ref, refs = refs[0], refs[1:]
    (mg_ref, g2_ref, sh_ref, sc_ref, wa_ref, wb_ref, wc_ref, wo_ref, wr_ref, xo_ref, h2_ref, aff_ref) = refs
    tl = x_ref.shape[0]
    half = tl // MERGE_SPLIT
    wr = wr_ref[...]
    wr_hi = wr.astype(BF16)
    wr_lo = (wr - wr_hi.astype(F32)).astype(BF16)
    for part in range(MERGE_SPLIT):
        rows = pl.ds(part * half, half)
        yb = (_gelu_tanh(lg_ref[rows, :].astype(F32)) * (hf_ref[rows, :] + hb_ref[rows, :])).astype(BF16)
        merged = _sigmoid(ga_ref[rows, :].astype(F32)) * jnp.dot(ya_ref[rows, :], wa_ref[...], preferred_element_type=F32)
        merged += _sigmoid(gb_ref[rows, :].astype(F32)) * jnp.dot(yb, wb_ref[...], preferred_element_type=F32)
        merged += _sigmoid(gc_ref[rows, :].astype(F32)) * jnp.dot(yc_ref[rows, :], wc_ref[...], preferred_element_type=F32)
        out = jnp.dot(merged.astype(BF16), wo_ref[...], preferred_element_type=F32)
        x_in = x_ref[rows, :] + pe_ref[rows, :] if has_pos_embed else x_ref[rows, :]
        x = x_in + mg_ref[...] * out
        xo_ref[rows, :] = x
        h2 = _rms_modulate(x, g2_ref[...], sh_ref[...], sc_ref[...])
        h2_hi = h2.astype(BF16)
        h2_ref[rows, :] = h2_hi
        h2_lo = (h2 - h2_hi.astype(F32)).astype(BF16)
        logits = (jnp.dot(h2_hi, wr_hi, preferred_element_type=F32)
                  + jnp.dot(h2_hi, wr_lo, preferred_element_type=F32)
                  + jnp.dot(h2_lo, wr_hi, preferred_element_type=F32))
        e = jnp.exp(logits - jnp.max(logits, axis=-1, keepdims=True))
        aff_ref[rows, :] = e / jnp.sum(e, axis=-1, keepdims=True)


def merge_mixers(proj, gate_sec, lg_sec, ya, yc, hf, hb, x, mod_gate, g2, shift2, scale2,
                 w_a, w_b, w_c, w_o, w_r, pos_embed=None):
    b, l, d = x.shape
    e = w_r.shape[1]
    tl = _tile(l, 512)
    row = lambda: pl.BlockSpec((None, tl, d), lambda bb, i: (bb, i, 0))
    sec = lambda s: pl.BlockSpec((None, tl, d), lambda bb, i: (bb, i, s))
    modv = lambda: pl.BlockSpec((None, 1, d), lambda bb, i: (bb, 0, 0))
    wsp = lambda: _resident((d, d), lambda bb, i: (0, 0))
    has_pos = pos_embed is not None
    pos_spec = [pl.BlockSpec((tl, d), lambda bb, i: (i, 0))] if has_pos else []
    pos_arg = [pos_embed] if has_pos else []
    return pl.pallas_call(
        functools.partial(_merge_body, has_pos_embed=has_pos),
        grid=(b, l // tl),
        in_specs=[row(), row(), sec(lg_sec), row(), row(),
                  sec(gate_sec), sec(gate_sec + 1), sec(gate_sec + 2), row()] + pos_spec
        + [modv(), pl.BlockSpec((1, d), lambda bb, i: (0, 0)), modv(), modv(),
           wsp(), wsp(), wsp(), wsp(), _resident((d, e), lambda bb, i: (0, 0))],
        out_specs=[row(), row(), pl.BlockSpec((None, tl, e), lambda bb, i: (bb, i, 0))],
        out_shape=[jax.ShapeDtypeStruct((b, l, d), F32),
                   jax.ShapeDtypeStruct((b, l, d), BF16),
                   jax.ShapeDtypeStruct((b, l, e), F32)],
        compiler_params=_cparams(2),
        name="merge",
    )(ya, yc, proj, hf, hb, proj, proj, proj, x, *pos_arg, mod_gate, g2.reshape(1, d), shift2, scale2,
      w_a, w_b, w_c, w_o, w_r)


def _select_body(aff_ref, tri_ref, rank_ref, *, cap):
    bits = lax.bitcast_convert_type(aff_ref[...], jnp.int32)

    def refine(i, thr):
        cand = thr | jnp.left_shift(jnp.int32(1), 30 - i)
        cnt = jnp.sum((bits >= cand).astype(F32), axis=1, keepdims=True)
        return jnp.where(cnt >= cap, cand, thr)

    thr = lax.fori_loop(0, 31, refine, jnp.zeros((bits.shape[0], 1), jnp.int32))
    gt = bits > thr
    eq = bits == thr
    n_gt = jnp.sum(gt.astype(F32), axis=1, keepdims=True)
    eq_rank = jnp.dot(eq.astype(BF16), tri_ref[...], preferred_element_type=F32)
    sel = gt | (eq & (eq_rank <= cap - n_gt))
    rank = jnp.dot(sel.astype(BF16), tri_ref[...], preferred_element_type=F32) - 1.0
    rank_ref[...] = jnp.where(sel, rank, -1.0).astype(jnp.int32)


def select_tokens(aff_t, cap):
    r, n = aff_t.shape
    tri = jnp.asarray(_prefix_table(n), BF16)
    return pl.pallas_call(
        functools.partial(_select_body, cap=cap),
        grid=(1,),
        in_specs=[pl.BlockSpec((r, n), lambda i: (0, 0)), pl.BlockSpec((n, n), lambda i: (0, 0))],
        out_specs=pl.BlockSpec((r, n), lambda i: (0, 0)),
        out_shape=jax.ShapeDtypeStruct((r, n), jnp.int32),
        compiler_params=_cparams(1),
        name="select",
    )(aff_t, tri)


GATHER_GROUP = 8


def _gather_body(rank_ref, hs_ref, o_ref):
    grp, cap, d = o_ref.shape
    n = hs_ref.shape[0]
    g0 = pl.multiple_of(pl.program_id(1) * grp, grp)
    ranks = rank_ref[pl.ds(g0, grp), :]
    slot = lax.broadcasted_iota(jnp.int32, (cap, n), 0)
    onehot = jnp.concatenate([jnp.where(slot == ranks[e:e + 1, :], 1.0, 0.0).astype(BF16) for e in range(grp)],
                             axis=0)
    rows = jnp.dot(onehot, hs_ref[...], preferred_element_type=F32)
    o_ref[...] = rows.reshape(grp, cap, d).astype(o_ref.dtype)


def gather_tokens(rank_t, hs, cap):
    s, e, n = rank_t.shape
    d = hs.shape[2]
    return pl.pallas_call(
        _gather_body,
        grid=(s, e // GATHER_GROUP),
        in_specs=[pl.BlockSpec((None, e, n), lambda ss, ee: (ss, 0, 0)),
                  pl.BlockSpec((None, n, d), lambda ss, ee: (ss, 0, 0))],
        out_specs=pl.BlockSpec((GATHER_GROUP, cap, d), lambda ss, ee: (ee, ss, 0)),
        out_shape=jax.ShapeDtypeStruct((e, s * cap, d), BF16),
        compiler_params=_cparams(2),
        name="gather",
    )(rank_t, hs)


FFN_SPLIT = 2
FFN_SPLIT_MIN_ROWS = 1024


def _ffn_body(*refs):
    x_refs, (wg_ref, wu_ref, wd_ref, o_ref, acc_ref) = refs[:-5], refs[-5:]
    f = pl.program_id(1)

    @pl.when(f == 0)
    def _():
        acc_ref[...] = jnp.zeros_like(acc_ref)

    wg = wg_ref[...].astype(BF16)
    wu = wu_ref[...].astype(BF16)
    wd = wd_ref[...].astype(BF16)
    row0 = 0
    for x_ref in x_refs:
        m = x_ref.shape[0]
        split = FFN_SPLIT if m >= FFN_SPLIT_MIN_ROWS else 1
        step = m // split
        for part in range(split):
            x = x_ref[pl.ds(part * step, step), :]
            g = jnp.dot(x, wg, preferred_element_type=F32)
            u = jnp.dot(x, wu, preferred_element_type=F32)
            hid = (g * _sigmoid(g) * u).astype(BF16)
            acc_ref[pl.ds(row0 + part * step, step), :] += jnp.dot(hid, wd, preferred_element_type=F32)
        row0 += m

    @pl.when(f == pl.num_programs(1) - 1)
    def _():
        o_ref[...] = acc_ref[...].astype(o_ref.dtype)


def expert_ffn(xs_list, wg, wu, wd, layer):
    e, _, d = xs_list[0].shape
    m = sum(x.shape[1] for x in xs_list)
    fdim = wg.shape[3]
    tf = _tile(fdim, 256)
    return pl.pallas_call(
        _ffn_body,
        grid=(e, fdim // tf),
        in_specs=[pl.BlockSpec((None, x.shape[1], d), lambda ee, f: (ee, 0, 0)) for x in xs_list]
        + [pl.BlockSpec((None, None, d, tf), lambda ee, f: (layer, ee, 0, f)),
           pl.BlockSpec((None, None, d, tf), lambda ee, f: (layer, ee, 0, f)),
           pl.BlockSpec((None, None, tf, d), lambda ee, f: (layer, ee, f, 0))],
        out_specs=pl.BlockSpec((None, m, d), lambda ee, f: (ee, 0, 0)),
        out_shape=jax.ShapeDtypeStruct((e, m, d), BF16),
        scratch_shapes=[pltpu.VMEM((m, d), F32)],
        compiler_params=_cparams(2),
        name="ffn",
    )(*xs_list, wg, wu, wd)


def _scatter_body(rank_ref, aff_ref, ys_ref, x_ref, mg_ref, gf_ref, o_ref, *, final_norm):
    tq = rank_ref.shape[0]
    n_exp, cap, _ = ys_ref.shape
    slot = lax.broadcasted_iota(jnp.int32, (tq, cap), 1)
    rank = rank_ref[...]
    aff = aff_ref[...]
    q = jnp.concatenate([jnp.where(slot == rank[:, e:e + 1], aff[:, e:e + 1], 0.0).astype(BF16)
                         for e in range(n_exp)], axis=1)
    acc = jnp.dot(q, ys_ref[...].reshape(n_exp * cap, ys_ref.shape[2]), preferred_element_type=F32)
    x = x_ref[...] + mg_ref[...] * acc
    if final_norm:
        x = x * lax.rsqrt(jnp.mean(x * x, axis=-1, keepdims=True) + NORM_EPS) * gf_ref[...]
    o_ref[...] = x


def scatter_residual(rank, aff, ys, row_off, cap, x, mod_gate, final_g):
    s, n, e = rank.shape
    d = x.shape[2]
    tq = _tile(n, 1024)
    boff = row_off // cap
    final_norm = final_g is not None
    gf = (final_g if final_norm else jnp.ones((d,), F32)).reshape(1, d)
    return pl.pallas_call(
        functools.partial(_scatter_body, final_norm=final_norm),
        grid=(s, n // tq),
        in_specs=[pl.BlockSpec((None, tq, e), lambda ss, i: (ss, i, 0)),
                  pl.BlockSpec((None, tq, e), lambda ss, i: (ss, i, 0)),
                  pl.BlockSpec((e, cap, d), lambda ss, i: (0, boff + ss, 0)),
                  pl.BlockSpec((None, tq, d), lambda ss, i: (ss, i, 0)),
                  pl.BlockSpec((None, 1, d), lambda ss, i: (ss, 0, 0)),
                  pl.BlockSpec((1, d), lambda ss, i: (0, 0))],
        out_specs=pl.BlockSpec((None, tq, d), lambda ss, i: (ss, i, 0)),
        out_shape=jax.ShapeDtypeStruct(x.shape, F32),
        compiler_params=_cparams(2),
        name="scatter",
    )(rank, aff, ys, x, mod_gate, gf)


def _mixer_and_route(x, h0_f, h0_b, mods, lw, fs, shared_mod=False, pos_embed=None):
    b, l, d = x.shape
    if shared_mod:
        flat = lambda a: a.reshape(1, b * l, a.shape[-1])
        unflat = lambda a: a.reshape(b, l, a.shape[-1])
        mods = [m[:1] for m in mods]
    else:
        flat = unflat = lambda a: a
    if pos_embed is None:
        proj = unflat(norm_proj(flat(x), lw['norm1_g'], mods[0], mods[1], lw['w_in']))
    else:
        proj = norm_proj(x, lw['norm1_g'], mods[0], mods[1], lw['w_in'], pos_embed)
    t1, t2, mid = hy_filter_tables(l, lw['hy_filt_w1'], lw['hy_filt_b1'], lw['hy_filt_w2'], lw['hy_filt_b2'],
                                  lw['hy_filt_w3'], lw['hy_filt_freq'], fs)
    ya = hyena_mixer(proj, lw['hy_conv_w'], lw['hy_conv_b'], lw['hy_bias'], t1, t2, mid, fs)
    hf, hb = lru_mixer(proj, 4, lw['lru_conv_w'], lw['lru_conv_b'], lw['lru_wa'], lw['lru_ba'],
                       lw['lru_wx'], lw['lru_bx'], lw['lru_lambda'], h0_f, h0_b)
    yc = shortconv_mixer(proj, 5, lw['sc_conv_w'])
    x_new, h2, aff = merge_mixers(flat(proj), 8, 3, flat(ya), flat(yc), flat(hf), flat(hb), flat(x), mods[2],
                                  lw['norm2_g'], mods[3], mods[4],
                                  lw['w_hy_out'], lw['w_lru_out'], lw['w_sc_out'], lw['w_o'], lw['w_router'],
                                  pos_embed)
    return unflat(x_new), unflat(h2), unflat(aff), hf[:, -1], hb[:, 0]


def _capacity(n, e):
    return EC_CAPACITY * n // e


def _route(h2, aff):
    s, n, e = aff.shape
    cap = _capacity(n, e)
    aff_t = jnp.transpose(aff, (0, 2, 1))
    rank_t = select_tokens(aff_t.reshape(s * e, n), cap).reshape(s, e, n)
    return gather_tokens(rank_t, h2, cap), jnp.transpose(rank_t, (0, 2, 1))


def kernel(x, c, ctx, c_ctx, w_ada, b_ada, norm1_g, norm2_g, w_in, hy_conv_w, hy_conv_b, hy_filt_w1, hy_filt_b1, hy_filt_w2, hy_filt_b2, hy_filt_w3, hy_filt_freq, hy_bias, lru_conv_w, lru_conv_b, lru_wa, lru_ba, lru_wx, lru_bx, lru_lambda, sc_conv_w, w_hy_out, w_lru_out, w_sc_out, w_o, w_router, w_exp_gate, w_exp_up, w_exp_down, final_norm_g):
    bsz, n_lat, d = x.shape
    n_ctx = ctx.shape[1]
    depth = w_ada.shape[0]
    rows = n_lat // GRID_W

    pos_embed = jnp.asarray(_pos_embed(rows, d))
    xc = ctx
    fs_lat = jnp.asarray(_dft_table(n_lat), BF16)
    fs_ctx = jnp.asarray(_dft_table(n_ctx), BF16)

    cc = jnp.zeros((2 * bsz, d), F32).at[:bsz].set(c).at[bsz].set(c_ctx)
    mods = ada_mods(cc, w_ada, b_ada)
    zeros_state = jnp.zeros((bsz, d), F32)

    for l in range(depth):
        last = l == depth - 1
        lw = {
            'norm1_g': norm1_g[l], 'norm2_g': norm2_g[l], 'w_in': w_in[l].astype(BF16),
            'hy_conv_w': hy_conv_w[l], 'hy_conv_b': hy_conv_b[l],
            'hy_filt_w1': hy_filt_w1[l], 'hy_filt_b1': hy_filt_b1[l], 'hy_filt_w2': hy_filt_w2[l],
            'hy_filt_b2': hy_filt_b2[l], 'hy_filt_w3': hy_filt_w3[l], 'hy_filt_freq': hy_filt_freq[l],
            'hy_bias': hy_bias[l], 'lru_conv_w': lru_conv_w[l], 'lru_conv_b': lru_conv_b[l],
            'lru_wa': lru_wa[l], 'lru_ba': lru_ba[l], 'lru_wx': lru_wx[l], 'lru_bx': lru_bx[l],
            'lru_lambda': lru_lambda[l], 'sc_conv_w': sc_conv_w[l],
            'w_hy_out': w_hy_out[l].astype(BF16), 'w_lru_out': w_lru_out[l].astype(BF16),
            'w_sc_out': w_sc_out[l].astype(BF16), 'w_o': w_o[l].astype(BF16), 'w_router': w_router[l],
        }
        mod_l = [mods[l, :bsz, k * d:(k + 1) * d].reshape(bsz, 1, d) for k in range(6)]
        mod_c = [jnp.broadcast_to(mods[l, bsz, k * d:(k + 1) * d].reshape(1, 1, d), (bsz, 1, d)) for k in range(6)]

        n_exp = w_router.shape[2]
        cap_l = _capacity(n_lat, n_exp)
        cap_c = _capacity(n_ctx, n_exp)
        rows_l = bsz * cap_l
        if last:
            proj_c = norm_proj(xc.reshape(1, bsz * n_ctx, d), lw['norm1_g'], mod_c[0][:1], mod_c[1][:1],
                               lw['w_in'][:, 4 * d:5 * d]).reshape(bsz, n_ctx, d)
            hf_c, hb_c = lru_mixer(proj_c, 0, lw['lru_conv_w'], lw['lru_conv_b'], lw['lru_wa'], lw['lru_ba'],
                                   lw['lru_wx'], lw['lru_bx'], lw['lru_lambda'], zeros_state, zeros_state)
            state_f, state_b = hf_c[:, -1], hb_c[:, 0]
            xs_streams = []
        else:
            xc_mid, h2_c, aff_c, state_f, state_b = _mixer_and_route(xc, zeros_state, zeros_state, mod_c, lw, fs_ctx,
                                                                     shared_mod=True)
            xs_c, rank_c = _route(h2_c, aff_c)
            xs_streams = [xs_c]

        x_mid, h2_l, aff_l, _, _ = _mixer_and_route(x, state_f, state_b, mod_l, lw, fs_lat,
                                                    pos_embed=pos_embed if l == 0 else None)
        xs_l, rank_l = _route(h2_l, aff_l)
        ys = expert_ffn([xs_l] + xs_streams, w_exp_gate, w_exp_up, w_exp_down, l)

        if not last:
            xc = scatter_residual(rank_c, aff_c, ys, rows_l, cap_c, xc_mid, mod_c[5], None)
        x = scatter_residual(rank_l, aff_l, ys, 0, cap_l, x_mid, mod_l[5], final_norm_g if last else None)
    return x
```

```python
import functools
import math

import numpy as np
import jax
import jax.numpy as jnp
from jax import lax
from jax.experimental import pallas as pl
from jax.experimental.pallas import tpu as pltpu

F32 = jnp.float32
BF16 = jnp.bfloat16
HI = lax.Precision.HIGHEST

NORM_EPS = 1e-6
GRID_W = 64
N_EXPERTS = 16
EC_CAPACITY = 2
LRU_C = 8.0
HYENA_ORDER = 2
HYENA_EMB_DIM = 33
HYENA_FAST_DECAY = 0.3
HYENA_SLOW_DECAY = 1.5
HYENA_TARGET = 1e-2
MLP_PAD = 128

V7X_VMEM_BYTES = 64 * 1024 * 1024
VMEM_LIMIT = 56 * 1024 * 1024
assert VMEM_LIMIT < V7X_VMEM_BYTES


def _cparams(n_grid):
    return pltpu.CompilerParams(dimension_semantics=("arbitrary",) * n_grid,
                                vmem_limit_bytes=VMEM_LIMIT)


def _tile(n, pref):
    if n <= pref:
        return n
    t = pref
    while n % t:
        t //= 2
    return t


def _resident(block_shape, index_map):
    return pl.BlockSpec(block_shape, index_map, pipeline_mode=pl.Buffered(1))


@functools.lru_cache(maxsize=None)
def _pos_embed(rows, d):
    def sincos(pos, dim):
        half = dim // 2
        omega = 1.0 / (10000.0 ** (np.arange(half, dtype=np.float64) / half))
        ang = pos[:, None] * omega[None, :]
        return np.concatenate([np.sin(ang), np.cos(ang)], axis=-1)
    half = d // 2
    er = sincos(np.arange(rows, dtype=np.float64), half)
    ec = sincos(np.arange(GRID_W, dtype=np.float64), half)
    emb = np.concatenate([np.broadcast_to(er[:, None, :], (rows, GRID_W, half)),
                          np.broadcast_to(ec[None, :, :], (rows, GRID_W, half))], axis=-1)
    return emb.reshape(rows * GRID_W, d).astype(np.float32)


@functools.lru_cache(maxsize=None)
def _dft_table(L):
    h = L // 2
    k = np.arange(h, dtype=np.int64)
    ang_e = np.pi * ((k[:, None] * (2 * k[None, :])) % (2 * L)).astype(np.float64) / L
    ang_o = np.pi * ((k[:, None] * (2 * k[None, :] + 1)) % (2 * L)).astype(np.float64) / L
    ce, se, co, so = np.cos(ang_e), np.sin(ang_e), np.cos(ang_o), np.sin(ang_o)
    return np.concatenate([ce, se, co, so, co.T, so.T], axis=0).astype(np.float32)


@functools.lru_cache(maxsize=None)
def _filter_feats(L):
    bands = (HYENA_EMB_DIM - 1) // 2
    t01 = np.linspace(0.0, 1.0, L, dtype=np.float32).astype(np.float64)[:, None]
    w = ((2.0 * math.pi / L) * np.arange(L, dtype=np.float32))[:, None].astype(np.float64)
    f = np.linspace(1e-4, bands - 1, bands, dtype=np.float32).astype(np.float64)[None, :]
    feats = np.concatenate([t01, np.cos(f * w), -np.sin(f * w)], axis=-1)
    out = np.zeros((L, MLP_PAD), np.float32)
    out[:, :HYENA_EMB_DIM] = feats
    return out


@functools.lru_cache(maxsize=None)
def _decay_window(L, d):
    max_decay = math.log(HYENA_TARGET) / HYENA_FAST_DECAY
    min_decay = math.log(HYENA_TARGET) / HYENA_SLOW_DECAY
    t01 = np.linspace(0.0, 1.0, L, dtype=np.float32).astype(np.float64)[:, None]
    deltas = np.linspace(min_decay, max_decay, d, dtype=np.float32).astype(np.float64)
    return np.exp(-t01 * np.abs(deltas)[None, :]).astype(np.float32)


@functools.lru_cache(maxsize=None)
def _prefix_table(n):
    return np.triu(np.ones((n, n), np.float32))


def _sigmoid(x):
    return 0.5 * jnp.tanh(0.5 * x) + 0.5


def _split_hi_lo(x):
    hi = x.astype(BF16)
    return hi, (x - hi.astype(F32)).astype(BF16)


def _dot_hi_lo(a, b):
    a_hi, a_lo = _split_hi_lo(a)
    b_hi, b_lo = _split_hi_lo(b)
    return (jnp.dot(a_hi, b_hi, preferred_element_type=F32) + jnp.dot(a_hi, b_lo, preferred_element_type=F32)
            + jnp.dot(a_lo, b_hi, preferred_element_type=F32))


def _rms_modulate(x, g, shift, scale):
    y = x * lax.rsqrt(jnp.mean(x * x, axis=-1, keepdims=True) + NORM_EPS)
    return (y * g) * (1.0 + scale) + shift


def _ada_body(c_ref, w_ref, b_ref, o_ref):
    c = c_ref[...]
    s = c * _sigmoid(c)
    o_ref[...] = jnp.dot(s, w_ref[...], precision=HI, preferred_element_type=F32) + b_ref[...]


def ada_mods(cc, w_ada, b_ada):
    depth, d, n = w_ada.shape
    r = cc.shape[0]
    tn = _tile(n, 1536)
    return pl.pallas_call(
        _ada_body,
        grid=(depth, n // tn),
        in_specs=[pl.BlockSpec((r, d), lambda l, j: (0, 0)),
                  pl.BlockSpec((None, d, tn), lambda l, j: (l, 0, j)),
                  pl.BlockSpec((None, 1, tn), lambda l, j: (l, 0, j))],
        out_specs=pl.BlockSpec((None, r, tn), lambda l, j: (l, 0, j)),
        out_shape=jax.ShapeDtypeStruct((depth, r, n), F32),
        compiler_params=_cparams(2),
        name="ada",
    )(cc, w_ada, b_ada.reshape(depth, 1, n))


def _norm_proj_body(x_ref, g_ref, sh_ref, sc_ref, w_ref, o_ref, h_ref):
    @pl.when(pl.program_id(2) == 0)
    def _():
        h_ref[...] = _rms_modulate(x_ref[...], g_ref[...], sh_ref[...], sc_ref[...]).astype(BF16)

    o_ref[...] = jnp.dot(h_ref[...], w_ref[...], preferred_element_type=F32).astype(o_ref.dtype)


def _embed_norm_proj_body(x_ref, pe_ref, g_ref, sh_ref, sc_ref, w_ref, o_ref, h_ref):
    @pl.when(pl.program_id(2) == 0)
    def _():
        h_ref[...] = _rms_modulate(x_ref[...] + pe_ref[...], g_ref[...], sh_ref[...], sc_ref[...]).astype(BF16)

    o_ref[...] = jnp.dot(h_ref[...], w_ref[...], preferred_element_type=F32).astype(o_ref.dtype)


PROJ_COLS = 2816


def norm_proj(x, g, shift, scale, w, pos_embed=None):
    b, l, d = x.shape
    n = w.shape[1]
    tl = _tile(l, 1024)
    tn = _tile(n, PROJ_COLS)
    row = pl.BlockSpec((None, tl, d), lambda bb, i, j: (bb, i, 0))
    tail = [pl.BlockSpec((1, d), lambda bb, i, j: (0, 0)),
            pl.BlockSpec((None, 1, d), lambda bb, i, j: (bb, 0, 0)),
            pl.BlockSpec((None, 1, d), lambda bb, i, j: (bb, 0, 0)),
            pl.BlockSpec((d, tn), lambda bb, i, j: (0, j))]
    proj_spec = pl.BlockSpec((None, tl, tn), lambda bb, i, j: (bb, i, j))
    proj_shape = jax.ShapeDtypeStruct((b, l, n), BF16)
    common = dict(grid=(b, l // tl, n // tn), scratch_shapes=[pltpu.VMEM((tl, d), BF16)],
                  compiler_params=_cparams(3), name="norm_proj")
    if pos_embed is None:
        return pl.pallas_call(_norm_proj_body, in_specs=[row] + tail, out_specs=proj_spec, out_shape=proj_shape,
                              **common)(x, g.reshape(1, d), shift, scale, w)
    return pl.pallas_call(
        _embed_norm_proj_body,
        in_specs=[row, pl.BlockSpec((tl, d), lambda bb, i, j: (i, 0))] + tail,
        out_specs=proj_spec, out_shape=proj_shape,
        **common)(x, pos_embed, g.reshape(1, d), shift, scale, w)


def _hy_hidden_body(feats_ref, w1_ref, b1_ref, w2_ref, b2_ref, fr_ref, h_ref):
    fr = fr_ref[...]
    h = jnp.sin(fr * (jnp.dot(feats_ref[...], w1_ref[...], precision=HI,
                              preferred_element_type=F32) + b1_ref[...]))
    h_ref[...] = jnp.sin(fr * (jnp.dot(h, w2_ref[...], precision=HI,
                                       preferred_element_type=F32) + b2_ref[...]))


LANES = 128


def _sign_col(rows):
    row = lax.broadcasted_iota(jnp.int32, (rows, 1), 0)
    return (1 - 2 * (row & 1)).astype(F32)


def _put_slabs(nat_ref, r0, rows):
    for s in range(nat_ref.shape[0]):
        nat_ref[s, pl.ds(r0, rows.shape[0]), :] = rows[:, s * LANES:(s + 1) * LANES]


def _get_slabs(nat_ref, r0, n):
    return jnp.concatenate([nat_ref[s, pl.ds(r0, n), :] for s in range(nat_ref.shape[0])], axis=1)


def _split_rows(nat_ref, e_ref, o_ref):
    nslab, L, _ = nat_ref.shape
    h = L // 2
    rd = min(h, 256)
    sgn = _sign_col(rd)

    def body(i, carry):
        m0 = pl.multiple_of(i * rd, rd)
        e = jnp.concatenate([nat_ref[s, pl.ds(2 * m0, rd, stride=2), :] for s in range(nslab)], axis=1)
        o = jnp.concatenate([nat_ref[s, pl.ds(2 * m0 + 1, rd, stride=2), :] for s in range(nslab)], axis=1)
        e_ref[pl.ds(m0, rd), :] = e.astype(BF16)
        o_ref[pl.ds(m0, rd), :] = o.astype(BF16)
        return (carry[0] + jnp.sum(e * sgn, axis=0, keepdims=True),
                carry[1] + jnp.sum(o * sgn, axis=0, keepdims=True))

    zero = jnp.zeros((1, nslab * LANES), F32)
    return lax.fori_loop(0, h // rd, body, (zero, zero))


def _hy_filter_body(h_ref, w3f_ref, w3b_ref, decay_ref, tb_ref, t1_ref, t2_ref, mid_ref,
                    p_ref, q_ref, pe_ref, po_ref, qe_ref, qo_ref):
    L = h_ref.shape[0]
    hl = L // 2
    hid = h_ref[...]
    decay = decay_ref[...]
    fwd = _dot_hi_lo(hid, w3f_ref[...]) * decay
    bwd = _dot_hi_lo(hid, w3b_ref[...]) * decay
    row = lax.broadcasted_iota(jnp.int32, fwd.shape, 0)
    bwd = jnp.where(row >= 1, bwd, 0.0)
    inv = 1.0 / (jnp.sum(jnp.abs(fwd), axis=0, keepdims=True)
                 + jnp.sum(jnp.abs(bwd), axis=0, keepdims=True))
    _put_slabs(p_ref, 0, fwd + bwd)
    _put_slabs(q_ref, 0, bwd - fwd)
    k_re_mid, _ = _split_rows(p_ref, pe_ref, po_ref)
    _, k_im_mid = _split_rows(q_ref, qe_ref, qo_ref)
    a_e = jnp.dot(tb_ref[0:hl, :], pe_ref[...], preferred_element_type=F32)
    a_o = jnp.dot(tb_ref[2 * hl:3 * hl, :], po_ref[...], preferred_element_type=F32)
    b_e = jnp.dot(tb_ref[hl:2 * hl, :], qe_ref[...], preferred_element_type=F32)
    b_o = jnp.dot(tb_ref[3 * hl:4 * hl, :], qo_ref[...], preferred_element_type=F32)
    n_fft = 2.0 * L
    row_h = lax.broadcasted_iota(jnp.int32, (hl, fwd.shape[1]), 0)
    w_k = jnp.where(row_h >= 1, 2.0 / n_fft, 1.0 / n_fft) * inv
    w_s = (2.0 / n_fft) * inv
    t1_ref[0:hl, :] = (a_e + a_o) * w_k
    t1_ref[hl:L, :] = (a_e - a_o) * w_k
    t2_ref[0:hl, :] = (b_e + b_o) * w_s
    t2_ref[hl:L, :] = (b_o - b_e) * w_s
    mid_ref[0:1, :] = k_re_mid * w_s
    mid_ref[1:2, :] = k_im_mid * w_s


def hy_filter_tables(L, w1, b1, w2, b2, w3, freq, fs):
    d = w3.shape[1] // (HYENA_ORDER * 2)
    fw = w1.shape[1]
    pad = lambda a, r, c: jnp.zeros((r, c), F32).at[:a.shape[0], :a.shape[1]].set(a)
    w1p = pad(w1, MLP_PAD, MLP_PAD)
    w2p = pad(w2, MLP_PAD, MLP_PAD)
    w3p = pad(w3, MLP_PAD, w3.shape[1])
    b1p = pad(b1.reshape(1, fw), 1, MLP_PAD)
    b2p = pad(b2.reshape(1, fw), 1, MLP_PAD)
    frp = pad(freq.reshape(1, fw), 1, MLP_PAD)
    feats = jnp.asarray(_filter_feats(L))
    decay = jnp.asarray(_decay_window(L, d))
    tn = _tile(d, 256)
    nj = d // tn
    full = lambda o, j: (0, 0)
    one = lambda i: (0, 0)
    hidden = pl.pallas_call(
        _hy_hidden_body,
        grid=(1,),
        in_specs=[pl.BlockSpec((L, MLP_PAD), one),
                  pl.BlockSpec((MLP_PAD, MLP_PAD), one),
                  pl.BlockSpec((1, MLP_PAD), one),
                  pl.BlockSpec((MLP_PAD, MLP_PAD), one),
                  pl.BlockSpec((1, MLP_PAD), one),
                  pl.BlockSpec((1, MLP_PAD), one)],
        out_specs=pl.BlockSpec((L, MLP_PAD), one),
        out_shape=jax.ShapeDtypeStruct((L, MLP_PAD), F32),
        compiler_params=_cparams(1),
        name="hy_hidden",
    )(feats, w1p, b1p, w2p, b2p, frp)
    hl = L // 2
    t1, t2, mid = pl.pallas_call(
        _hy_filter_body,
        grid=(HYENA_ORDER, nj),
        in_specs=[pl.BlockSpec((L, MLP_PAD), full),
                  pl.BlockSpec((MLP_PAD, tn), lambda o, j: (0, o * 2 * nj + j)),
                  pl.BlockSpec((MLP_PAD, tn), lambda o, j: (0, o * 2 * nj + nj + j)),
                  pl.BlockSpec((L, tn), lambda o, j: (0, j)),
                  _resident(fs.shape, full)],
        out_specs=[pl.BlockSpec((L, tn), lambda o, j: (0, o * nj + j)),
                   pl.BlockSpec((L, tn), lambda o, j: (0, o * nj + j)),
                   pl.BlockSpec((2, tn), lambda o, j: (0, o * nj + j))],
        out_shape=[jax.ShapeDtypeStruct((L, HYENA_ORDER * d), F32),
                   jax.ShapeDtypeStruct((L, HYENA_ORDER * d), F32),
                   jax.ShapeDtypeStruct((2, HYENA_ORDER * d), F32)],
        scratch_shapes=[pltpu.VMEM((tn // LANES, L, LANES), F32), pltpu.VMEM((tn // LANES, L, LANES), F32),
                        pltpu.VMEM((hl, tn), BF16), pltpu.VMEM((hl, tn), BF16),
                        pltpu.VMEM((hl, tn), BF16), pltpu.VMEM((hl, tn), BF16)],
        compiler_params=_cparams(2),
        name="hy_filter",
    )(hidden, w3p, w3p, decay, fs)
    return t1, t2, mid


HALO = 8


def _chunks(n_rows, chunk, fn):
    def body(c, carry):
        fn(pl.multiple_of(c * chunk, chunk))
        return carry
    lax.fori_loop(0, n_rows // chunk, body, 0)


def _conv3_even_odd(src_ref, w_ref, b_ref, pad_ref, emit):
    assert w_ref.shape[0] == 3
    L = src_ref.shape[0]
    nslab = pad_ref.shape[0]
    h = L // 2
    rc = min(L, 256)
    rd = min(h, 128)
    zeros = jnp.zeros((HALO, LANES), F32)
    for s in range(nslab):
        pad_ref[s, 0:HALO, :] = zeros
        pad_ref[s, L + HALO:L + 2 * HALO, :] = zeros

    def fill(r0):
        _put_slabs(pad_ref, r0 + HALO, src_ref[pl.ds(r0, rc), :].astype(F32))
    _chunks(L, rc, fill)

    def taps(m0, off):
        return jnp.concatenate([pad_ref[s, pl.ds(2 * m0 + (HALO - 1 + off), rd, stride=2), :]
                                for s in range(nslab)], axis=1)

    def conv(m0):
        t0, t1, t2, t3 = (taps(m0, off) for off in range(4))
        w0, w1, w2 = w_ref[0:1, :], w_ref[1:2, :], w_ref[2:3, :]
        bias = b_ref[...]
        emit(m0, t0 * w0 + t1 * w1 + t2 * w2 + bias, t1 * w0 + t2 * w1 + t3 * w2 + bias)
    _chunks(h, rd, conv)


def _long_conv_even_odd(u_refs, t1_ref, t2_ref, mid_ref, skip, tb_ref, spec_e_ref, spec_o_ref,
                        xec_ref, xes_ref, xoc_ref, xos_ref, emit):
    ue32_ref, uo32_ref, ue_ref, uo_ref, umid_ref = u_refs
    h = ue_ref.shape[0]
    rm = min(h, 64)
    ri = min(h, 1024)

    spec_e_ref[...] = jnp.dot(tb_ref[0:2 * h, :], ue_ref[...], preferred_element_type=F32)
    spec_o_ref[...] = jnp.dot(tb_ref[2 * h:4 * h, :], uo_ref[...], preferred_element_type=F32)

    def mix(r0):
        ae = spec_e_ref[pl.ds(r0, rm), :]
        be = spec_e_ref[pl.ds(h + r0, rm), :]
        ao = spec_o_ref[pl.ds(r0, rm), :]
        bo = spec_o_ref[pl.ds(h + r0, rm), :]
        p0, p1, q0, q1 = ae + ao, ae - ao, be + bo, bo - be
        t1_lo = t1_ref[pl.ds(r0, rm), :]
        t2_lo = t2_ref[pl.ds(r0, rm), :]
        t1_hi = t1_ref[pl.ds(h + r0, rm), :]
        t2_hi = t2_ref[pl.ds(h + r0, rm), :]
        za0 = p0 * t1_lo + q0 * t2_lo
        zb0 = q0 * t1_lo - p0 * t2_lo
        za1 = p1 * t1_hi + q1 * t2_hi
        zb1 = q1 * t1_hi - p1 * t2_hi
        xec_ref[pl.ds(r0, rm), :] = (za0 + za1).astype(BF16)
        xes_ref[pl.ds(r0, rm), :] = (zb0 - zb1).astype(BF16)
        xoc_ref[pl.ds(r0, rm), :] = (za0 - za1).astype(BF16)
        xos_ref[pl.ds(r0, rm), :] = (zb0 + zb1).astype(BF16)
    _chunks(h, rm, mix)

    ua_mid = umid_ref[0:1, :]
    ub_mid = umid_ref[1:2, :]
    t1_mid = mid_ref[0:1, :]
    t2_mid = mid_ref[1:2, :]
    za_mid = ua_mid * t1_mid + ub_mid * t2_mid
    zb_mid = ub_mid * t1_mid - ua_mid * t2_mid
    sgn = _sign_col(ri)

    def inverse(r0):
        ye = jnp.dot(tb_ref[pl.ds(r0, ri), :], xec_ref[...], preferred_element_type=F32)
        ye = ye + jnp.dot(tb_ref[pl.ds(h + r0, ri), :], xes_ref[...], preferred_element_type=F32)
        yo = jnp.dot(tb_ref[pl.ds(4 * h + r0, ri), :], xoc_ref[...], preferred_element_type=F32)
        yo = yo + jnp.dot(tb_ref[pl.ds(5 * h + r0, ri), :], xos_ref[...], preferred_element_type=F32)
        emit(r0, ye + sgn * za_mid + ue32_ref[pl.ds(r0, ri), :] * skip,
             yo + sgn * zb_mid + uo32_ref[pl.ds(r0, ri), :] * skip)
    _chunks(h, ri, inverse)


def _hyena_body(p1_ref, p2_ref, pv_ref, cw1_ref, cw2_ref, cwv_ref, cb1_ref, cb2_ref, cbv_ref,
                bias_ref, t1a_ref, t2a_ref, mida_ref, t1b_ref, t2b_ref, midb_ref, tb_ref, o_ref,
                pad_ref, x1e_ref, x1o_ref, x2e_ref, x2o_ref, ue32_ref, uo32_ref, ue_ref, uo_ref, umid_ref,
                spec_e_ref, spec_o_ref, xec_ref, xes_ref, xoc_ref, xos_ref, out_ref):
    L = p1_ref.shape[0]

    def put(e_ref, o_ref_):
        def emit(m0, e, o):
            e_ref[pl.ds(m0, e.shape[0]), :] = e
            o_ref_[pl.ds(m0, o.shape[0]), :] = o
        return emit

    def put_u(m0, e, o):
        n = e.shape[0]
        ue32_ref[pl.ds(m0, n), :] = e
        uo32_ref[pl.ds(m0, n), :] = o
        ue_ref[pl.ds(m0, n), :] = e.astype(BF16)
        uo_ref[pl.ds(m0, n), :] = o.astype(BF16)
        sgn = _sign_col(n)
        umid_ref[0:1, :] += jnp.sum(e * sgn, axis=0, keepdims=True)
        umid_ref[1:2, :] += jnp.sum(o * sgn, axis=0, keepdims=True)

    _conv3_even_odd(p1_ref, cw1_ref, cb1_ref, pad_ref, put(x1e_ref, x1o_ref))
    _conv3_even_odd(p2_ref, cw2_ref, cb2_ref, pad_ref, put(x2e_ref, x2o_ref))
    umid_ref[...] = jnp.zeros_like(umid_ref)
    _conv3_even_odd(pv_ref, cwv_ref, cbv_ref, pad_ref, put_u)
    u_refs = (ue32_ref, uo32_ref, ue_ref, uo_ref, umid_ref)
    work = (tb_ref, spec_e_ref, spec_o_ref, xec_ref, xes_ref, xoc_ref, xos_ref)

    def first(m0, ce, co):
        @pl.when(m0 == 0)
        def _():
            umid_ref[...] = jnp.zeros_like(umid_ref)
        n = ce.shape[0]
        put_u(m0, x1e_ref[pl.ds(m0, n), :] * ce, x1o_ref[pl.ds(m0, n), :] * co)
    _long_conv_even_odd(u_refs, t1a_ref, t2a_ref, mida_ref, bias_ref[0:1, :], *work, first)

    def second(m0, ce, co):
        n = ce.shape[0]
        oe = x2e_ref[pl.ds(m0, n), :] * ce
        oo = x2o_ref[pl.ds(m0, n), :] * co
        for s in range(out_ref.shape[0]):
            out_ref[s, pl.ds(2 * m0, n, stride=2), :] = oe[:, s * LANES:(s + 1) * LANES]
            out_ref[s, pl.ds(2 * m0 + 1, n, stride=2), :] = oo[:, s * LANES:(s + 1) * LANES]
    _long_conv_even_odd(u_refs, t1b_ref, t2b_ref, midb_ref, bias_ref[1:2, :], *work, second)

    rc = min(L, 256)

    def finish(r0):
        o_ref[pl.ds(r0, rc), :] = _get_slabs(out_ref, r0, rc).astype(o_ref.dtype)
    _chunks(L, rc, finish)


def hyena_mixer(proj, conv_w, conv_b, bias, t1, t2, mid, fs):
    b, l, _ = proj.shape
    d = bias.shape[1]
    tn = _tile(d, 256)
    nj = d // tn
    k = conv_w.shape[0]
    hl = l // 2
    cb = conv_b.reshape(1, 3 * d)
    sec = lambda s: pl.BlockSpec((None, l, tn), lambda j, bb: (bb, 0, s * nj + j))
    cws = lambda s: pl.BlockSpec((k, tn), lambda j, bb: (0, s * nj + j))
    cbs = lambda s: pl.BlockSpec((1, tn), lambda j, bb: (0, s * nj + j))
    tab = lambda o, r: _resident((r, tn), lambda j, bb: (0, o * nj + j))
    half_f32 = pltpu.VMEM((hl, tn), F32)
    half_bf16 = pltpu.VMEM((hl, tn), BF16)
    return pl.pallas_call(
        _hyena_body,
        grid=(nj, b),
        in_specs=[sec(0), sec(1), sec(2), cws(0), cws(1), cws(2), cbs(0), cbs(1), cbs(2),
                  pl.BlockSpec((HYENA_ORDER, tn), lambda j, bb: (0, j)),
                  tab(0, l), tab(0, l), tab(0, 2), tab(1, l), tab(1, l), tab(1, 2),
                  _resident(fs.shape, lambda j, bb: (0, 0))],
        out_specs=pl.BlockSpec((None, l, tn), lambda j, bb: (bb, 0, j)),
        out_shape=jax.ShapeDtypeStruct((b, l, d), BF16),
        scratch_shapes=[pltpu.VMEM((tn // LANES, l + 2 * HALO, LANES), F32),
                        half_f32, half_f32, half_f32, half_f32,
                        half_f32, half_f32, half_bf16, half_bf16,
                        pltpu.VMEM((8, tn), F32),
                        pltpu.VMEM((l, tn), F32), pltpu.VMEM((l, tn), F32),
                        half_bf16, half_bf16, half_bf16, half_bf16,
                        pltpu.VMEM((tn // LANES, l, LANES), F32)],
        compiler_params=_cparams(2),
        name="hyena",
    )(proj, proj, proj, conv_w, conv_w, conv_w, cb, cb, cb, bias, t1, t2, mid, t1, t2, mid, fs)


SQRT_GUARD = 1e-30
LRU_CONV_LEFT = 2
SCAN_UNROLL = 4
LRU_GATE_TILE = 256
LRU_LANES = 512


def _dwconv_padded(src_ref, cols, w_ref, left, pad_ref):
    L = src_ref.shape[0]
    nslab = pad_ref.shape[0]
    zeros = jnp.zeros((HALO, LANES), F32)
    for s in range(nslab):
        pad_ref[s, 0:HALO, :] = zeros
        pad_ref[s, L + HALO:L + 2 * HALO, :] = zeros
    _put_slabs(pad_ref, HALO, src_ref[:, cols].astype(F32))
    acc = None
    for k in range(w_ref.shape[0]):
        tap = jnp.concatenate([pad_ref[s, pl.ds(HALO - left + k, L, stride=1), :] for s in range(nslab)], axis=1)
        term = tap * w_ref[k:k + 1, cols]
        acc = term if acc is None else acc + term
    return acc


def _lru_gates(rec_ref, cw_ref, cb_ref, wa_ref, ba_ref, wx_ref, bx_ref, lam_ref, store, pad_ref):
    tg = wa_ref.shape[-1]
    for t in range(rec_ref.shape[1] // tg):
        cols = slice(t * tg, (t + 1) * tg)
        xc = _dwconv_padded(rec_ref, cols, cw_ref, LRU_CONV_LEFT, pad_ref) + cb_ref[:, cols]
        xcb = xc.astype(BF16)
        xh = 0.5 * xc
        for dr in range(2):
            tr = jnp.tanh(jnp.dot(xcb, wa_ref[dr, t], preferred_element_type=F32) + 0.5 * ba_ref[dr:dr + 1, cols])
            ti = jnp.tanh(jnp.dot(xcb, wx_ref[dr, t], preferred_element_type=F32) + 0.5 * bx_ref[dr:dr + 1, cols])
            ch = (-0.5 * LRU_C * math.log2(math.e)) * jnp.log1p(jnp.exp(-lam_ref[dr:dr + 1, cols]))
            a = jnp.exp2(tr * ch + ch)
            om = 1.0 - a * a
            store(dr, t, a, (om * lax.rsqrt(jnp.maximum(om, SQRT_GUARD))) * (ti * xh + xh))


def _scan_window4(a_ref, b_ref, L, reverse):
    rc = min(L, 128)
    sgn = 1 if reverse else -1
    nchunk = L // rc

    def chunk(c, carry):
        r0 = pl.multiple_of((c if reverse else nchunk - 1 - c) * rc, rc) + HALO
        for s in range(a_ref.shape[0]):
            a0, a1, a2, a3 = (a_ref[s, pl.ds(r0 + sgn * k, rc, stride=1), :] for k in range(4))
            b0, b1, b2, b3 = (b_ref[s, pl.ds(r0 + sgn * k, rc, stride=1), :] for k in range(4))
            b_ref[s, pl.ds(r0, rc), :] = b0 + a0 * (b1 + a1 * (b2 + a2 * b3))
            a_ref[s, pl.ds(r0, rc), :] = (a0 * a1) * (a2 * a3)
        return carry
    lax.fori_loop(0, nchunk, chunk, 0)


def _lru_body(rec_ref, cw_ref, cb_ref, wa_ref, ba_ref, wx_ref, bx_ref, lam_ref, h0f_ref, h0b_ref,
              hf_ref, hb_ref, af_ref, bf_ref, ab_ref, bb_ref, pad_ref):
    L = rec_ref.shape[0]
    tg = wa_ref.shape[-1]
    spt = tg // LANES
    one = jnp.ones((HALO, LANES), F32)
    zero = jnp.zeros((HALO, LANES), F32)
    for a_ref, b_ref in ((af_ref, bf_ref), (ab_ref, bb_ref)):
        for s in range(a_ref.shape[0]):
            for lo in (0, L + HALO):
                a_ref[s, lo:lo + HALO, :] = one
                b_ref[s, lo:lo + HALO, :] = zero

    def store(dr, t, a, b):
        a_ref, b_ref = ((af_ref, bf_ref), (ab_ref, bb_ref))[dr]
        for s in range(spt):
            a_ref[t * spt + s, HALO:HALO + L, :] = a[:, s * LANES:(s + 1) * LANES]
            b_ref[t * spt + s, HALO:HALO + L, :] = b[:, s * LANES:(s + 1) * LANES]

    _lru_gates(rec_ref, cw_ref, cb_ref, wa_ref, ba_ref, wx_ref, bx_ref, lam_ref, store, pad_ref)
    _scan_window4(af_ref, bf_ref, L, False)
    _scan_window4(ab_ref, bb_ref, L, True)
    groups = L // 8
    nslab = af_ref.shape[0]

    def step(g, carry):
        rf = pl.multiple_of(g * 8, 8)
        rb = pl.multiple_of((groups - 1 - g) * 8, 8)
        new = []
        for s in range(nslab):
            lanes = slice(s * LANES, (s + 1) * LANES)
            a4, b4 = af_ref[s, pl.ds(rf + HALO, 8), :], bf_ref[s, pl.ds(rf + HALO, 8), :]
            a4p = af_ref[s, pl.ds(rf + HALO - 4, 8, stride=1), :]
            b4p = bf_ref[s, pl.ds(rf + HALO - 4, 8, stride=1), :]
            hf = (a4 * a4p) * carry[0][s] + (a4 * b4p + b4)
            hf_ref[pl.ds(rf, 8), lanes] = hf
            a4, b4 = ab_ref[s, pl.ds(rb + HALO, 8), :], bb_ref[s, pl.ds(rb + HALO, 8), :]
            a4n = ab_ref[s, pl.ds(rb + HALO + 4, 8, stride=1), :]
            b4n = bb_ref[s, pl.ds(rb + HALO + 4, 8, stride=1), :]
            hb = (a4 * a4n) * carry[1][s] + (a4 * b4n + b4)
            hb_ref[pl.ds(rb, 8), lanes] = hb
            new.append((hf, hb))
        return tuple(n[0] for n in new), tuple(n[1] for n in new)

    def tiles(h0_ref):
        return tuple(jnp.broadcast_to(h0_ref[:, s * LANES:(s + 1) * LANES], (8, LANES)) for s in range(nslab))

    lax.fori_loop(0, groups, step, (tiles(h0f_ref), tiles(h0b_ref)), unroll=SCAN_UNROLL)


def _block_diag_tiles(w, tn):
    two, h, bs, _ = w.shape
    hpt = tn // bs
    wt = w.reshape(two, h // hpt, hpt, bs, bs)
    eye = jnp.eye(hpt, dtype=w.dtype)
    dense = jnp.einsum('tnhij,hg->tnhigj', wt, eye)
    return dense.reshape(two, h // hpt, tn, tn)


def lru_mixer(proj, rec_sec, conv_w, conv_b, wa, ba, wx, bx, lam, h0_f, h0_b):
    b, l, n = proj.shape
    d = conv_w.shape[1]
    tg = _tile(d, LRU_GATE_TILE)
    tn = _tile(d, LRU_LANES)
    nj = d // tn
    tpj = tn // tg
    k = conv_w.shape[0]
    wa_t = (0.5 * _block_diag_tiles(wa, tg)).astype(BF16)
    wx_t = (0.5 * _block_diag_tiles(wx, tg)).astype(BF16)
    vec = lambda r: pl.BlockSpec((r, tn), lambda bb, j: (0, j))
    wsp = pl.BlockSpec((2, tpj, tg, tg), lambda bb, j: (0, j, 0, 0))
    h0s = pl.BlockSpec((None, 1, tn), lambda bb, j: (bb, 0, j))
    out = pl.BlockSpec((None, l, tn), lambda bb, j: (bb, 0, j))
    shp = jax.ShapeDtypeStruct((b, l, d), F32)
    coef = pltpu.VMEM((tn // LANES, l + 2 * HALO, LANES), F32)
    return pl.pallas_call(
        _lru_body,
        grid=(b, nj),
        in_specs=[pl.BlockSpec((None, l, tn), lambda bb, j: (bb, 0, rec_sec * nj + j)),
                  vec(k), vec(1), wsp, vec(2), wsp, vec(2), vec(2), h0s, h0s],
        out_specs=[out, out],
        out_shape=[shp, shp],
        scratch_shapes=[coef, coef, coef, coef,
                        pltpu.VMEM((tg // LANES, l + 2 * HALO, LANES), F32)],
        compiler_params=_cparams(2),
        name="lru",
    )(proj, conv_w, conv_b.reshape(1, d), wa_t, ba, wx_t, bx, lam, h0_f.reshape(b, 1, d), h0_b.reshape(b, 1, d))


MERGE_SPLIT = 2
MERGE_SPLIT_MIN_ROWS = 512


def _gelu_tanh(x):
    return 0.5 * x * (1.0 + jnp.tanh(math.sqrt(2.0 / math.pi) * (x + 0.044715 * (x * x * x))))


SC_HALO = 16


def _merge_body(*refs, has_pos_embed):
    (ya_ref, lg_ref, hf_ref, hb_ref, ga_ref, gb_ref, gc_ref, x_ref), refs = refs[:8], refs[8:]
    if has_pos_embed:
        pe_ref, refs = refs[0], refs[1:]
    (bg_ref, cg_ref, xv_ref, cgp_ref, xvp_ref, cgn_ref, xvn_ref, scw_ref,
     mg_ref, g2_ref, sh_ref, sc_ref, wa_ref, wb_ref, wc_ref, wo_ref, wr_ref,
     xo_ref, h2_ref, aff_ref, pad_ref) = refs
    tl = x_ref.shape[0]
    nslab = pad_ref.shape[0]
    i = pl.program_id(1)

    f32 = lambda ref: ref[...].astype(F32)
    prev = jnp.where(i > 0, f32(cgp_ref) * f32(xvp_ref), 0.0)
    nxt = jnp.where(i < pl.num_programs(1) - 1, f32(cgn_ref) * f32(xvn_ref), 0.0)
    _put_slabs(pad_ref, 0, prev)
    _put_slabs(pad_ref, SC_HALO, f32(cg_ref) * f32(xv_ref))
    _put_slabs(pad_ref, SC_HALO + tl, nxt)

    split = MERGE_SPLIT if tl >= MERGE_SPLIT_MIN_ROWS else 1
    half = tl // split
    wr_hi, wr_lo = _split_hi_lo(wr_ref[...])
    for part in range(split):
        rows = pl.ds(part * half, half)
        conv = None
        for k in range(scw_ref.shape[0]):
            tap = jnp.concatenate([pad_ref[s, pl.ds(part * half + SC_HALO - 1 + k, half, stride=1), :]
                                   for s in range(nslab)], axis=1)
            conv = tap * scw_ref[k:k + 1, :] if conv is None else conv + tap * scw_ref[k:k + 1, :]
        yc = (bg_ref[rows, :].astype(F32) * conv).astype(BF16)
        yb = (_gelu_tanh(lg_ref[rows, :].astype(F32)) * (hf_ref[rows, :] + hb_ref[rows, :])).astype(BF16)
        merged = _sigmoid(ga_ref[rows, :].astype(F32)) * jnp.dot(ya_ref[rows, :], wa_ref[...], preferred_element_type=F32)
        merged += _sigmoid(gb_ref[rows, :].astype(F32)) * jnp.dot(yb, wb_ref[...], preferred_element_type=F32)
        merged += _sigmoid(gc_ref[rows, :].astype(F32)) * jnp.dot(yc, wc_ref[...], preferred_element_type=F32)
        out = jnp.dot(merged.astype(BF16), wo_ref[...], preferred_element_type=F32)
        x_in = x_ref[rows, :] + pe_ref[rows, :] if has_pos_embed else x_ref[rows, :]
        x = x_in + mg_ref[...] * out
        xo_ref[rows, :] = x
        h2 = _rms_modulate(x, g2_ref[...], sh_ref[...], sc_ref[...])
        h2_hi, h2_lo = _split_hi_lo(h2)
        h2_ref[rows, :] = h2_hi
        logits = (jnp.dot(h2_hi, wr_hi, preferred_element_type=F32)
                  + jnp.dot(h2_hi, wr_lo, preferred_element_type=F32)
                  + jnp.dot(h2_lo, wr_hi, preferred_element_type=F32))
        e = jnp.exp(logits - jnp.max(logits, axis=-1, keepdims=True))
        aff_ref[rows, :] = e / jnp.sum(e, axis=-1, keepdims=True)


def merge_mixers(proj, gate_sec, lg_sec, sc_sec, sc_w, ya, hf, hb, x, mod_gate, g2, shift2, scale2,
                 w_a, w_b, w_c, w_o, w_r, pos_embed=None):
    b, l, d = x.shape
    e = w_r.shape[1]
    tl = _tile(l, 512)
    hb_per_tile = tl // SC_HALO
    n_halo = l // SC_HALO
    row = lambda: pl.BlockSpec((None, tl, d), lambda bb, i: (bb, i, 0))
    sec = lambda s: pl.BlockSpec((None, tl, d), lambda bb, i: (bb, i, s))
    before = lambda s: pl.BlockSpec((None, SC_HALO, d), lambda bb, i: (bb, jnp.maximum(i * hb_per_tile - 1, 0), s))
    after = lambda s: pl.BlockSpec((None, SC_HALO, d),
                                   lambda bb, i: (bb, jnp.minimum((i + 1) * hb_per_tile, n_halo - 1), s))
    modv = lambda: pl.BlockSpec((None, 1, d), lambda bb, i: (bb, 0, 0))
    wsp = lambda: _resident((d, d), lambda bb, i: (0, 0))
    has_pos = pos_embed is not None
    pos_spec = [pl.BlockSpec((tl, d), lambda bb, i: (i, 0))] if has_pos else []
    pos_arg = [pos_embed] if has_pos else []
    return pl.pallas_call(
        functools.partial(_merge_body, has_pos_embed=has_pos),
        grid=(b, l // tl),
        in_specs=[row(), sec(lg_sec), row(), row(),
                  sec(gate_sec), sec(gate_sec + 1), sec(gate_sec + 2), row()] + pos_spec
        + [sec(sc_sec), sec(sc_sec + 1), sec(sc_sec + 2),
           before(sc_sec + 1), before(sc_sec + 2), after(sc_sec + 1), after(sc_sec + 2),
           pl.BlockSpec(sc_w.shape, lambda bb, i: (0, 0)),
           modv(), pl.BlockSpec((1, d), lambda bb, i: (0, 0)), modv(), modv(),
           wsp(), wsp(), wsp(), wsp(), _resident((d, e), lambda bb, i: (0, 0))],
        out_specs=[row(), row(), pl.BlockSpec((None, tl, e), lambda bb, i: (bb, i, 0))],
        out_shape=[jax.ShapeDtypeStruct((b, l, d), F32),
                   jax.ShapeDtypeStruct((b, l, d), BF16),
                   jax.ShapeDtypeStruct((b, l, e), F32)],
        scratch_shapes=[pltpu.VMEM((d // LANES, tl + 2 * SC_HALO, LANES), F32)],
        compiler_params=_cparams(2),
        name="merge",
    )(ya, proj, hf, hb, proj, proj, proj, x, *pos_arg, proj, proj, proj, proj, proj, proj, proj, sc_w,
      mod_gate, g2.reshape(1, d), shift2, scale2, w_a, w_b, w_c, w_o, w_r)


def _select_body(aff_ref, tri_ref, rank_ref, *, cap):
    aff = aff_ref[...]

    def refine(i, thr_bits):
        cand = thr_bits | jnp.left_shift(jnp.int32(1), 30 - i)
        cnt = jnp.sum((aff >= lax.bitcast_convert_type(cand, F32)).astype(F32), axis=1, keepdims=True)
        return jnp.where(cnt >= cap, cand, thr_bits)

    thr_bits = lax.fori_loop(0, 31, refine, jnp.zeros((aff.shape[0], 1), jnp.int32))
    thr = lax.bitcast_convert_type(thr_bits, F32)
    gt = aff > thr
    eq = aff == thr
    n_gt = jnp.sum(gt.astype(F32), axis=1, keepdims=True)
    eq_rank = jnp.dot(eq.astype(BF16), tri_ref[...], preferred_element_type=F32)
    sel = gt | (eq & (eq_rank <= cap - n_gt))
    rank = jnp.dot(sel.astype(BF16), tri_ref[...], preferred_element_type=F32) - 1.0
    rank_ref[...] = jnp.where(sel, rank, -1.0).astype(jnp.int32)


def select_tokens(aff_t, cap):
    r, n = aff_t.shape
    tri = jnp.asarray(_prefix_table(n), BF16)
    return pl.pallas_call(
        functools.partial(_select_body, cap=cap),
        grid=(1,),
        in_specs=[pl.BlockSpec((r, n), lambda i: (0, 0)), pl.BlockSpec((n, n), lambda i: (0, 0))],
        out_specs=pl.BlockSpec((r, n), lambda i: (0, 0)),
        out_shape=jax.ShapeDtypeStruct((r, n), jnp.int32),
        compiler_params=_cparams(1),
        name="select",
    )(aff_t, tri)


GATHER_GROUP = 8


def _gather_body(rank_ref, hs_ref, o_ref):
    grp, cap, d = o_ref.shape
    n = hs_ref.shape[0]
    g0 = pl.multiple_of(pl.program_id(1) * grp, grp)
    ranks = rank_ref[pl.ds(g0, grp), :]
    slot = lax.broadcasted_iota(jnp.int32, (cap, n), 0)
    onehot = jnp.concatenate([jnp.where(slot == ranks[e:e + 1, :], 1.0, 0.0).astype(BF16) for e in range(grp)],
                             axis=0)
    rows = jnp.dot(onehot, hs_ref[...], preferred_element_type=F32)
    o_ref[...] = rows.reshape(grp, cap, d).astype(o_ref.dtype)


def gather_tokens(rank_t, hs, cap):
    s, e, n = rank_t.shape
    d = hs.shape[2]
    return pl.pallas_call(
        _gather_body,
        grid=(s, e // GATHER_GROUP),
        in_specs=[pl.BlockSpec((None, e, n), lambda ss, ee: (ss, 0, 0)),
                  pl.BlockSpec((None, n, d), lambda ss, ee: (ss, 0, 0))],
        out_specs=pl.BlockSpec((GATHER_GROUP, cap, d), lambda ss, ee: (ee, ss, 0)),
        out_shape=jax.ShapeDtypeStruct((e, s * cap, d), BF16),
        compiler_params=_cparams(2),
        name="gather",
    )(rank_t, hs)


FFN_SPLIT = 2
FFN_SPLIT_MIN_ROWS = 1024


def _ffn_body(*refs):
    x_refs, (wg_ref, wu_ref, wd_ref, o_ref, acc_ref) = refs[:-5], refs[-5:]
    f = pl.program_id(1)

    @pl.when(f == 0)
    def _():
        acc_ref[...] = jnp.zeros_like(acc_ref)

    wg = wg_ref[...].astype(BF16)
    wu = wu_ref[...].astype(BF16)
    wd = wd_ref[...].astype(BF16)
    row0 = 0
    for x_ref in x_refs:
        m = x_ref.shape[0]
        split = FFN_SPLIT if m >= FFN_SPLIT_MIN_ROWS else 1
        step = m // split
        for part in range(split):
            x = x_ref[pl.ds(part * step, step), :]
            g = jnp.dot(x, wg, preferred_element_type=F32)
            u = jnp.dot(x, wu, preferred_element_type=F32)
            hid = (g * _sigmoid(g) * u).astype(BF16)
            acc_ref[pl.ds(row0 + part * step, step), :] += jnp.dot(hid, wd, preferred_element_type=F32)
        row0 += m

    @pl.when(f == pl.num_programs(1) - 1)
    def _():
        o_ref[...] = acc_ref[...].astype(o_ref.dtype)


def expert_ffn(xs_list, wg, wu, wd, layer):
    e, _, d = xs_list[0].shape
    m = sum(x.shape[1] for x in xs_list)
    fdim = wg.shape[3]
    tf = _tile(fdim, 256)
    return pl.pallas_call(
        _ffn_body,
        grid=(e, fdim // tf),
        in_specs=[pl.BlockSpec((None, x.shape[1], d), lambda ee, f: (ee, 0, 0)) for x in xs_list]
        + [pl.BlockSpec((None, None, d, tf), lambda ee, f: (layer, ee, 0, f)),
           pl.BlockSpec((None, None, d, tf), lambda ee, f: (layer, ee, 0, f)),
           pl.BlockSpec((None, None, tf, d), lambda ee, f: (layer, ee, f, 0))],
        out_specs=pl.BlockSpec((None, m, d), lambda ee, f: (ee, 0, 0)),
        out_shape=jax.ShapeDtypeStruct((e, m, d), BF16),
        scratch_shapes=[pltpu.VMEM((m, d), F32)],
        compiler_params=_cparams(2),
        name="ffn",
    )(*xs_list, wg, wu, wd)


def _scatter_body(rank_ref, aff_ref, ys_ref, x_ref, mg_ref, gf_ref, o_ref, *, final_norm):
    tq = rank_ref.shape[0]
    n_exp, cap, _ = ys_ref.shape
    slot = lax.broadcasted_iota(jnp.int32, (tq, cap), 1)
    rank = rank_ref[...]
    aff = aff_ref[...]
    q = jnp.concatenate([jnp.where(slot == rank[:, e:e + 1], aff[:, e:e + 1], 0.0).astype(BF16)
                         for e in range(n_exp)], axis=1)
    acc = jnp.dot(q, ys_ref[...].reshape(n_exp * cap, ys_ref.shape[2]), preferred_element_type=F32)
    x = x_ref[...] + mg_ref[...] * acc
    if final_norm:
        x = x * lax.rsqrt(jnp.mean(x * x, axis=-1, keepdims=True) + NORM_EPS) * gf_ref[...]
    o_ref[...] = x


def scatter_residual(rank, aff, ys, row_off, cap, x, mod_gate, final_g):
    s, n, e = rank.shape
    d = x.shape[2]
    tq = _tile(n, 1024)
    boff = row_off // cap
    final_norm = final_g is not None
    gf = (final_g if final_norm else jnp.ones((d,), F32)).reshape(1, d)
    return pl.pallas_call(
        functools.partial(_scatter_body, final_norm=final_norm),
        grid=(s, n // tq),
        in_specs=[pl.BlockSpec((None, tq, e), lambda ss, i: (ss, i, 0)),
                  pl.BlockSpec((None, tq, e), lambda ss, i: (ss, i, 0)),
                  pl.BlockSpec((e, cap, d), lambda ss, i: (0, boff + ss, 0)),
                  pl.BlockSpec((None, tq, d), lambda ss, i: (ss, i, 0)),
                  pl.BlockSpec((None, 1, d), lambda ss, i: (ss, 0, 0)),
                  pl.BlockSpec((1, d), lambda ss, i: (0, 0))],
        out_specs=pl.BlockSpec((None, tq, d), lambda ss, i: (ss, i, 0)),
        out_shape=jax.ShapeDtypeStruct(x.shape, F32),
        compiler_params=_cparams(2),
        name="scatter",
    )(rank, aff, ys, x, mod_gate, gf)


def _mixer_and_route(x, h0_f, h0_b, mods, lw, fs, shared_mod=False, pos_embed=None):
    b, l, d = x.shape
    if pos_embed is not None:
        proj = norm_proj(x, lw['norm1_g'], mods[0], mods[1], lw['w_in'], pos_embed)
    elif shared_mod:
        proj = norm_proj(x.reshape(1, b * l, d), lw['norm1_g'], mods[0][:1], mods[1][:1],
                         lw['w_in']).reshape(b, l, -1)
    else:
        proj = norm_proj(x, lw['norm1_g'], mods[0], mods[1], lw['w_in'])
    t1, t2, mid = hy_filter_tables(l, lw['hy_filt_w1'], lw['hy_filt_b1'], lw['hy_filt_w2'], lw['hy_filt_b2'],
                                  lw['hy_filt_w3'], lw['hy_filt_freq'], fs)
    ya = hyena_mixer(proj, lw['hy_conv_w'], lw['hy_conv_b'], lw['hy_bias'], t1, t2, mid, fs)
    hf, hb = lru_mixer(proj, 4, lw['lru_conv_w'], lw['lru_conv_b'], lw['lru_wa'], lw['lru_ba'],
                       lw['lru_wx'], lw['lru_bx'], lw['lru_lambda'], h0_f, h0_b)
    x_new, h2, aff = merge_mixers(proj, 8, 3, 5, lw['sc_conv_w'], ya, hf, hb, x, mods[2],
                                  lw['norm2_g'], mods[3], mods[4],
                                  lw['w_hy_out'], lw['w_lru_out'], lw['w_sc_out'], lw['w_o'], lw['w_router'],
                                  pos_embed)
    return x_new, h2, aff, hf[:, -1], hb[:, 0]


def _capacity(n, e):
    return EC_CAPACITY * n // e


def _route(h2, aff):
    s, n, e = aff.shape
    cap = _capacity(n, e)
    aff_t = jnp.transpose(aff, (0, 2, 1))
    rank_t = select_tokens(aff_t.reshape(s * e, n), cap).reshape(s, e, n)
    return gather_tokens(rank_t, h2, cap), jnp.transpose(rank_t, (0, 2, 1))


def kernel(x, c, ctx, c_ctx, w_ada, b_ada, norm1_g, norm2_g, w_in, hy_conv_w, hy_conv_b, hy_filt_w1, hy_filt_b1, hy_filt_w2, hy_filt_b2, hy_filt_w3, hy_filt_freq, hy_bias, lru_conv_w, lru_conv_b, lru_wa, lru_ba, lru_wx, lru_bx, lru_lambda, sc_conv_w, w_hy_out, w_lru_out, w_sc_out, w_o, w_router, w_exp_gate, w_exp_up, w_exp_down, final_norm_g):
    bsz, n_lat, d = x.shape
    n_ctx = ctx.shape[1]
    depth = w_ada.shape[0]
    rows = n_lat // GRID_W

    pos_embed = jnp.asarray(_pos_embed(rows, d))
    xc = ctx
    fs_lat = jnp.asarray(_dft_table(n_lat)).astype(BF16)
    fs_ctx = jnp.asarray(_dft_table(n_ctx)).astype(BF16)

    cc = jnp.zeros((2 * bsz, d), F32).at[:bsz].set(c).at[bsz].set(c_ctx)
    mods = ada_mods(cc, w_ada, b_ada)
    zeros_state = jnp.zeros((bsz, d), F32)

    for l in range(depth):
        last = l == depth - 1
        lw = {
            'norm1_g': norm1_g[l], 'norm2_g': norm2_g[l], 'w_in': w_in[l].astype(BF16),
            'hy_conv_w': hy_conv_w[l], 'hy_conv_b': hy_conv_b[l],
            'hy_filt_w1': hy_filt_w1[l], 'hy_filt_b1': hy_filt_b1[l], 'hy_filt_w2': hy_filt_w2[l],
            'hy_filt_b2': hy_filt_b2[l], 'hy_filt_w3': hy_filt_w3[l], 'hy_filt_freq': hy_filt_freq[l],
            'hy_bias': hy_bias[l], 'lru_conv_w': lru_conv_w[l], 'lru_conv_b': lru_conv_b[l],
            'lru_wa': lru_wa[l], 'lru_ba': lru_ba[l], 'lru_wx': lru_wx[l], 'lru_bx': lru_bx[l],
            'lru_lambda': lru_lambda[l], 'sc_conv_w': sc_conv_w[l],
            'w_hy_out': w_hy_out[l].astype(BF16), 'w_lru_out': w_lru_out[l].astype(BF16),
            'w_sc_out': w_sc_out[l].astype(BF16), 'w_o': w_o[l].astype(BF16), 'w_router': w_router[l],
        }
        mod_l = [mods[l, :bsz, k * d:(k + 1) * d].reshape(bsz, 1, d) for k in range(6)]
        mod_c = [jnp.broadcast_to(mods[l, bsz, k * d:(k + 1) * d].reshape(1, 1, d), (bsz, 1, d)) for k in range(6)]

        n_exp = w_router.shape[2]
        cap_l = _capacity(n_lat, n_exp)
        cap_c = _capacity(n_ctx, n_exp)
        rows_l = bsz * cap_l
        if last:
            proj_c = norm_proj(xc.reshape(1, bsz * n_ctx, d), lw['norm1_g'], mod_c[0][:1], mod_c[1][:1],
                               lw['w_in'][:, 4 * d:5 * d]).reshape(bsz, n_ctx, d)
            hf_c, hb_c = lru_mixer(proj_c, 0, lw['lru_conv_w'], lw['lru_conv_b'], lw['lru_wa'], lw['lru_ba'],
                                   lw['lru_wx'], lw['lru_bx'], lw['lru_lambda'], zeros_state, zeros_state)
            state_f, state_b = hf_c[:, -1], hb_c[:, 0]
            xs_streams = []
        else:
            xc_mid, h2_c, aff_c, state_f, state_b = _mixer_and_route(xc, zeros_state, zeros_state, mod_c, lw, fs_ctx,
                                                                     shared_mod=True)
            xs_c, rank_c = _route(h2_c, aff_c)
            xs_streams = [xs_c]

        x_mid, h2_l, aff_l, _, _ = _mixer_and_route(x, state_f, state_b, mod_l, lw, fs_lat,
                                                    pos_embed=pos_embed if l == 0 else None)
        xs_l, rank_l = _route(h2_l, aff_l)
        ys = expert_ffn([xs_l] + xs_streams, w_exp_gate, w_exp_up, w_exp_down, l)

        if not last:
            xc = scatter_residual(rank_c, aff_c, ys, rows_l, cap_c, xc_mid, mod_c[5], None)
        x = scatter_residual(rank_l, aff_l, ys, 0, cap_l, x_mid, mod_l[5], final_norm_g if last else None)
    return x
```
